```python
import math
import jax, jax.numpy as jnp
from jax import lax
import numpy as np

D_MODEL = 2048
BATCH = 2
SEQ = 4096
DEPTH = 1
DEC_BATCH = 8
DEC_SEQ = 4
PAST_LEN = 16384
PAGE_SIZE = 128

N_META = 16
POOL_WIDTH = D_MODEL // 2
POOL_WINDOWS = (2, 4, 8, 16)
N_POOL_GROUPS = len(POOL_WINDOWS)
POOL_GROUP_DIM = POOL_WIDTH // N_POOL_GROUPS
POOL_BUF = max(POOL_WINDOWS) - 1
ATTN_WIDTH = D_MODEL - POOL_WIDTH
N_HEADS = 8
HEAD_DIM = ATTN_WIDTH // N_HEADS
QK_HALF = HEAD_DIM // 2
D_FF = ((8 * D_MODEL // 3 + 255) // 256) * 256
Q_BLOCK = 128
EPS = 1e-6
NEG_INF = -1e30

kernel_name = 'hymba_pool_diffattn_macaron_step'


def rmsnorm(x, g):
    xf = x.astype(jnp.float32)
    y = xf * lax.rsqrt(jnp.mean(xf * xf, axis=-1, keepdims=True) + EPS)
    return (y * g.astype(jnp.float32)).astype(x.dtype)


def swiglu_half(x, g, w_gate, w_up, w_down):
    h = rmsnorm(x, g)
    return x + 0.5 * ((jax.nn.silu(h @ w_gate) * (h @ w_up)) @ w_down)


def alibi_slopes():
    return jnp.asarray(2.0 ** (-8.0 * np.arange(1, N_HEADS + 1) / N_HEADS), dtype=jnp.float32)


def multiscale_pool(p, buf, pos0, w_pool, pool_scale):
    B, T, _ = p.shape
    xp = jnp.concatenate([buf.astype(p.dtype), p], axis=1)
    cs = jnp.cumsum(xp.astype(jnp.float32), axis=1)
    cs = jnp.concatenate([jnp.zeros((B, 1, POOL_WIDTH), jnp.float32), cs], axis=1)
    cs = cs.reshape(B, POOL_BUF + T + 1, N_POOL_GROUPS, POOL_GROUP_DIM)
    end = cs[:, POOL_BUF + 1:]
    start = jnp.stack([cs[:, POOL_BUF + 1 - w:POOL_BUF + 1 - w + T, gi]
                       for gi, w in enumerate(POOL_WINDOWS)], axis=2)
    pos = pos0 + jnp.arange(T)
    win = jnp.asarray(POOL_WINDOWS, jnp.float32)
    count = jnp.minimum((pos + 1).astype(jnp.float32)[:, None], win[None, :])
    feat = (end - start) / count[None, :, :, None] - p.astype(jnp.float32).reshape(B, T, N_POOL_GROUPS, POOL_GROUP_DIM)
    out = jnp.einsum('btgc,gcd->btgd', feat.astype(p.dtype), w_pool).reshape(B, T, POOL_WIDTH) * pool_scale
    return out.astype(p.dtype), xp[:, -POOL_BUF:]


def diff_attention(q, k, v, q_pos, k_pos, lam, lam_init, subln_gain):
    scale = QK_HALF ** -0.5
    s1 = jnp.einsum('bqhd,bkhd->bhqk', q[..., :QK_HALF], k[..., :QK_HALF]).astype(jnp.float32) * scale
    s2 = jnp.einsum('bqhd,bkhd->bhqk', q[..., QK_HALF:], k[..., QK_HALF:]).astype(jnp.float32) * scale
    dist = q_pos[:, None] - k_pos[None, :]
    bias = -alibi_slopes()[:, None, None] * dist.astype(jnp.float32)[None]
    mask = (dist >= 0)[None, None]
    a1 = jax.nn.softmax(jnp.where(mask, s1 + bias, NEG_INF), axis=-1)
    a2 = jax.nn.softmax(jnp.where(mask, s2 + bias, NEG_INF), axis=-1)
    o = jnp.einsum('bhqk,bkhd->bqhd', a1 - lam * a2, v.astype(jnp.float32))
    o = rmsnorm(o, subln_gain) * (1.0 - lam_init)
    return o.astype(q.dtype)


def token_mixer(h, pos0, pool_buf, past_k, past_v, lam_init, w_in, w_pool, pool_scale,
                lambda_q1, lambda_k1, lambda_q2, lambda_k2, subln_gain, w_out):
    B, T, _ = h.shape
    proj = h @ w_in
    p = proj[..., :POOL_WIDTH]
    q = proj[..., POOL_WIDTH:POOL_WIDTH + ATTN_WIDTH].reshape(B, T, N_HEADS, HEAD_DIM)
    k = proj[..., POOL_WIDTH + ATTN_WIDTH:POOL_WIDTH + 2 * ATTN_WIDTH].reshape(B, T, N_HEADS, HEAD_DIM)
    v = proj[..., POOL_WIDTH + 2 * ATTN_WIDTH:].reshape(B, T, N_HEADS, HEAD_DIM)
    pool_out, new_buf = multiscale_pool(p, pool_buf, pos0, w_pool, pool_scale)
    if past_k is None:
        keys, values = k, v
    else:
        keys = jnp.concatenate([past_k.astype(k.dtype), k], axis=1)
        values = jnp.concatenate([past_v.astype(v.dtype), v], axis=1)
    k_pos = jnp.arange(keys.shape[1])
    lam = (jnp.exp(jnp.sum(lambda_q1.astype(jnp.float32) * lambda_k1.astype(jnp.float32)))
           - jnp.exp(jnp.sum(lambda_q2.astype(jnp.float32) * lambda_k2.astype(jnp.float32))) + lam_init)
    qblk = min(Q_BLOCK, T)
    n_blk = -(-T // qblk)
    pad = n_blk * qblk - T
    qb = jnp.pad(q, ((0, 0), (0, pad), (0, 0), (0, 0))).reshape(B, n_blk, qblk, N_HEADS, HEAD_DIM).transpose(1, 0, 2, 3, 4)
    posb = (pos0 + jnp.arange(n_blk * qblk)).reshape(n_blk, qblk)

    def attend(args):
        q_blk, q_pos = args
        return diff_attention(q_blk, keys, values, q_pos, k_pos, lam, lam_init, subln_gain)

    ob = lax.map(attend, (qb, posb))
    o = ob.transpose(1, 0, 2, 3, 4).reshape(B, n_blk * qblk, ATTN_WIDTH)[:, :T]
    mixed = jnp.concatenate([pool_out, o.astype(h.dtype)], axis=-1) @ w_out
    return mixed, k, v, new_buf


def decoder_layer(x, pos0, pool_buf, past_k, past_v, lam_init,
                  norm_ffn1, w_gate1, w_up1, w_down1, norm_mix, w_in, w_pool, pool_scale,
                  lambda_q1, lambda_k1, lambda_q2, lambda_k2, subln_gain, w_out,
                  norm_ffn2, w_gate2, w_up2, w_down2):
    x = swiglu_half(x, norm_ffn1, w_gate1, w_up1, w_down1)
    mixed, k_new, v_new, new_buf = token_mixer(rmsnorm(x, norm_mix), pos0, pool_buf, past_k, past_v, lam_init,
                                               w_in, w_pool, pool_scale, lambda_q1, lambda_k1, lambda_q2,
                                               lambda_k2, subln_gain, w_out)
    x = x + mixed
    x = swiglu_half(x, norm_ffn2, w_gate2, w_up2, w_down2)
    return x, k_new, v_new, new_buf


def setup_inputs(seed: int = 0) -> dict:
    key = jax.random.key(seed)
    ks = jax.random.split(key, 32)
    f32 = jnp.float32

    def nrm(k, shape, scale):
        return jax.random.normal(k, shape, f32) * scale

    n_pages = PAST_LEN // PAGE_SIZE
    n_used = DEC_BATCH * n_pages
    n_pool = n_used + max(1, n_used // 4)
    page_table = jax.random.permutation(ks[0], n_pool)[:n_used].reshape(DEC_BATCH, n_pages).astype(jnp.int32)
    return {
        'x_prompt': nrm(ks[1], (BATCH, SEQ, D_MODEL), 1.0),
        'x_sample': nrm(ks[2], (DEC_BATCH, DEC_SEQ, D_MODEL), 1.0),
        'cache_k': nrm(ks[3], (DEPTH, n_pool, PAGE_SIZE, N_HEADS, HEAD_DIM), 1.0),
        'cache_v': nrm(ks[4], (DEPTH, n_pool, PAGE_SIZE, N_HEADS, HEAD_DIM), 1.0),
        'state_pool': nrm(ks[5], (DEPTH, DEC_BATCH, POOL_BUF, POOL_WIDTH), 1.0),
        'page_table': page_table,
        'meta_tokens': nrm(ks[6], (N_META, D_MODEL), 1.0),
        'norm_ffn1': 1.0 + nrm(ks[7], (DEPTH, D_MODEL), 0.02),
        'w_gate1': nrm(ks[8], (DEPTH, D_MODEL, D_FF), D_MODEL ** -0.5),
        'w_up1': nrm(ks[9], (DEPTH, D_MODEL, D_FF), D_MODEL ** -0.5),
        'w_down1': nrm(ks[10], (DEPTH, D_FF, D_MODEL), D_FF ** -0.5),
        'norm_mix': 1.0 + nrm(ks[11], (DEPTH, D_MODEL), 0.02),
        'w_in': nrm(ks[12], (DEPTH, D_MODEL, POOL_WIDTH + 3 * ATTN_WIDTH), D_MODEL ** -0.5),
        'w_pool': nrm(ks[13], (DEPTH, N_POOL_GROUPS, POOL_GROUP_DIM, POOL_GROUP_DIM), POOL_GROUP_DIM ** -0.5),
        'pool_scale': 1.0 + nrm(ks[14], (DEPTH, POOL_WIDTH), 0.02),
        'lambda_q1': nrm(ks[15], (DEPTH, QK_HALF), 0.1),
        'lambda_k1': nrm(ks[16], (DEPTH, QK_HALF), 0.1),
        'lambda_q2': nrm(ks[17], (DEPTH, QK_HALF), 0.1),
        'lambda_k2': nrm(ks[18], (DEPTH, QK_HALF), 0.1),
        'subln_gain': 1.0 + nrm(ks[19], (DEPTH, HEAD_DIM), 0.02),
        'w_out': nrm(ks[20], (DEPTH, POOL_WIDTH + ATTN_WIDTH, D_MODEL), (POOL_WIDTH + ATTN_WIDTH) ** -0.5),
        'norm_ffn2': 1.0 + nrm(ks[21], (DEPTH, D_MODEL), 0.02),
        'w_gate2': nrm(ks[22], (DEPTH, D_MODEL, D_FF), D_MODEL ** -0.5),
        'w_up2': nrm(ks[23], (DEPTH, D_MODEL, D_FF), D_MODEL ** -0.5),
        'w_down2': nrm(ks[24], (DEPTH, D_FF, D_MODEL), D_FF ** -0.5),
        'norm_final': 1.0 + nrm(ks[25], (D_MODEL,), 0.02),
    }


def reference(x_prompt, x_sample, cache_k, cache_v, state_pool, page_table, meta_tokens,
              norm_ffn1, w_gate1, w_up1, w_down1, norm_mix, w_in, w_pool, pool_scale,
              lambda_q1, lambda_k1, lambda_q2, lambda_k2, subln_gain, w_out,
              norm_ffn2, w_gate2, w_up2, w_down2, norm_final):
    b_prompt = x_prompt.shape[0]
    b_sample, n_pages = page_table.shape
    xp = jnp.concatenate([jnp.broadcast_to(meta_tokens.astype(x_prompt.dtype)[None], (b_prompt, N_META, D_MODEL)),
                          x_prompt], axis=1)
    xs = x_sample
    kp_l, vp_l, bp_l, ks_l, vs_l, bs_l = [], [], [], [], [], []
    for i in range(DEPTH):
        lam_init = 0.8 - 0.6 * math.exp(-0.3 * i)
        lw = (norm_ffn1[i], w_gate1[i], w_up1[i], w_down1[i], norm_mix[i], w_in[i], w_pool[i], pool_scale[i],
              lambda_q1[i], lambda_k1[i], lambda_q2[i], lambda_k2[i], subln_gain[i], w_out[i],
              norm_ffn2[i], w_gate2[i], w_up2[i], w_down2[i])
        zero_buf = jnp.zeros((b_prompt, POOL_BUF, POOL_WIDTH), xp.dtype)
        xp, kp, vp, bp = decoder_layer(xp, 0, zero_buf, None, None, lam_init, *lw)
        past_k = cache_k[i, page_table].reshape(b_sample, n_pages * PAGE_SIZE, N_HEADS, HEAD_DIM)
        past_v = cache_v[i, page_table].reshape(b_sample, n_pages * PAGE_SIZE, N_HEADS, HEAD_DIM)
        xs, kn, vn, bn = decoder_layer(xs, PAST_LEN, state_pool[i], past_k, past_v, lam_init, *lw)
        kp_l.append(kp); vp_l.append(vp); bp_l.append(bp)
        ks_l.append(kn); vs_l.append(vn); bs_l.append(bn)
    y_prompt = rmsnorm(xp[:, N_META:], norm_final)
    y_sample = rmsnorm(xs, norm_final)
    k_prompt = jnp.stack(kp_l)
    v_prompt = jnp.stack(vp_l)
    pool_prompt = jnp.stack(bp_l)
    k_sample = jnp.stack(ks_l)
    v_sample = jnp.stack(vs_l)
    pool_sample = jnp.stack(bs_l)
    return (y_prompt, y_sample, k_prompt, v_prompt, pool_prompt, k_sample, v_sample, pool_sample)
```

```python
import functools
import math

import jax
import jax.numpy as jnp
from jax import lax
from jax.experimental import pallas as pl
from jax.experimental.pallas import tpu as pltpu

F32 = jnp.float32
BF16 = jnp.bfloat16

N_META = 16
POOL_WINDOWS = (2, 4, 8, 16)
POOL_BUF = max(POOL_WINDOWS) - 1
HALO = 16
N_HEADS = 8
HEAD_DIM = 128
QK_HALF = HEAD_DIM // 2
EPS = 1e-6
NEG_INF = -1e30
SMALL_ROWS = 128

VMEM_LIMIT = 56 * 1024 * 1024


def _params(*sem):
    return pltpu.CompilerParams(dimension_semantics=sem, vmem_limit_bytes=VMEM_LIMIT)


def _rms(x, g):
    return x * lax.rsqrt(jnp.mean(x * x, axis=-1, keepdims=True) + EPS) * g


def _ffn_body(x_ref, g_ref, wg_ref, wu_ref, wd_ref, gf_ref, o_ref, h_ref, acc_ref, *, final_norm):
    j = pl.program_id(1)

    @pl.when(j == 0)
    def _():
        h_ref[...] = _rms(x_ref[...], g_ref[...]).astype(BF16)
        acc_ref[...] = jnp.zeros_like(acc_ref)

    h = h_ref[...]
    g = jnp.dot(h, wg_ref[...], preferred_element_type=F32)
    u = jnp.dot(h, wu_ref[...], preferred_element_type=F32)
    a = (g * jax.nn.sigmoid(g) * u).astype(BF16)
    acc_ref[...] += jnp.dot(a, wd_ref[...], preferred_element_type=F32)

    @pl.when(j == pl.num_programs(1) - 1)
    def _():
        y = x_ref[...] + 0.5 * acc_ref[...]
        if final_norm:
            y = _rms(y, gf_ref[...])
        o_ref[...] = y


def _ffn(x, g, wg, wu, wd, gf, *, final_norm, tm, tf):
    m, d = x.shape
    f = wg.shape[1]
    return pl.pallas_call(
        functools.partial(_ffn_body, final_norm=final_norm),
        grid=(m // tm, f // tf),
        in_specs=[
            pl.BlockSpec((tm, d), lambda i, j: (i, 0)),
            pl.BlockSpec((1, d), lambda i, j: (0, 0)),
            pl.BlockSpec((d, tf), lambda i, j: (0, j)),
            pl.BlockSpec((d, tf), lambda i, j: (0, j)),
            pl.BlockSpec((tf, d), lambda i, j: (j, 0)),
            pl.BlockSpec((1, d), lambda i, j: (0, 0)),
        ],
        out_specs=pl.BlockSpec((tm, d), lambda i, j: (i, 0)),
        out_shape=jax.ShapeDtypeStruct((m, d), F32),
        scratch_shapes=[pltpu.VMEM((tm, d), BF16), pltpu.VMEM((tm, d), F32)],
        compiler_params=_params("parallel", "arbitrary"),
        name="ffn",
    )(x, g, wg, wu, wd, gf)


def _proj_body(x_ref, g_ref, w_ref, p_ref, q_ref, k_ref, v_ref, kb_ref, vb_ref, h_ref):
    j = pl.program_id(1)

    @pl.when(j == 0)
    def _():
        h_ref[...] = _rms(x_ref[...], g_ref[...]).astype(BF16)

    r = jnp.dot(h_ref[...], w_ref[...], preferred_element_type=F32)

    @pl.when(j == 0)
    def _():
        p_ref[...] = r

    @pl.when(j == 1)
    def _():
        q_ref[...] = (r * (QK_HALF ** -0.5)).astype(BF16)

    @pl.when(j == 2)
    def _():
        k_ref[...] = r
        kb_ref[...] = r.astype(BF16)

    @pl.when(j == 3)
    def _():
        v_ref[...] = r
        vb_ref[...] = r.astype(BF16)


def _proj(x, g, w_in, *, tm):
    m, d = x.shape
    w = w_in.shape[1] // 4
    row = lambda i, j: (i, 0)
    return pl.pallas_call(
        _proj_body,
        grid=(m // tm, 4),
        in_specs=[
            pl.BlockSpec((tm, d), row),
            pl.BlockSpec((1, d), lambda i, j: (0, 0)),
            pl.BlockSpec((d, w), lambda i, j: (0, j)),
        ],
        out_specs=[pl.BlockSpec((tm, w), row)] * 6,
        out_shape=[
            jax.ShapeDtypeStruct((m, w), F32),
            jax.ShapeDtypeStruct((m, w), BF16),
            jax.ShapeDtypeStruct((m, w), F32),
            jax.ShapeDtypeStruct((m, w), F32),
            jax.ShapeDtypeStruct((m, w), BF16),
            jax.ShapeDtypeStruct((m, w), BF16),
        ],
        scratch_shapes=[pltpu.VMEM((tm, d), BF16)],
        compiler_params=_params("parallel", "arbitrary"),
        name="proj",
    )(x, g, w_in)


def _pool_groups(buf_ref, rows, wp_ref, sc_ref, count_fn, write):
    gd = wp_ref.shape[1]
    for gi, w in enumerate(POOL_WINDOWS):
        cs = slice(gi * gd, (gi + 1) * gd)
        cur = buf_ref[HALO:HALO + rows, cs]
        acc = cur
        for dlt in range(1, w):
            acc = acc + buf_ref[HALO - dlt:HALO - dlt + rows, cs]
        feat = acc / count_fn(w) - cur
        out = jnp.dot(feat.astype(BF16), wp_ref[gi], preferred_element_type=F32) * sc_ref[:, cs]
        write(cs, out)


def _pool_big_body(p_ref, halo_ref, pm_ref, wp_ref, sc_ref, o_ref, buf_ref):
    i = pl.program_id(1)
    rows = p_ref.shape[1]

    @pl.when(i == 0)
    def _():
        buf_ref[0:HALO, :] = pm_ref[...]

    @pl.when(i > 0)
    def _():
        buf_ref[0:HALO, :] = halo_ref[0]

    buf_ref[HALO:HALO + rows, :] = p_ref[0]

    def write(cs, val):
        o_ref[0, :, cs] = val.astype(BF16)

    _pool_groups(buf_ref, rows, wp_ref, sc_ref, lambda w: float(w), write)


def _pool_big(p, p_meta, w_pool, pool_scale, *, tm):
    b, t, c = p.shape
    hb = tm // HALO
    return pl.pallas_call(
        _pool_big_body,
        grid=(b, t // tm),
        in_specs=[
            pl.BlockSpec((1, tm, c), lambda bi, i: (bi, i, 0)),
            pl.BlockSpec((1, HALO, c), lambda bi, i: (bi, jnp.maximum(i * hb - 1, 0), 0)),
            pl.BlockSpec((HALO, c), lambda bi, i: (0, 0)),
            pl.BlockSpec(w_pool.shape, lambda bi, i: (0, 0, 0)),
            pl.BlockSpec((1, c), lambda bi, i: (0, 0)),
        ],
        out_specs=pl.BlockSpec((1, tm, c), lambda bi, i: (bi, i, 0)),
        out_shape=jax.ShapeDtypeStruct((b, t, c), BF16),
        scratch_shapes=[pltpu.VMEM((HALO + tm, c), F32)],
        compiler_params=_params("parallel", "arbitrary"),
        name="pool_big",
    )(p, p, p_meta, w_pool, pool_scale)


def _pool_small_body(p_ref, st_ref, wp_ref, sc_ref, o_ref, buf_ref, *, n_seq, t_dec):
    buf_ref[0:HALO, :] = jnp.zeros((HALO, buf_ref.shape[1]), F32)
    buf_ref[HALO:2 * HALO, :] = p_ref[0:N_META, :]
    pos1 = lax.broadcasted_iota(jnp.int32, (N_META, 1), 0).astype(F32) + 1.0

    def write_meta(cs, val):
        o_ref[0:N_META, cs] = val

    _pool_groups(buf_ref, N_META, wp_ref, sc_ref, lambda w: jnp.minimum(pos1, float(w)), write_meta)

    o_ref[N_META + n_seq * t_dec:, :] = jnp.zeros(
        (o_ref.shape[0] - N_META - n_seq * t_dec, o_ref.shape[1]), F32)
    for s in range(n_seq):
        r0 = N_META + s * t_dec
        buf_ref[0:HALO, :] = st_ref[s]
        buf_ref[HALO:HALO + 8, :] = p_ref[r0:r0 + 8, :]

        def write_s(cs, val, r0=r0):
            o_ref[r0:r0 + t_dec, cs] = val[0:t_dec]

        _pool_groups(buf_ref, 8, wp_ref, sc_ref, lambda w: float(w), write_s)


def _pool_small(p_small, state16, w_pool, pool_scale, *, n_seq, t_dec):
    m, c = p_small.shape
    return pl.pallas_call(
        functools.partial(_pool_small_body, n_seq=n_seq, t_dec=t_dec),
        out_shape=jax.ShapeDtypeStruct((m, c), F32),
        scratch_shapes=[pltpu.VMEM((2 * HALO, c), F32)],
        compiler_params=pltpu.CompilerParams(vmem_limit_bytes=VMEM_LIMIT),
        name="pool_small",
    )(p_small, state16, w_pool, pool_scale)


def _lambda(lq1_ref, lk1_ref, lq2_ref, lk2_ref, lam_init):
    a = jnp.sum(lq1_ref[...] * lk1_ref[...], axis=-1, keepdims=True)
    b = jnp.sum(lq2_ref[...] * lk2_ref[...], axis=-1, keepdims=True)
    return jnp.exp(a) - jnp.exp(b) + lam_init


def _split_halves(q):
    lane = lax.broadcasted_iota(jnp.int32, q.shape, q.ndim - 1) % HEAD_DIM
    zero = jnp.zeros_like(q)
    return jnp.where(lane < QK_HALF, q, zero), jnp.where(lane >= QK_HALF, q, zero)


def _nt_dot(a, b):
    return lax.dot_general(a, b, (((1,), (1,)), ((), ())), preferred_element_type=F32)


def _subln(o, gain, lam_init):
    return _rms(o, gain) * (1.0 - lam_init)


def _attn_big_body(slope_ref, lq1_ref, lk1_ref, lq2_ref, lk2_ref, gain_ref, rel_ref,
                   q_ref, k_ref, v_ref, km_ref, vm_ref, o_ref,
                   m_ref, l_ref, acc_ref, brel_ref, *, lam_init, tk):
    h = pl.program_id(1)
    qi = pl.program_id(2)
    tq = q_ref.shape[1]
    slope = slope_ref[h]
    lam = _lambda(lq1_ref, lk1_ref, lq2_ref, lk2_ref, lam_init)
    qs = _split_halves(q_ref[0])

    rel = rel_ref[...]
    brel_ref[...] = rel * slope
    m_ref[...] = jnp.full(m_ref.shape, NEG_INF, F32)
    l_ref[...] = jnp.zeros(l_ref.shape, F32)
    acc_ref[...] = jnp.zeros(acc_ref.shape, F32)

    def update(kb, vb, bias, shift, mask):
        for mi in range(2):
            s = _nt_dot(qs[mi], kb) + bias
            if mask is not None:
                s = jnp.where(mask, s, NEG_INF)
            m_prev = m_ref[mi]
            m_new = jnp.maximum(m_prev, jnp.max(s, axis=-1, keepdims=True) + shift)
            alpha = jnp.exp(m_prev - m_new)
            p = jnp.exp(s - (m_new - shift))
            l_ref[mi] = alpha * l_ref[mi] + jnp.sum(p, axis=-1, keepdims=True)
            acc_ref[mi] = alpha * acc_ref[mi] + jnp.dot(p.astype(BF16), vb, preferred_element_type=F32)
            m_ref[mi] = m_new

    q0 = (qi * tq).astype(F32)
    update(km_ref[...], vm_ref[...], brel_ref[:, 0:N_META], -slope * (q0 + N_META), None)

    def full_block(kj, carry):
        k0 = pl.multiple_of(kj * tk, tk)
        shift = slope * (k0.astype(F32) - q0)
        update(k_ref[0, pl.ds(k0, tk), :], v_ref[0, pl.ds(k0, tk), :], brel_ref[...], shift, None)
        return carry

    lax.fori_loop(0, qi, full_block, 0)
    k0 = pl.multiple_of(qi * tk, tk)
    update(k_ref[0, pl.ds(k0, tk), :], v_ref[0, pl.ds(k0, tk), :], brel_ref[...], 0.0, rel <= 0.0)

    o = acc_ref[0] / l_ref[0] - lam * (acc_ref[1] / l_ref[1])
    o_ref[0] = _subln(o, gain_ref[...], lam_init).astype(BF16)


def _attn_big(q, kb, vb, k_meta, v_meta, slopes, lams, gain, *, lam_init, tq):
    b, t, c = q.shape
    tk = tq
    rel = (jnp.arange(tk, dtype=F32)[None, :] - jnp.arange(tq, dtype=F32)[:, None])
    vec = lambda n: pl.BlockSpec((1, n), lambda bi, h, i: (0, 0))
    return pl.pallas_call(
        functools.partial(_attn_big_body, lam_init=lam_init, tk=tk),
        grid=(b, N_HEADS, t // tq),
        in_specs=[
            pl.BlockSpec(memory_space=pltpu.SMEM),
            vec(QK_HALF), vec(QK_HALF), vec(QK_HALF), vec(QK_HALF), vec(HEAD_DIM),
            pl.BlockSpec((tq, tk), lambda bi, h, i: (0, 0)),
            pl.BlockSpec((1, tq, HEAD_DIM), lambda bi, h, i: (bi, i, h)),
            pl.BlockSpec((1, t, HEAD_DIM), lambda bi, h, i: (bi, 0, h)),
            pl.BlockSpec((1, t, HEAD_DIM), lambda bi, h, i: (bi, 0, h)),
            pl.BlockSpec((N_META, HEAD_DIM), lambda bi, h, i: (0, h)),
            pl.BlockSpec((N_META, HEAD_DIM), lambda bi, h, i: (0, h)),
        ],
        out_specs=pl.BlockSpec((1, tq, HEAD_DIM), lambda bi, h, i: (bi, i, h)),
        out_shape=jax.ShapeDtypeStruct((b, t, c), BF16),
        scratch_shapes=[
            pltpu.VMEM((2, tq, 1), F32),
            pltpu.VMEM((2, tq, 1), F32),
            pltpu.VMEM((2, tq, HEAD_DIM), F32),
            pltpu.VMEM((tq, tk), F32),
        ],
        compiler_params=_params("parallel", "parallel", "arbitrary"),
        name="attn_big",
    )(slopes, *lams, gain, rel, q, kb, vb, k_meta, v_meta)


def _attn_meta_body(slope_ref, lq1_ref, lk1_ref, lq2_ref, lk2_ref, gain_ref,
                    q_ref, k_ref, v_ref, o_ref, *, lam_init):
    h = pl.program_id(0)
    slope = slope_ref[h]
    lam = _lambda(lq1_ref, lk1_ref, lq2_ref, lk2_ref, lam_init)
    qs = _split_halves(q_ref[...])
    kb = k_ref[...]
    shape = (N_META, kb.shape[0])
    qpos = lax.broadcasted_iota(jnp.int32, shape, 0)
    kpos = lax.broadcasted_iota(jnp.int32, shape, 1)
    dist = (qpos - kpos).astype(F32)
    mask = (qpos >= kpos) & (kpos < N_META)
    outs = []
    for mi in range(2):
        s = jnp.where(mask, _nt_dot(qs[mi], kb) - slope * dist, NEG_INF)
        p = jnp.exp(s - jnp.max(s, axis=-1, keepdims=True))
        p = p / jnp.sum(p, axis=-1, keepdims=True)
        outs.append(p)
    a = (outs[0] - lam * outs[1]).astype(BF16)
    o = jnp.dot(a, v_ref[...], preferred_element_type=F32)
    o_ref[...] = _subln(o, gain_ref[...], lam_init).astype(BF16)


def _attn_meta(q_s, kb_s, vb_s, slopes, lams, gain, *, lam_init):
    m, c = kb_s.shape
    vec = lambda n: pl.BlockSpec((1, n), lambda h: (0, 0))
    return pl.pallas_call(
        functools.partial(_attn_meta_body, lam_init=lam_init),
        grid=(N_HEADS,),
        in_specs=[
            pl.BlockSpec(memory_space=pltpu.SMEM),
            vec(QK_HALF), vec(QK_HALF), vec(QK_HALF), vec(QK_HALF), vec(HEAD_DIM),
            pl.BlockSpec((N_META, HEAD_DIM), lambda h: (0, h)),
            pl.BlockSpec((m, HEAD_DIM), lambda h: (0, h)),
            pl.BlockSpec((m, HEAD_DIM), lambda h: (0, h)),
        ],
        out_specs=pl.BlockSpec((N_META, HEAD_DIM), lambda h: (0, h)),
        out_shape=jax.ShapeDtypeStruct((N_META, c), BF16),
        compiler_params=_params("parallel"),
        name="attn_meta",
    )(slopes, *lams, gain, q_s, kb_s, vb_s)


def _attn_sample_body(pt_ref, lq1_ref, lk1_ref, lq2_ref, lk2_ref, gain_ref,
                      q_ref, kn_ref, vn_ref, kc_ref, vc_ref, o_ref,
                      qbd_ref, m_ref, l_ref, acc_ref, *, lam_init, t_dec, past_len):
    s_id = pl.program_id(0)
    pg = pl.program_id(1)
    page = kc_ref.shape[1]
    rows = 2 * t_dec * N_HEADS
    assert rows == qbd_ref.shape[0] and 2 * t_dec == 8
    shape = (rows, page)
    row = lax.broadcasted_iota(jnp.int32, shape, 0)
    lane = lax.broadcasted_iota(jnp.int32, shape, 1)
    r_q = row % t_dec
    slope = jnp.exp2(-((row // (2 * t_dec)) + 1).astype(F32))

    def update(kb, vb, bias, mask):
        s = _nt_dot(qbd_ref[...], kb) + bias
        if mask is not None:
            s = jnp.where(mask, s, NEG_INF)
        m_prev = m_ref[...]
        m_new = jnp.maximum(m_prev, jnp.max(s, axis=-1, keepdims=True))
        alpha = jnp.exp(m_prev - m_new)
        p = jnp.exp(s - m_new)
        l_ref[...] = alpha * l_ref[...] + jnp.sum(p, axis=-1, keepdims=True)
        acc_ref[...] = alpha * acc_ref[...] + jnp.dot(p.astype(BF16), vb, preferred_element_type=F32)
        m_ref[...] = m_new

    @pl.when(pg == 0)
    def _():
        q = q_ref[0]
        r2 = lax.broadcasted_iota(jnp.int32, q.shape, 0)
        c2 = lax.broadcasted_iota(jnp.int32, q.shape, 1)
        keep = (c2 // QK_HALF) == (r2 // (2 * t_dec)) * 2 + (r2 % (2 * t_dec)) // t_dec
        qbd_ref[...] = jnp.where(keep, q, jnp.zeros_like(q))
        m_ref[...] = jnp.full(m_ref.shape, NEG_INF, F32)
        l_ref[...] = jnp.zeros(l_ref.shape, F32)
        acc_ref[...] = jnp.zeros(acc_ref.shape, F32)
        r_k = lane - (N_META + s_id * t_dec)
        mask = (r_k >= 0) & (r_k <= r_q)
        update(kn_ref[...], vn_ref[...], -slope * (r_q - r_k).astype(F32), mask)

    kpos = pg * page + lane
    bias = -slope * (past_len + r_q - kpos).astype(F32)
    update(kc_ref[0].astype(BF16), vc_ref[0].astype(BF16), bias, None)

    @pl.when(pg == pl.num_programs(1) - 1)
    def _():
        lam = _lambda(lq1_ref, lk1_ref, lq2_ref, lk2_ref, lam_init)
        for hd in range(N_HEADS):
            rs = slice(hd * 2 * t_dec, (hd + 1) * 2 * t_dec)
            cs = slice(hd * HEAD_DIM, (hd + 1) * HEAD_DIM)
            n = acc_ref[rs, cs] / l_ref[rs, :]
            o = n - lam * pltpu.roll(n, t_dec, 0)
            o_ref[0, :, cs] = _subln(o, gain_ref[...], lam_init)


def _attn_sample(page_table, q_rep, kb_s, vb_s, cache_k, cache_v, lams, gain, *, lam_init, t_dec):
    n_seq, n_pages = page_table.shape
    _, page, c = cache_k.shape
    rows = 2 * t_dec * N_HEADS
    vec = lambda n: pl.BlockSpec((1, n), lambda s, p, pt: (0, 0))
    grid_spec = pltpu.PrefetchScalarGridSpec(
        num_scalar_prefetch=1,
        grid=(n_seq, n_pages),
        in_specs=[
            vec(QK_HALF), vec(QK_HALF), vec(QK_HALF), vec(QK_HALF), vec(HEAD_DIM),
            pl.BlockSpec((1, rows, c), lambda s, p, pt: (s, 0, 0)),
            pl.BlockSpec(kb_s.shape, lambda s, p, pt: (0, 0)),
            pl.BlockSpec(vb_s.shape, lambda s, p, pt: (0, 0)),
            pl.BlockSpec((1, page, c), lambda s, p, pt: (pt[s, p], 0, 0)),
            pl.BlockSpec((1, page, c), lambda s, p, pt: (pt[s, p], 0, 0)),
        ],
        out_specs=pl.BlockSpec((1, 2 * t_dec, c), lambda s, p, pt: (s, 0, 0)),
        scratch_shapes=[
            pltpu.VMEM((rows, c), BF16),
            pltpu.VMEM((rows, 1), F32),
            pltpu.VMEM((rows, 1), F32),
            pltpu.VMEM((rows, c), F32),
        ],
    )
    return pl.pallas_call(
        functools.partial(_attn_sample_body, lam_init=lam_init, t_dec=t_dec, past_len=n_pages * page),
        grid_spec=grid_spec,
        out_shape=jax.ShapeDtypeStruct((n_seq, 2 * t_dec, c), F32),
        compiler_params=_params("parallel", "arbitrary"),
        name="attn_sample",
    )(page_table, *lams, gain, q_rep, kb_s, vb_s, cache_k, cache_v)


def _outproj_body(x_ref, a_ref, b_ref, w_ref, o_ref):
    half = a_ref.shape[1]
    o_ref[...] = (x_ref[...]
                  + jnp.dot(a_ref[...], w_ref[0:half, :], preferred_element_type=F32)
                  + jnp.dot(b_ref[...], w_ref[half:, :], preferred_element_type=F32))


def _outproj(x, pool_out, attn_out, w_out, *, tm):
    m, d = x.shape
    c = pool_out.shape[1]
    return pl.pallas_call(
        _outproj_body,
        grid=(m // tm,),
        in_specs=[
            pl.BlockSpec((tm, d), lambda i: (i, 0)),
            pl.BlockSpec((tm, c), lambda i: (i, 0)),
            pl.BlockSpec((tm, c), lambda i: (i, 0)),
            pl.BlockSpec(w_out.shape, lambda i: (0, 0)),
        ],
        out_specs=pl.BlockSpec((tm, d), lambda i: (i, 0)),
        out_shape=jax.ShapeDtypeStruct((m, d), F32),
        compiler_params=_params("parallel"),
        name="outproj",
    )(x, pool_out, attn_out, w_out)


def kernel(x_prompt, x_sample, cache_k, cache_v, state_pool, page_table, meta_tokens,
           norm_ffn1, w_gate1, w_up1, w_down1, norm_mix, w_in, w_pool, pool_scale,
           lambda_q1, lambda_k1, lambda_q2, lambda_k2, subln_gain, w_out,
           norm_ffn2, w_gate2, w_up2, w_down2, norm_final):
    depth = w_in.shape[0]
    assert depth == 1
    bsz, seq, d = x_prompt.shape
    n_seq, t_dec, _ = x_sample.shape
    n_pool, page = cache_k.shape[1], cache_k.shape[2]
    c = N_HEADS * HEAD_DIM
    n_small = N_META + n_seq * t_dec
    assert n_small <= SMALL_ROWS and 2 * t_dec == 8
    lam_init = 0.8 - 0.6 * math.exp(-0.3 * 0)

    row = lambda a: a.reshape(1, -1)
    wg1, wu1, wd1 = w_gate1[0].astype(BF16), w_up1[0].astype(BF16), w_down1[0].astype(BF16)
    wg2, wu2, wd2 = w_gate2[0].astype(BF16), w_up2[0].astype(BF16), w_down2[0].astype(BF16)
    w_in_b, w_out_b, w_pool_b = w_in[0].astype(BF16), w_out[0].astype(BF16), w_pool[0].astype(BF16)
    g1, gm, g2, gf = row(norm_ffn1[0]), row(norm_mix[0]), row(norm_ffn2[0]), row(norm_final)
    pscale, gain = row(pool_scale[0]), row(subln_gain[0])
    lams = (row(lambda_q1[0]), row(lambda_k1[0]), row(lambda_q2[0]), row(lambda_k2[0]))
    slopes = jnp.exp2(-jnp.arange(1, N_HEADS + 1, dtype=F32))

    x_big = x_prompt.reshape(bsz * seq, d)
    x_small = jnp.concatenate([meta_tokens, x_sample.reshape(n_seq * t_dec, d),
                               jnp.zeros((SMALL_ROWS - n_small, d), F32)], axis=0)

    xs1 = _ffn(x_small, g1, wg1, wu1, wd1, gf, final_norm=False, tm=SMALL_ROWS, tf=512)
    ps, qs, ks, vs, kbs, vbs = _proj(xs1, gm, w_in_b, tm=SMALL_ROWS)
    state16 = jnp.pad(state_pool[0], ((0, 0), (HALO - POOL_BUF, 0), (0, 0)))
    pool_s = _pool_small(ps, state16, w_pool_b, pscale, n_seq=n_seq, t_dec=t_dec).astype(BF16)
    o_meta = _attn_meta(qs, kbs, vbs, slopes, lams, gain, lam_init=lam_init)
    q_dec = qs[N_META:n_small].reshape(n_seq, t_dec, c)
    q_rep = jnp.tile(q_dec, (1, 2 * N_HEADS, 1))
    o_dec = _attn_sample(page_table, q_rep, kbs, vbs, cache_k[0].reshape(n_pool, page, c),
                         cache_v[0].reshape(n_pool, page, c), lams, gain,
                         lam_init=lam_init, t_dec=t_dec)
    o_small = jnp.concatenate([o_meta, o_dec[:, :t_dec].reshape(n_seq * t_dec, c).astype(BF16),
                               jnp.zeros((SMALL_ROWS - n_small, c), BF16)], axis=0)
    xs2 = _outproj(xs1, pool_s, o_small, w_out_b, tm=SMALL_ROWS)
    ys = _ffn(xs2, g2, wg2, wu2, wd2, gf, final_norm=True, tm=SMALL_ROWS, tf=512)

    xb1 = _ffn(x_big, g1, wg1, wu1, wd1, gf, final_norm=False, tm=512, tf=512)
    pb, qb, kb, vb, kbb, vbb = _proj(xb1, gm, w_in_b, tm=512)
    pool_b = _pool_big(pb.reshape(bsz, seq, c), ps[:N_META], w_pool_b, pscale, tm=512)
    o_big = _attn_big(qb.reshape(bsz, seq, c), kbb.reshape(bsz, seq, c), vbb.reshape(bsz, seq, c),
                      kbs[:N_META], vbs[:N_META], slopes, lams, gain, lam_init=lam_init, tq=512)
    xb2 = _outproj(xb1, pool_b.reshape(bsz * seq, c), o_big.reshape(bsz * seq, c), w_out_b, tm=512)
    yb = _ffn(xb2, g2, wg2, wu2, wd2, gf, final_norm=True, tm=512, tf=512)

    def with_meta(small, big):
        meta = jnp.broadcast_to(small[None, :N_META], (bsz, N_META, c))
        full = jnp.concatenate([meta, big.reshape(bsz, seq, c)], axis=1)
        return full.reshape(1, bsz, N_META + seq, N_HEADS, HEAD_DIM)

    dec = lambda a: a[N_META:n_small].reshape(1, n_seq, t_dec, N_HEADS, HEAD_DIM)
    y_prompt = yb.reshape(bsz, seq, d)
    y_sample = ys[N_META:n_small].reshape(n_seq, t_dec, d)
    pool_prompt = pb.reshape(bsz, seq, c)[:, seq - POOL_BUF:][None]
    pool_sample = jnp.concatenate([state_pool[0][:, t_dec:], ps[N_META:n_small].reshape(n_seq, t_dec, c)],
                                  axis=1)[None]
    return (y_prompt, y_sample, with_meta(ks, kb), with_meta(vs, vb), pool_prompt,
            dec(ks), dec(vs), pool_sample)
```

```python
import functools
import math

import jax
import jax.numpy as jnp
from jax import lax
from jax.experimental import pallas as pl
from jax.experimental.pallas import tpu as pltpu

F32 = jnp.float32
BF16 = jnp.bfloat16

N_META = 16
POOL_WINDOWS = (2, 4, 8, 16)
POOL_BUF = max(POOL_WINDOWS) - 1
HALO = 16
N_HEADS = 8
HEAD_DIM = 128
QK_HALF = HEAD_DIM // 2
EPS = 1e-6
NEG_INF = -1e30
SMALL_ROWS = 128
ATTN_BLOCK = 256
V_ROWS = HEAD_DIM + 16
KEY_SUB = 128
PAGES_PER_STEP = 4

VMEM_LIMIT = 56 * 1024 * 1024


def _params(*sem):
    return pltpu.CompilerParams(dimension_semantics=sem, vmem_limit_bytes=VMEM_LIMIT)


def _rms(x, g):
    return x * lax.rsqrt(jnp.mean(x * x, axis=-1, keepdims=True) + EPS) * g


def _ffn_body(x_ref, g_ref, wg_ref, wu_ref, wd_ref, gf_ref, o_ref, h_ref, acc_ref, *, final_norm):
    j = pl.program_id(1)

    @pl.when(j == 0)
    def _():
        h_ref[...] = _rms(x_ref[...], g_ref[...]).astype(BF16)
        acc_ref[...] = jnp.zeros_like(acc_ref)

    h = h_ref[...]
    g = jnp.dot(h, wg_ref[...], preferred_element_type=F32)
    u = jnp.dot(h, wu_ref[...], preferred_element_type=F32)
    a = (g * jax.nn.sigmoid(g) * u).astype(BF16)
    acc_ref[...] += jnp.dot(a, wd_ref[...], preferred_element_type=F32)

    @pl.when(j == pl.num_programs(1) - 1)
    def _():
        y = x_ref[...] + 0.5 * acc_ref[...]
        if final_norm:
            y = _rms(y, gf_ref[...])
        o_ref[...] = y


def _ffn(x, g, wg, wu, wd, gf, *, final_norm, tm, tf):
    m, d = x.shape
    f = wg.shape[1]
    return pl.pallas_call(
        functools.partial(_ffn_body, final_norm=final_norm),
        grid=(m // tm, f // tf),
        in_specs=[
            pl.BlockSpec((tm, d), lambda i, j: (i, 0)),
            pl.BlockSpec((1, d), lambda i, j: (0, 0)),
            pl.BlockSpec((d, tf), lambda i, j: (0, j)),
            pl.BlockSpec((d, tf), lambda i, j: (0, j)),
            pl.BlockSpec((tf, d), lambda i, j: (j, 0)),
            pl.BlockSpec((1, d), lambda i, j: (0, 0)),
        ],
        out_specs=pl.BlockSpec((tm, d), lambda i, j: (i, 0)),
        out_shape=jax.ShapeDtypeStruct((m, d), F32),
        scratch_shapes=[pltpu.VMEM((tm, d), BF16), pltpu.VMEM((tm, d), F32)],
        compiler_params=_params("parallel", "arbitrary"),
        name="ffn",
    )(x, g, wg, wu, wd, gf)


def _proj_body(x_ref, g_ref, w_ref, p_ref, k_ref, v_ref, *rest, seq):
    j = pl.program_id(1)
    h_ref = rest[-1]

    @pl.when(j == 0)
    def _():
        h_ref[...] = _rms(x_ref[...], g_ref[...]).astype(BF16)

    r = jnp.dot(h_ref[...], w_ref[...], preferred_element_type=F32)

    def put_t(dst, val, extra_ones):
        tc = dst.shape[3]
        for hd in range(N_HEADS):
            for ci in range(dst.shape[1]):
                blk = val[ci * tc:(ci + 1) * tc, hd * HEAD_DIM:(hd + 1) * HEAD_DIM]
                dst[hd, ci, 0:HEAD_DIM, :] = blk.T.astype(BF16)
                if extra_ones:
                    rid = lax.broadcasted_iota(jnp.int32, (dst.shape[2] - HEAD_DIM, tc), 0)
                    dst[hd, ci, HEAD_DIM:, :] = jnp.where(rid == 0, 1.0, 0.0).astype(BF16)

    @pl.when(j == 0)
    def _():
        p_ref[...] = r

    @pl.when(j == 1)
    def _():
        q = r * (QK_HALF ** -0.5)
        if seq is None:
            rest[0][...] = q.astype(BF16)
        else:
            put_t(rest[2], q, False)

    @pl.when(j == 2)
    def _():
        k_ref[...] = r
        if seq is None:
            rest[1][...] = r.astype(BF16)
        else:
            tm = r.shape[0]
            rw = lax.broadcasted_iota(jnp.int32, r.shape, 0)
            cl = lax.broadcasted_iota(jnp.int32, r.shape, 1)
            pos = (pl.program_id(0) * tm + rw) % seq
            slope = jnp.exp2(-((cl // HEAD_DIM) + 1).astype(F32))
            hi = slope * ((pos // QK_HALF) * QK_HALF).astype(F32)
            lo = slope * (pos % QK_HALF).astype(F32)
            lane = cl % HEAD_DIM
            for mi in range(2):
                own = (lane < QK_HALF) if mi == 0 else (lane >= QK_HALF)
                a0 = QK_HALF * (1 - mi)
                aux = jnp.where(lane == a0, hi, jnp.where(lane == a0 + 1, lo, 0.0))
                rest[mi][...] = jnp.where(own, r, aux).astype(BF16)

    @pl.when(j == 3)
    def _():
        v_ref[...] = r
        if seq is None:
            rest[2][...] = r.astype(BF16)
        else:
            put_t(rest[3], r, True)


def _proj(x, g, w_in, *, tm, seq=None):
    m, d = x.shape
    w = w_in.shape[1] // 4
    row = lambda i, j: (i, 0)
    rm_spec = pl.BlockSpec((tm, w), row)
    rm_shape = jax.ShapeDtypeStruct((m, w), BF16)
    if seq is None:
        extra_specs = [rm_spec] * 3
        extra_shapes = [rm_shape] * 3
    else:
        tc = ATTN_BLOCK
        t_spec = lambda n: pl.BlockSpec((N_HEADS, tm // tc, n, tc), lambda i, j: (0, i, 0, 0))
        t_shape = lambda n: jax.ShapeDtypeStruct((N_HEADS, m // tc, n, tc), BF16)
        extra_specs = [rm_spec, rm_spec, t_spec(HEAD_DIM), t_spec(V_ROWS)]
        extra_shapes = [rm_shape, rm_shape, t_shape(HEAD_DIM), t_shape(V_ROWS)]
    f32_shape = jax.ShapeDtypeStruct((m, w), F32)
    return pl.pallas_call(
        functools.partial(_proj_body, seq=seq),
        grid=(m // tm, 4),
        in_specs=[
            pl.BlockSpec((tm, d), row),
            pl.BlockSpec((1, d), lambda i, j: (0, 0)),
            pl.BlockSpec((d, w), lambda i, j: (0, j)),
        ],
        out_specs=[rm_spec] * 3 + extra_specs,
        out_shape=[f32_shape] * 3 + extra_shapes,
        scratch_shapes=[pltpu.VMEM((tm, d), BF16)],
        compiler_params=_params("parallel", "arbitrary"),
        name="proj",
    )(x, g, w_in)


def _pool_groups(buf_ref, rows, wp_ref, sc_ref, count_fn, write):
    gd = wp_ref.shape[1]
    for gi, w in enumerate(POOL_WINDOWS):
        cs = slice(gi * gd, (gi + 1) * gd)
        cur = buf_ref[HALO:HALO + rows, cs]
        acc = cur
        for dlt in range(1, w):
            acc = acc + buf_ref[HALO - dlt:HALO - dlt + rows, cs]
        feat = acc / count_fn(w) - cur
        out = jnp.dot(feat.astype(BF16), wp_ref[gi], preferred_element_type=F32) * sc_ref[:, cs]
        write(cs, out)


def _pool_big_body(p_ref, halo_ref, pm_ref, wp_ref, sc_ref, o_ref, buf_ref):
    i = pl.program_id(1)
    rows = p_ref.shape[1]

    @pl.when(i == 0)
    def _():
        buf_ref[0:HALO, :] = pm_ref[...]

    @pl.when(i > 0)
    def _():
        buf_ref[0:HALO, :] = halo_ref[0]

    buf_ref[HALO:HALO + rows, :] = p_ref[0]

    def write(cs, val):
        o_ref[0, :, cs] = val.astype(BF16)

    _pool_groups(buf_ref, rows, wp_ref, sc_ref, lambda w: float(w), write)


def _pool_big(p, p_meta, w_pool, pool_scale, *, tm):
    b, t, c = p.shape
    hb = tm // HALO
    return pl.pallas_call(
        _pool_big_body,
        grid=(b, t // tm),
        in_specs=[
            pl.BlockSpec((1, tm, c), lambda bi, i: (bi, i, 0)),
            pl.BlockSpec((1, HALO, c), lambda bi, i: (bi, jnp.maximum(i * hb - 1, 0), 0)),
            pl.BlockSpec((HALO, c), lambda bi, i: (0, 0)),
            pl.BlockSpec(w_pool.shape, lambda bi, i: (0, 0, 0)),
            pl.BlockSpec((1, c), lambda bi, i: (0, 0)),
        ],
        out_specs=pl.BlockSpec((1, tm, c), lambda bi, i: (bi, i, 0)),
        out_shape=jax.ShapeDtypeStruct((b, t, c), BF16),
        scratch_shapes=[pltpu.VMEM((HALO + tm, c), F32)],
        compiler_params=_params("parallel", "arbitrary"),
        name="pool_big",
    )(p, p, p_meta, w_pool, pool_scale)


def _pool_small_body(p_ref, st_ref, wp_ref, sc_ref, o_ref, buf_ref, *, n_seq, t_dec):
    buf_ref[0:HALO, :] = jnp.zeros((HALO, buf_ref.shape[1]), F32)
    buf_ref[HALO:2 * HALO, :] = p_ref[0:N_META, :]
    pos1 = lax.broadcasted_iota(jnp.int32, (N_META, 1), 0).astype(F32) + 1.0

    def write_meta(cs, val):
        o_ref[0:N_META, cs] = val

    _pool_groups(buf_ref, N_META, wp_ref, sc_ref, lambda w: jnp.minimum(pos1, float(w)), write_meta)

    o_ref[N_META + n_seq * t_dec:, :] = jnp.zeros(
        (o_ref.shape[0] - N_META - n_seq * t_dec, o_ref.shape[1]), F32)
    for s in range(n_seq):
        r0 = N_META + s * t_dec
        buf_ref[0:HALO, :] = st_ref[s]
        buf_ref[HALO:HALO + 8, :] = p_ref[r0:r0 + 8, :]

        def write_s(cs, val, r0=r0):
            o_ref[r0:r0 + t_dec, cs] = val[0:t_dec]

        _pool_groups(buf_ref, 8, wp_ref, sc_ref, lambda w: float(w), write_s)


def _pool_small(p_small, state16, w_pool, pool_scale, *, n_seq, t_dec):
    m, c = p_small.shape
    return pl.pallas_call(
        functools.partial(_pool_small_body, n_seq=n_seq, t_dec=t_dec),
        out_shape=jax.ShapeDtypeStruct((m, c), F32),
        scratch_shapes=[pltpu.VMEM((2 * HALO, c), F32)],
        compiler_params=pltpu.CompilerParams(vmem_limit_bytes=VMEM_LIMIT),
        name="pool_small",
    )(p_small, state16, w_pool, pool_scale)


def _lambda(lq1_ref, lk1_ref, lq2_ref, lk2_ref, lam_init):
    a = jnp.sum(lq1_ref[...] * lk1_ref[...], axis=-1, keepdims=True)
    b = jnp.sum(lq2_ref[...] * lk2_ref[...], axis=-1, keepdims=True)
    return jnp.exp(a) - jnp.exp(b) + lam_init


def _split_halves(q, axis):
    idx = lax.broadcasted_iota(jnp.int32, q.shape, axis) % HEAD_DIM
    zero = jnp.zeros_like(q)
    return jnp.where(idx < QK_HALF, q, zero), jnp.where(idx >= QK_HALF, q, zero)


def _nt_dot(a, b):
    return lax.dot_general(a, b, (((1,), (1,)), ((), ())), preferred_element_type=F32)


def _attn_big_body(slope_ref, lq1_ref, lk1_ref, lq2_ref, lk2_ref, gain_ref,
                   qt_ref, k1_ref, k2_ref, vt_ref, km_ref, vmt_ref, o_ref,
                   m_ref, acc_ref, p_ref, al_ref, *, lam_init):
    h = pl.program_id(1)
    blk = ATTN_BLOCK
    n_blk = qt_ref.shape[1]
    slope = slope_ref[h]
    lam = _lambda(lq1_ref, lk1_ref, lq2_ref, lk2_ref, lam_init)
    diag = (lax.broadcasted_iota(jnp.int32, (blk, blk), 0)
            <= lax.broadcasted_iota(jnp.int32, (blk, blk), 1))
    meta_bias = slope * (lax.broadcasted_iota(jnp.int32, (N_META, blk), 0) - N_META).astype(F32)

    def q_pair(qj, carry):
        q_pure, q_aug = [], []
        for qh in range(2):
            q = qt_ref[0, 2 * qj + qh].astype(F32)
            rid = lax.broadcasted_iota(jnp.int32, q.shape, 0)
            q_lo = jnp.where(rid < QK_HALF, q, 0.0)
            q_hi = jnp.where(rid >= QK_HALF, q, 0.0)
            ones_lo = jnp.where(rid < 2, 1.0, 0.0)
            ones_hi = jnp.where((rid >= QK_HALF) & (rid < QK_HALF + 2), 1.0, 0.0)
            q_pure.append((q_lo.astype(BF16), q_hi.astype(BF16)))
            q_aug.append(((q_lo + ones_hi).astype(BF16), (q_hi + ones_lo).astype(BF16)))
        m_ref[...] = jnp.full(m_ref.shape, NEG_INF, F32)
        acc_ref[...] = jnp.zeros(acc_ref.shape, F32)
        p_ref[...] = jnp.zeros(p_ref.shape, BF16)
        al_ref[...] = jnp.ones(al_ref.shape, F32)

        def scores(slot, kbs, qsel, bias, targets):
            n_keys = kbs[0].shape[0]
            sub = min(KEY_SUB, n_keys)
            for ki in range(n_keys // sub):
                rs = slice(ki * sub, (ki + 1) * sub)
                for qh, mask in targets:
                    for mi in range(2):
                        st = 2 * qh + mi
                        s = jnp.dot(kbs[mi][rs], qsel[qh][mi], preferred_element_type=F32)
                        if bias is not None:
                            s = s + bias[rs]
                        if mask is not None:
                            s = jnp.where(mask[rs], s, NEG_INF)
                        m_prev = m_ref[st]
                        m_new = jnp.maximum(m_prev, jnp.max(s, axis=0, keepdims=True))
                        al_ref[slot, ki, st] = jnp.exp(m_prev - m_new)
                        p_ref[slot, st, rs, :] = jnp.exp(s - m_new).astype(BF16)
                        m_ref[st] = m_new

        def values(slot, vt, streams):
            n_keys = vt.shape[1]
            sub = min(KEY_SUB, n_keys)
            for ki in range(n_keys // sub):
                rs = slice(ki * sub, (ki + 1) * sub)
                for st in streams:
                    acc_ref[st] = al_ref[slot, ki, st] * acc_ref[st] + jnp.dot(
                        vt[:, rs], p_ref[slot, st, rs, :], preferred_element_type=F32)

        def keys(kj):
            k0 = pl.multiple_of(kj * blk, blk)
            return k1_ref[pl.ds(k0, blk), :], k2_ref[pl.ds(k0, blk), :]

        both = [(0, None), (1, None)]
        all_streams = (0, 1, 2, 3)

        def block_pair(t, c2):
            values(1, vt_ref[0, jnp.maximum(2 * t - 1, 0)], all_streams)
            scores(0, keys(2 * t), q_aug, None, both)
            values(0, vt_ref[0, 2 * t], all_streams)
            scores(1, keys(2 * t + 1), q_aug, None, both)
            return c2

        lax.fori_loop(0, qj, block_pair, 0)
        values(1, vt_ref[0, jnp.maximum(2 * qj - 1, 0)], all_streams)
        scores(0, keys(2 * qj), q_aug, None, [(0, diag), (1, None)])
        values(0, vt_ref[0, 2 * qj], all_streams)
        scores(1, keys(2 * qj + 1), q_aug, None, [(1, diag)])
        values(1, vt_ref[0, 2 * qj + 1], (2, 3))
        km = km_ref[...]
        scores(0, (km, km), q_pure, meta_bias, both)
        values(0, vmt_ref[0], all_streams)

        for qh in range(2):
            a1, a2 = acc_ref[2 * qh], acc_ref[2 * qh + 1]
            o = (a1[0:HEAD_DIM] / a1[HEAD_DIM:HEAD_DIM + 1]
                 - lam * (a2[0:HEAD_DIM] / a2[HEAD_DIM:HEAD_DIM + 1]))
            o = o * lax.rsqrt(jnp.mean(o * o, axis=0, keepdims=True) + EPS) * gain_ref[...]
            r0 = pl.multiple_of((2 * qj + qh) * blk, blk)
            o_ref[pl.ds(r0, blk), :] = (o * (1.0 - lam_init)).T.astype(BF16)
        return carry

    assert n_blk % 2 == 0
    lax.fori_loop(0, n_blk // 2, q_pair, 0)


def _attn_big(q_t, k1, k2, v_t, k_meta, vm_t, slopes, lams, gain_col, *, lam_init, bsz):
    n_heads, n_chunks, hd, blk = q_t.shape
    m, c = k1.shape
    seq = m // bsz
    n_blk = seq // blk
    vec = lambda n: pl.BlockSpec((1, n), lambda bi, h: (0, 0))
    return pl.pallas_call(
        functools.partial(_attn_big_body, lam_init=lam_init),
        grid=(bsz, n_heads),
        in_specs=[
            pl.BlockSpec(memory_space=pltpu.SMEM),
            vec(QK_HALF), vec(QK_HALF), vec(QK_HALF), vec(QK_HALF),
            pl.BlockSpec((hd, 1), lambda bi, h: (0, 0)),
            pl.BlockSpec((1, n_blk, hd, blk), lambda bi, h: (h, bi, 0, 0)),
            pl.BlockSpec((seq, hd), lambda bi, h: (bi, h)),
            pl.BlockSpec((seq, hd), lambda bi, h: (bi, h)),
            pl.BlockSpec((1, n_blk, V_ROWS, blk), lambda bi, h: (h, bi, 0, 0)),
            pl.BlockSpec((N_META, hd), lambda bi, h: (0, h)),
            pl.BlockSpec((1, V_ROWS, N_META), lambda bi, h: (h, 0, 0)),
        ],
        out_specs=pl.BlockSpec((seq, hd), lambda bi, h: (bi, h)),
        out_shape=jax.ShapeDtypeStruct((m, c), BF16),
        scratch_shapes=[
            pltpu.VMEM((4, 1, blk), F32),
            pltpu.VMEM((4, V_ROWS, blk), F32),
            pltpu.VMEM((2, 4, blk, blk), BF16),
            pltpu.VMEM((2, blk // KEY_SUB, 4, 1, blk), F32),
        ],
        compiler_params=_params("parallel", "arbitrary"),
        name="attn_big",
    )(slopes, *lams, gain_col, q_t, k1, k2, v_t, k_meta, vm_t)


def _attn_meta_body(slope_ref, lq1_ref, lk1_ref, lq2_ref, lk2_ref, gain_ref,
                    q_ref, k_ref, v_ref, o_ref, *, lam_init):
    h = pl.program_id(0)
    slope = slope_ref[h]
    lam = _lambda(lq1_ref, lk1_ref, lq2_ref, lk2_ref, lam_init)
    qs = _split_halves(q_ref[...], 1)
    kb = k_ref[...]
    shape = (N_META, kb.shape[0])
    qpos = lax.broadcasted_iota(jnp.int32, shape, 0)
    kpos = lax.broadcasted_iota(jnp.int32, shape, 1)
    dist = (qpos - kpos).astype(F32)
    mask = (qpos >= kpos) & (kpos < N_META)
    outs = []
    for mi in range(2):
        s = jnp.where(mask, _nt_dot(qs[mi], kb) - slope * dist, NEG_INF)
        p = jnp.exp(s - jnp.max(s, axis=-1, keepdims=True))
        p = p / jnp.sum(p, axis=-1, keepdims=True)
        outs.append(p)
    a = (outs[0] - lam * outs[1]).astype(BF16)
    o = jnp.dot(a, v_ref[...], preferred_element_type=F32)
    o_ref[...] = (_rms(o, gain_ref[...]) * (1.0 - lam_init)).astype(BF16)


def _attn_meta(q_s, kb_s, vb_s, slopes, lams, gain, *, lam_init):
    m, c = kb_s.shape
    vec = lambda n: pl.BlockSpec((1, n), lambda h: (0, 0))
    return pl.pallas_call(
        functools.partial(_attn_meta_body, lam_init=lam_init),
        grid=(N_HEADS,),
        in_specs=[
            pl.BlockSpec(memory_space=pltpu.SMEM),
            vec(QK_HALF), vec(QK_HALF), vec(QK_HALF), vec(QK_HALF), vec(HEAD_DIM),
            pl.BlockSpec((N_META, HEAD_DIM), lambda h: (0, h)),
            pl.BlockSpec((m, HEAD_DIM), lambda h: (0, h)),
            pl.BlockSpec((m, HEAD_DIM), lambda h: (0, h)),
        ],
        out_specs=pl.BlockSpec((N_META, HEAD_DIM), lambda h: (0, h)),
        out_shape=jax.ShapeDtypeStruct((N_META, c), BF16),
        compiler_params=_params("parallel"),
        name="attn_meta",
    )(slopes, *lams, gain, q_s, kb_s, vb_s)


def _attn_sample_body(pt_ref, lq1_ref, lk1_ref, lq2_ref, lk2_ref, gain_ref,
                      q_ref, kn_ref, vn_ref, *rest, lam_init, t_dec, past_len):
    n_pg = PAGES_PER_STEP
    k_refs, v_refs = rest[:n_pg], rest[n_pg:2 * n_pg]
    o_ref, qm_ref, m_ref, l_ref, acc_ref, bias_ref = rest[2 * n_pg:]
    pg = pl.program_id(1)
    page = k_refs[0].shape[0]
    grp = 2 * t_dec
    rows = grp * N_HEADS
    cols = page * N_HEADS
    row1 = lax.broadcasted_iota(jnp.int32, (rows, 1), 0)
    r_q = row1 % t_dec
    slope = jnp.exp2(-((row1 // grp) + 1).astype(F32))

    def update(kb, vb, bias, shift):
        s = _nt_dot(qm_ref[...], kb) + bias
        m_prev = m_ref[...]
        m_new = jnp.maximum(m_prev, jnp.max(s, axis=-1, keepdims=True) + shift)
        alpha = jnp.exp(m_prev - m_new)
        p = jnp.exp(s - (m_new - shift))
        l_ref[...] = alpha * l_ref[...] + jnp.sum(p, axis=-1, keepdims=True)
        acc_ref[...] = alpha * acc_ref[...] + jnp.dot(p.astype(BF16), vb, preferred_element_type=F32)
        m_ref[...] = m_new

    @pl.when(pg == 0)
    def _():
        q = q_ref[0]
        r2 = lax.broadcasted_iota(jnp.int32, q.shape, 0)
        c2 = lax.broadcasted_iota(jnp.int32, q.shape, 1)
        qm_ref[...] = jnp.where((c2 // QK_HALF) == (r2 % grp) // t_dec, q, jnp.zeros_like(q))
        m_ref[...] = jnp.full(m_ref.shape, NEG_INF, F32)
        l_ref[...] = jnp.zeros(l_ref.shape, F32)
        acc_ref[...] = jnp.zeros(acc_ref.shape, F32)
        rw = lax.broadcasted_iota(jnp.int32, (rows, cols), 0)
        cl = lax.broadcasted_iota(jnp.int32, (rows, cols), 1)
        bias_ref[...] = jnp.where((cl % N_HEADS) == (rw // grp),
                                  slope * (cl // N_HEADS).astype(F32), NEG_INF)
        rw = lax.broadcasted_iota(jnp.int32, (rows, t_dec * N_HEADS), 0)
        cl = lax.broadcasted_iota(jnp.int32, (rows, t_dec * N_HEADS), 1)
        r_k = cl // N_HEADS
        ok = ((cl % N_HEADS) == (rw // grp)) & (r_k <= rw % t_dec)
        bias_new = jnp.where(ok, -slope * (rw % t_dec - r_k).astype(F32), NEG_INF)
        update(kn_ref[0], vn_ref[0], bias_new, 0.0)

    qm = qm_ref[...]
    s_pg, top = [], None
    for gi in range(n_pg):
        start = (pg * n_pg + gi) * page
        shift = slope * (start - past_len - r_q).astype(F32)
        kb = k_refs[gi][...].reshape(cols, HEAD_DIM).astype(BF16)
        s = _nt_dot(qm, kb) + bias_ref[...]
        s_pg.append((s, shift))
        top_g = jnp.max(s, axis=-1, keepdims=True) + shift
        top = top_g if top is None else jnp.maximum(top, top_g)
    m_prev = m_ref[...]
    m_new = jnp.maximum(m_prev, top)
    alpha = jnp.exp(m_prev - m_new)
    l_new = alpha * l_ref[...]
    acc = alpha * acc_ref[...]
    for gi in range(n_pg):
        s, shift = s_pg[gi]
        p = jnp.exp(s + (shift - m_new))
        l_new = l_new + jnp.sum(p, axis=-1, keepdims=True)
        vb = v_refs[gi][...].reshape(cols, HEAD_DIM).astype(BF16)
        acc = acc + jnp.dot(p.astype(BF16), vb, preferred_element_type=F32)
    m_ref[...] = m_new
    l_ref[...] = l_new
    acc_ref[...] = acc

    @pl.when(pg == pl.num_programs(1) - 1)
    def _():
        lam = _lambda(lq1_ref, lk1_ref, lq2_ref, lk2_ref, lam_init)
        for hd in range(N_HEADS):
            rs = slice(hd * grp, (hd + 1) * grp)
            n = acc_ref[rs, :] / l_ref[rs, :]
            o = n - lam * pltpu.roll(n, t_dec, 0)
            o_ref[0, rs, :] = _rms(o, gain_ref[...]) * (1.0 - lam_init)


def _attn_sample(page_table, q_rows, k_new, v_new, cache_k, cache_v, lams, gain, *, lam_init, t_dec):
    n_seq, n_pages = page_table.shape
    _, _, page, n_heads, hd = cache_k.shape
    n_pg = PAGES_PER_STEP
    assert n_pages % n_pg == 0 and n_heads == N_HEADS and hd == HEAD_DIM and 2 * t_dec == 8
    rows = 2 * t_dec * N_HEADS
    vec = lambda n: pl.BlockSpec((1, n), lambda s, p, pt: (0, 0))

    def page_spec(gi):
        return pl.BlockSpec((None, None, page, n_heads, hd),
                            lambda s, p, pt: (0, pt[s, p * n_pg + gi], 0, 0, 0))

    grid_spec = pltpu.PrefetchScalarGridSpec(
        num_scalar_prefetch=1,
        grid=(n_seq, n_pages // n_pg),
        in_specs=[
            vec(QK_HALF), vec(QK_HALF), vec(QK_HALF), vec(QK_HALF), vec(HEAD_DIM),
            pl.BlockSpec((1, rows, hd), lambda s, p, pt: (s, 0, 0)),
            pl.BlockSpec((1, t_dec * n_heads, hd), lambda s, p, pt: (s, 0, 0)),
            pl.BlockSpec((1, t_dec * n_heads, hd), lambda s, p, pt: (s, 0, 0)),
        ] + [page_spec(gi) for gi in range(n_pg)] * 2,
        out_specs=pl.BlockSpec((1, rows, hd), lambda s, p, pt: (s, 0, 0)),
        scratch_shapes=[
            pltpu.VMEM((rows, hd), BF16),
            pltpu.VMEM((rows, 1), F32),
            pltpu.VMEM((rows, 1), F32),
            pltpu.VMEM((rows, hd), F32),
            pltpu.VMEM((rows, page * n_heads), F32),
        ],
    )
    return pl.pallas_call(
        functools.partial(_attn_sample_body, lam_init=lam_init, t_dec=t_dec, past_len=n_pages * page),
        grid_spec=grid_spec,
        out_shape=jax.ShapeDtypeStruct((n_seq, rows, hd), F32),
        compiler_params=_params("parallel", "arbitrary"),
        name="attn_sample",
    )(page_table, *lams, gain, q_rows, k_new, v_new,
      *([cache_k] * n_pg), *([cache_v] * n_pg))


def _outproj_body(x_ref, a_ref, b_ref, w_ref, o_ref):
    half = a_ref.shape[1]
    o_ref[...] = (x_ref[...]
                  + jnp.dot(a_ref[...], w_ref[0:half, :], preferred_element_type=F32)
                  + jnp.dot(b_ref[...], w_ref[half:, :], preferred_element_type=F32))


def _outproj(x, pool_out, attn_out, w_out, *, tm):
    m, d = x.shape
    c = pool_out.shape[1]
    return pl.pallas_call(
        _outproj_body,
        grid=(m // tm,),
        in_specs=[
            pl.BlockSpec((tm, d), lambda i: (i, 0)),
            pl.BlockSpec((tm, c), lambda i: (i, 0)),
            pl.BlockSpec((tm, c), lambda i: (i, 0)),
            pl.BlockSpec(w_out.shape, lambda i: (0, 0)),
        ],
        out_specs=pl.BlockSpec((tm, d), lambda i: (i, 0)),
        out_shape=jax.ShapeDtypeStruct((m, d), F32),
        compiler_params=_params("parallel"),
        name="outproj",
    )(x, pool_out, attn_out, w_out)


def kernel(x_prompt, x_sample, cache_k, cache_v, state_pool, page_table, meta_tokens,
           norm_ffn1, w_gate1, w_up1, w_down1, norm_mix, w_in, w_pool, pool_scale,
           lambda_q1, lambda_k1, lambda_q2, lambda_k2, subln_gain, w_out,
           norm_ffn2, w_gate2, w_up2, w_down2, norm_final):
    depth = w_in.shape[0]
    assert depth == 1
    bsz, seq, d = x_prompt.shape
    n_seq, t_dec, _ = x_sample.shape
    c = N_HEADS * HEAD_DIM
    n_small = N_META + n_seq * t_dec
    assert n_small <= SMALL_ROWS and 2 * t_dec == 8
    lam_init = 0.8 - 0.6 * math.exp(-0.3 * 0)

    row = lambda a: a.reshape(1, -1)
    wg1, wu1, wd1 = w_gate1[0].astype(BF16), w_up1[0].astype(BF16), w_down1[0].astype(BF16)
    wg2, wu2, wd2 = w_gate2[0].astype(BF16), w_up2[0].astype(BF16), w_down2[0].astype(BF16)
    w_in_b, w_out_b, w_pool_b = w_in[0].astype(BF16), w_out[0].astype(BF16), w_pool[0].astype(BF16)
    g1, gm, g2, gf = row(norm_ffn1[0]), row(norm_mix[0]), row(norm_ffn2[0]), row(norm_final)
    pscale, gain = row(pool_scale[0]), row(subln_gain[0])
    lams = (row(lambda_q1[0]), row(lambda_k1[0]), row(lambda_q2[0]), row(lambda_k2[0]))
    slopes = jnp.exp2(-jnp.arange(1, N_HEADS + 1, dtype=F32))

    x_big = x_prompt.reshape(bsz * seq, d)
    x_small = jnp.concatenate([meta_tokens, x_sample.reshape(n_seq * t_dec, d),
                               jnp.zeros((SMALL_ROWS - n_small, d), F32)], axis=0)

    xs1 = _ffn(x_small, g1, wg1, wu1, wd1, gf, final_norm=False, tm=SMALL_ROWS, tf=512)
    ps, ks, vs, qs, kbs, vbs = _proj(xs1, gm, w_in_b, tm=SMALL_ROWS)
    state16 = jnp.pad(state_pool[0], ((0, 0), (HALO - POOL_BUF, 0), (0, 0)))
    pool_s = _pool_small(ps, state16, w_pool_b, pscale, n_seq=n_seq, t_dec=t_dec).astype(BF16)
    o_meta = _attn_meta(qs, kbs, vbs, slopes, lams, gain, lam_init=lam_init)
    q_dec = qs[N_META:n_small].reshape(n_seq, t_dec, N_HEADS, HEAD_DIM).transpose(0, 2, 1, 3)
    q_rows = jnp.broadcast_to(q_dec[:, :, None], (n_seq, N_HEADS, 2, t_dec, HEAD_DIM))
    q_rows = q_rows.reshape(n_seq, 2 * t_dec * N_HEADS, HEAD_DIM)
    k_new = kbs[N_META:n_small].reshape(n_seq, t_dec * N_HEADS, HEAD_DIM)
    v_new = vbs[N_META:n_small].reshape(n_seq, t_dec * N_HEADS, HEAD_DIM)
    o_dec = _attn_sample(page_table, q_rows, k_new, v_new, cache_k, cache_v, lams, gain,
                         lam_init=lam_init, t_dec=t_dec)
    o_dec = o_dec.reshape(n_seq, N_HEADS, 2, t_dec, HEAD_DIM)[:, :, 0].transpose(0, 2, 1, 3)
    o_small = jnp.concatenate([o_meta, o_dec.reshape(n_seq * t_dec, c).astype(BF16),
                               jnp.zeros((SMALL_ROWS - n_small, c), BF16)], axis=0)
    xs2 = _outproj(xs1, pool_s, o_small, w_out_b, tm=SMALL_ROWS)
    ys = _ffn(xs2, g2, wg2, wu2, wd2, gf, final_norm=True, tm=SMALL_ROWS, tf=512)

    xb1 = _ffn(x_big, g1, wg1, wu1, wd1, gf, final_norm=False, tm=512, tf=512)
    pb, kb, vb, k1b, k2b, q_t, v_t = _proj(xb1, gm, w_in_b, tm=512, seq=seq)
    pool_b = _pool_big(pb.reshape(bsz, seq, c), ps[:N_META], w_pool_b, pscale, tm=512)
    vm_t = vbs[:N_META].reshape(N_META, N_HEADS, HEAD_DIM).transpose(1, 2, 0)
    ones_rows = jnp.zeros((N_HEADS, V_ROWS - HEAD_DIM, N_META), BF16).at[:, 0].set(1.0)
    vm_t = jnp.concatenate([vm_t, ones_rows], axis=1)
    o_big = _attn_big(q_t, k1b, k2b, v_t, kbs[:N_META], vm_t, slopes, lams, gain.reshape(HEAD_DIM, 1),
                      lam_init=lam_init, bsz=bsz)
    xb2 = _outproj(xb1, pool_b.reshape(bsz * seq, c), o_big, w_out_b, tm=512)
    yb = _ffn(xb2, g2, wg2, wu2, wd2, gf, final_norm=True, tm=512, tf=512)

    def with_meta(small, big):
        meta = jnp.broadcast_to(small[None, :N_META], (bsz, N_META, c))
        full = jnp.concatenate([meta, big.reshape(bsz, seq, c)], axis=1)
        return full.reshape(1, bsz, N_META + seq, N_HEADS, HEAD_DIM)

    dec = lambda a: a[N_META:n_small].reshape(1, n_seq, t_dec, N_HEADS, HEAD_DIM)
    y_prompt = yb.reshape(bsz, seq, d)
    y_sample = ys[N_META:n_small].reshape(n_seq, t_dec, d)
    pool_prompt = pb.reshape(bsz, seq, c)[:, seq - POOL_BUF:][None]
    pool_sample = jnp.concatenate([state_pool[0][:, t_dec:], ps[N_META:n_small].reshape(n_seq, t_dec, c)],
                                  axis=1)[None]
    return (y_prompt, y_sample, with_meta(ks, kb), with_meta(vs, vb), pool_prompt,
            dec(ks), dec(vs), pool_sample)
```

```python
import functools
import math

import jax
import jax.numpy as jnp
from jax import lax
from jax.experimental import pallas as pl
from jax.experimental.pallas import tpu as pltpu

F32 = jnp.float32
BF16 = jnp.bfloat16

N_META = 16
POOL_WINDOWS = (2, 4, 8, 16)
POOL_BUF = max(POOL_WINDOWS) - 1
HALO = 16
N_HEADS = 8
HEAD_DIM = 128
QK_HALF = HEAD_DIM // 2
EPS = 1e-6
NEG_INF = -1e30
SMALL_ROWS = 128
ATTN_BLOCK = 256
V_ROWS = HEAD_DIM + 16
KEY_SUB = 128
FFN_ROWS = 1024
FFN_COLS = 256
SAMPLE_GROUP = 2
PAGES_PER_STEP = 8

VMEM_LIMIT = 60 * 1024 * 1024


def _params(*sem):
    return pltpu.CompilerParams(dimension_semantics=sem, vmem_limit_bytes=VMEM_LIMIT)


def _rms(x, g):
    return x * lax.rsqrt(jnp.mean(x * x, axis=-1, keepdims=True) + EPS) * g


def _ffn_body(xb_ref, xs_ref, g_ref, wg_ref, wu_ref, wd_ref, gf_ref, ob_ref, os_ref,
              hb_ref, hs_ref, wgb_ref, wub_ref, wdb_ref, *, final_norm):
    i, s = pl.program_id(0), pl.program_id(1)
    n_blk = pl.num_programs(1) - 1

    def round_weights():
        slot = s % 2
        wgb_ref[slot] = wg_ref[...].astype(BF16)
        wub_ref[slot] = wu_ref[...].astype(BF16)
        wdb_ref[slot] = wd_ref[...].astype(BF16)

    def start(x_ref, o_ref, h_ref):
        h_ref[...] = _rms(x_ref[...], g_ref[...]).astype(BF16)
        o_ref[...] = jnp.zeros_like(o_ref)

    def accumulate(o_ref, h_ref):
        slot = (s + 1) % 2
        h = h_ref[...]
        g = jnp.dot(h, wgb_ref[slot], preferred_element_type=F32)
        u = jnp.dot(h, wub_ref[slot], preferred_element_type=F32)
        a = (g * jax.nn.sigmoid(g) * u).astype(BF16)
        o_ref[...] += jnp.dot(a, wdb_ref[slot], preferred_element_type=F32)

    def finish(x_ref, o_ref):
        y = x_ref[...] + 0.5 * o_ref[...]
        if final_norm:
            y = _rms(y, gf_ref[...])
        o_ref[...] = y

    last_tile = i == pl.num_programs(0) - 1

    @pl.when(s == 0)
    def _():
        round_weights()
        start(xb_ref, ob_ref, hb_ref)

        @pl.when(last_tile)
        def _():
            start(xs_ref, os_ref, hs_ref)

    @pl.when(s > 0)
    def _():
        accumulate(ob_ref, hb_ref)
        round_weights()

        @pl.when(last_tile)
        def _():
            accumulate(os_ref, hs_ref)

    @pl.when(s == n_blk)
    def _():
        finish(xb_ref, ob_ref)

        @pl.when(last_tile)
        def _():
            finish(xs_ref, os_ref)


def _ffn(x_big, x_small, g, wg, wu, wd, gf, *, final_norm):
    m, d = x_big.shape
    ms = x_small.shape[0]
    f = wg.shape[1]
    tm, tf = FFN_ROWS, FFN_COLS
    assert m % tm == 0 and f % tf == 0
    n_blk = f // tf
    const = lambda i, s: (0, 0)
    blk = lambda s: jnp.minimum(s, n_blk - 1)
    return pl.pallas_call(
        functools.partial(_ffn_body, final_norm=final_norm),
        grid=(m // tm, n_blk + 1),
        in_specs=[
            pl.BlockSpec((tm, d), lambda i, s: (i, 0), pipeline_mode=pl.Buffered(1)),
            pl.BlockSpec((ms, d), const),
            pl.BlockSpec((1, d), const),
            pl.BlockSpec((d, tf), lambda i, s: (0, blk(s))),
            pl.BlockSpec((d, tf), lambda i, s: (0, blk(s))),
            pl.BlockSpec((tf, d), lambda i, s: (blk(s), 0)),
            pl.BlockSpec((1, d), const),
        ],
        out_specs=[pl.BlockSpec((tm, d), lambda i, s: (i, 0)), pl.BlockSpec((ms, d), const)],
        out_shape=[jax.ShapeDtypeStruct((m, d), F32), jax.ShapeDtypeStruct((ms, d), F32)],
        scratch_shapes=[
            pltpu.VMEM((tm, d), BF16), pltpu.VMEM((ms, d), BF16),
            pltpu.VMEM((2, d, tf), BF16), pltpu.VMEM((2, d, tf), BF16), pltpu.VMEM((2, tf, d), BF16),
        ],
        compiler_params=_params("arbitrary", "arbitrary"),
        name="ffn",
    )(x_big, x_small, g, wg, wu, wd, gf)


def _proj_body(x_ref, g_ref, w_ref, p_ref, k_ref, v_ref, kb_ref, a_ref, b_ref, h_ref, *, transposed):
    j = pl.program_id(1)

    @pl.when(j == 0)
    def _():
        h_ref[...] = _rms(x_ref[...], g_ref[...]).astype(BF16)

    r = jnp.dot(h_ref[...], w_ref[...], preferred_element_type=F32)

    def put_t(dst, val, extra_ones):
        tc = dst.shape[3]
        for hd in range(N_HEADS):
            for ci in range(dst.shape[1]):
                blk = val[ci * tc:(ci + 1) * tc, hd * HEAD_DIM:(hd + 1) * HEAD_DIM]
                dst[hd, ci, 0:HEAD_DIM, :] = blk.T.astype(BF16)
                if extra_ones:
                    rid = lax.broadcasted_iota(jnp.int32, (dst.shape[2] - HEAD_DIM, tc), 0)
                    dst[hd, ci, HEAD_DIM:, :] = jnp.where(rid == 0, 1.0, 0.0).astype(BF16)

    @pl.when(j == 0)
    def _():
        p_ref[...] = r

    @pl.when(j == 1)
    def _():
        q = r * (QK_HALF ** -0.5)
        if transposed:
            put_t(a_ref, q, False)
        else:
            a_ref[...] = q.astype(BF16)

    @pl.when(j == 2)
    def _():
        k_ref[...] = r
        kb_ref[...] = r.astype(BF16)

    @pl.when(j == 3)
    def _():
        v_ref[...] = r
        if transposed:
            put_t(b_ref, r, True)
        else:
            b_ref[...] = r.astype(BF16)


def _proj(x, g, w_in, *, tm, transposed):
    m, d = x.shape
    w = w_in.shape[1] // 4
    row = lambda i, j: (i, 0)
    rm_spec = pl.BlockSpec((tm, w), row)
    rm_shape = jax.ShapeDtypeStruct((m, w), BF16)
    if transposed:
        tc = ATTN_BLOCK
        t_spec = lambda n: pl.BlockSpec((N_HEADS, tm // tc, n, tc), lambda i, j: (0, i, 0, 0))
        t_shape = lambda n: jax.ShapeDtypeStruct((N_HEADS, m // tc, n, tc), BF16)
        extra_specs = [rm_spec, t_spec(HEAD_DIM), t_spec(V_ROWS)]
        extra_shapes = [rm_shape, t_shape(HEAD_DIM), t_shape(V_ROWS)]
    else:
        extra_specs = [rm_spec] * 3
        extra_shapes = [rm_shape] * 3
    f32_shape = jax.ShapeDtypeStruct((m, w), F32)
    return pl.pallas_call(
        functools.partial(_proj_body, transposed=transposed),
        grid=(m // tm, 4),
        in_specs=[
            pl.BlockSpec((tm, d), row),
            pl.BlockSpec((1, d), lambda i, j: (0, 0)),
            pl.BlockSpec((d, w), lambda i, j: (0, j)),
        ],
        out_specs=[rm_spec] * 3 + extra_specs,
        out_shape=[f32_shape] * 3 + extra_shapes,
        scratch_shapes=[pltpu.VMEM((tm, d), BF16)],
        compiler_params=_params("parallel", "arbitrary"),
        name="proj",
    )(x, g, w_in)


def _pool_groups(buf_ref, rows, wp_ref, sc_ref, count_fn, write):
    gd = wp_ref.shape[1]
    for gi, w in enumerate(POOL_WINDOWS):
        cs = slice(gi * gd, (gi + 1) * gd)
        cur = buf_ref[HALO:HALO + rows, cs]
        acc = cur
        for dlt in range(1, w):
            acc = acc + buf_ref[HALO - dlt:HALO - dlt + rows, cs]
        feat = acc / count_fn(w) - cur
        out = jnp.dot(feat.astype(BF16), wp_ref[gi], preferred_element_type=F32) * sc_ref[:, cs]
        write(cs, out)


def _pool_big_body(p_ref, halo_ref, pm_ref, wp_ref, sc_ref, o_ref, buf_ref):
    i = pl.program_id(1)
    rows = p_ref.shape[1]

    @pl.when(i == 0)
    def _():
        buf_ref[0:HALO, :] = pm_ref[...]

    @pl.when(i > 0)
    def _():
        buf_ref[0:HALO, :] = halo_ref[0]

    buf_ref[HALO:HALO + rows, :] = p_ref[0]

    def write(cs, val):
        o_ref[0, :, cs] = val.astype(BF16)

    _pool_groups(buf_ref, rows, wp_ref, sc_ref, lambda w: float(w), write)


def _pool_big(p, p_meta, w_pool, pool_scale, *, tm):
    b, t, c = p.shape
    hb = tm // HALO
    return pl.pallas_call(
        _pool_big_body,
        grid=(b, t // tm),
        in_specs=[
            pl.BlockSpec((1, tm, c), lambda bi, i: (bi, i, 0)),
            pl.BlockSpec((1, HALO, c), lambda bi, i: (bi, jnp.maximum(i * hb - 1, 0), 0)),
            pl.BlockSpec((HALO, c), lambda bi, i: (0, 0)),
            pl.BlockSpec(w_pool.shape, lambda bi, i: (0, 0, 0)),
            pl.BlockSpec((1, c), lambda bi, i: (0, 0)),
        ],
        out_specs=pl.BlockSpec((1, tm, c), lambda bi, i: (bi, i, 0)),
        out_shape=jax.ShapeDtypeStruct((b, t, c), BF16),
        scratch_shapes=[pltpu.VMEM((HALO + tm, c), F32)],
        compiler_params=_params("parallel", "arbitrary"),
        name="pool_big",
    )(p, p, p_meta, w_pool, pool_scale)


def _pool_small_body(p_ref, st_ref, wp_ref, sc_ref, o_ref, buf_ref, *, n_seq, t_dec):
    buf_ref[0:HALO, :] = jnp.zeros((HALO, buf_ref.shape[1]), F32)
    buf_ref[HALO:2 * HALO, :] = p_ref[0:N_META, :]
    pos1 = lax.broadcasted_iota(jnp.int32, (N_META, 1), 0).astype(F32) + 1.0

    def write_meta(cs, val):
        o_ref[0:N_META, cs] = val

    _pool_groups(buf_ref, N_META, wp_ref, sc_ref, lambda w: jnp.minimum(pos1, float(w)), write_meta)

    o_ref[N_META + n_seq * t_dec:, :] = jnp.zeros(
        (o_ref.shape[0] - N_META - n_seq * t_dec, o_ref.shape[1]), F32)
    for s in range(n_seq):
        r0 = N_META + s * t_dec
        buf_ref[0:HALO, :] = st_ref[s]
        buf_ref[HALO:HALO + 8, :] = p_ref[r0:r0 + 8, :]

        def write_s(cs, val, r0=r0):
            o_ref[r0:r0 + t_dec, cs] = val[0:t_dec]

        _pool_groups(buf_ref, 8, wp_ref, sc_ref, lambda w: float(w), write_s)


def _pool_small(p_small, state16, w_pool, pool_scale, *, n_seq, t_dec):
    m, c = p_small.shape
    return pl.pallas_call(
        functools.partial(_pool_small_body, n_seq=n_seq, t_dec=t_dec),
        out_shape=jax.ShapeDtypeStruct((m, c), F32),
        scratch_shapes=[pltpu.VMEM((2 * HALO, c), F32)],
        compiler_params=pltpu.CompilerParams(vmem_limit_bytes=VMEM_LIMIT),
        name="pool_small",
    )(p_small, state16, w_pool, pool_scale)


def _lambda(lq1_ref, lk1_ref, lq2_ref, lk2_ref, lam_init):
    a = jnp.sum(lq1_ref[...] * lk1_ref[...], axis=-1, keepdims=True)
    b = jnp.sum(lq2_ref[...] * lk2_ref[...], axis=-1, keepdims=True)
    return jnp.exp(a) - jnp.exp(b) + lam_init


def _split_halves(q, axis):
    idx = lax.broadcasted_iota(jnp.int32, q.shape, axis) % HEAD_DIM
    zero = jnp.zeros_like(q)
    return jnp.where(idx < QK_HALF, q, zero), jnp.where(idx >= QK_HALF, q, zero)


def _nt_dot(a, b):
    return lax.dot_general(a, b, (((1,), (1,)), ((), ())), preferred_element_type=F32)


def _attn_big_body(slope_ref, lq1_ref, lk1_ref, lq2_ref, lk2_ref, gain_ref,
                   qt_ref, k_ref, vt_ref, km_ref, vmt_ref, o_ref,
                   m_ref, acc_ref, p_ref, al_ref, pos_ref, posm_ref, *, lam_init):
    h = pl.program_id(1)
    blk = ATTN_BLOCK
    n_blk = qt_ref.shape[1]
    slope = slope_ref[h]
    lam = _lambda(lq1_ref, lk1_ref, lq2_ref, lk2_ref, lam_init)
    diag = (lax.broadcasted_iota(jnp.int32, (blk, blk), 0)
            <= lax.broadcasted_iota(jnp.int32, (blk, blk), 1))

    def pos_block(n, first):
        pos = lax.broadcasted_iota(jnp.int32, (n, HEAD_DIM), 0) + first
        col = lax.broadcasted_iota(jnp.int32, (n, HEAD_DIM), 1)
        coarse = slope * ((pos // QK_HALF) * QK_HALF).astype(F32)
        fine = slope * (pos % QK_HALF).astype(F32)
        return jnp.where(col == 0, coarse, jnp.where(col == 1, fine, 0.0)).astype(BF16)

    pos_ref[...] = pos_block(pos_ref.shape[0], 0)
    posm_ref[...] = pos_block(N_META, -N_META)

    def q_pair(qj, carry):
        q_ext = []
        for qh in range(2):
            q = qt_ref[0, 2 * qj + qh].astype(F32)
            rid = lax.broadcasted_iota(jnp.int32, q.shape, 0)
            ones = jnp.where(rid < 2, 1.0, 0.0).astype(BF16)
            q_ext.append(tuple(
                jnp.concatenate([jnp.where(keep, q, 0.0).astype(BF16), ones], axis=0)
                for keep in (rid < QK_HALF, rid >= QK_HALF)))
        m_ref[...] = jnp.full(m_ref.shape, NEG_INF, F32)
        acc_ref[...] = jnp.zeros(acc_ref.shape, F32)
        p_ref[...] = jnp.zeros(p_ref.shape, BF16)
        al_ref[...] = jnp.ones(al_ref.shape, F32)

        def scores(slot, kb, kpos, targets):
            n_keys = kb.shape[0]
            sub = min(KEY_SUB, n_keys)
            for ki in range(n_keys // sub):
                rs = slice(ki * sub, (ki + 1) * sub)
                k_ext = jnp.concatenate([kb[rs], kpos[rs]], axis=1)
                for qh, mask in targets:
                    for mi in range(2):
                        st = 2 * qh + mi
                        s = jnp.dot(k_ext, q_ext[qh][mi], preferred_element_type=F32)
                        if mask is not None:
                            s = jnp.where(mask[rs], s, NEG_INF)
                        m_prev = m_ref[st]
                        m_new = jnp.maximum(m_prev, jnp.max(s, axis=0, keepdims=True))
                        al_ref[slot, ki, st] = jnp.exp(m_prev - m_new)
                        p_ref[slot, st, rs, :] = jnp.exp(s - m_new).astype(BF16)
                        m_ref[st] = m_new

        def values(slot, vt, streams):
            n_keys = vt.shape[1]
            sub = min(KEY_SUB, n_keys)
            for ki in range(n_keys // sub):
                rs = slice(ki * sub, (ki + 1) * sub)
                for st in streams:
                    acc_ref[st] = al_ref[slot, ki, st] * acc_ref[st] + jnp.dot(
                        vt[:, rs], p_ref[slot, st, rs, :], preferred_element_type=F32)

        def keys(kj):
            k0 = pl.multiple_of(kj * blk, blk)
            return k_ref[pl.ds(k0, blk), :], pos_ref[pl.ds(k0, blk), :]

        both = [(0, None), (1, None)]
        all_streams = (0, 1, 2, 3)

        def block_pair(t, c2):
            values(1, vt_ref[0, jnp.maximum(2 * t - 1, 0)], all_streams)
            scores(0, *keys(2 * t), both)
            values(0, vt_ref[0, 2 * t], all_streams)
            scores(1, *keys(2 * t + 1), both)
            return c2

        lax.fori_loop(0, qj, block_pair, 0)
        values(1, vt_ref[0, jnp.maximum(2 * qj - 1, 0)], all_streams)
        scores(0, *keys(2 * qj), [(0, diag), (1, None)])
        values(0, vt_ref[0, 2 * qj], all_streams)
        scores(1, *keys(2 * qj + 1), [(1, diag)])
        values(1, vt_ref[0, 2 * qj + 1], (2, 3))
        scores(0, km_ref[...], posm_ref[...], both)
        values(0, vmt_ref[0], all_streams)

        for qh in range(2):
            a1, a2 = acc_ref[2 * qh], acc_ref[2 * qh + 1]
            o = (a1[0:HEAD_DIM] / a1[HEAD_DIM:HEAD_DIM + 1]
                 - lam * (a2[0:HEAD_DIM] / a2[HEAD_DIM:HEAD_DIM + 1]))
            o = o * lax.rsqrt(jnp.mean(o * o, axis=0, keepdims=True) + EPS) * gain_ref[...]
            r0 = pl.multiple_of((2 * qj + qh) * blk, blk)
            o_ref[pl.ds(r0, blk), :] = (o * (1.0 - lam_init)).T.astype(BF16)
        return carry

    assert n_blk % 2 == 0
    lax.fori_loop(0, n_blk // 2, q_pair, 0)


def _attn_big(q_t, kb, v_t, k_meta, vm_t, slopes, lams, gain_col, *, lam_init, bsz):
    n_heads, n_chunks, hd, blk = q_t.shape
    m, c = kb.shape
    seq = m // bsz
    n_blk = seq // blk
    vec = lambda n: pl.BlockSpec((1, n), lambda bi, h: (0, 0))
    return pl.pallas_call(
        functools.partial(_attn_big_body, lam_init=lam_init),
        grid=(bsz, n_heads),
        in_specs=[
            pl.BlockSpec(memory_space=pltpu.SMEM),
            vec(QK_HALF), vec(QK_HALF), vec(QK_HALF), vec(QK_HALF),
            pl.BlockSpec((hd, 1), lambda bi, h: (0, 0)),
            pl.BlockSpec((1, n_blk, hd, blk), lambda bi, h: (h, bi, 0, 0)),
            pl.BlockSpec((seq, hd), lambda bi, h: (bi, h)),
            pl.BlockSpec((1, n_blk, V_ROWS, blk), lambda bi, h: (h, bi, 0, 0)),
            pl.BlockSpec((N_META, hd), lambda bi, h: (0, h)),
            pl.BlockSpec((1, V_ROWS, N_META), lambda bi, h: (h, 0, 0)),
        ],
        out_specs=pl.BlockSpec((seq, hd), lambda bi, h: (bi, h)),
        out_shape=jax.ShapeDtypeStruct((m, c), BF16),
        scratch_shapes=[
            pltpu.VMEM((4, 1, blk), F32),
            pltpu.VMEM((4, V_ROWS, blk), F32),
            pltpu.VMEM((2, 4, blk, blk), BF16),
            pltpu.VMEM((2, blk // KEY_SUB, 4, 1, blk), F32),
            pltpu.VMEM((seq, hd), BF16),
            pltpu.VMEM((N_META, hd), BF16),
        ],
        compiler_params=_params("parallel", "arbitrary"),
        name="attn_big",
    )(slopes, *lams, gain_col, q_t, kb, v_t, k_meta, vm_t)


def _attn_meta_body(slope_ref, lq1_ref, lk1_ref, lq2_ref, lk2_ref, gain_ref,
                    q_ref, k_ref, v_ref, o_ref, *, lam_init):
    h = pl.program_id(0)
    slope = slope_ref[h]
    lam = _lambda(lq1_ref, lk1_ref, lq2_ref, lk2_ref, lam_init)
    qs = _split_halves(q_ref[...], 1)
    kb = k_ref[...]
    shape = (N_META, kb.shape[0])
    qpos = lax.broadcasted_iota(jnp.int32, shape, 0)
    kpos = lax.broadcasted_iota(jnp.int32, shape, 1)
    dist = (qpos - kpos).astype(F32)
    mask = (qpos >= kpos) & (kpos < N_META)
    outs = []
    for mi in range(2):
        s = jnp.where(mask, _nt_dot(qs[mi], kb) - slope * dist, NEG_INF)
        p = jnp.exp(s - jnp.max(s, axis=-1, keepdims=True))
        p = p / jnp.sum(p, axis=-1, keepdims=True)
        outs.append(p)
    a = (outs[0] - lam * outs[1]).astype(BF16)
    o = jnp.dot(a, v_ref[...], preferred_element_type=F32)
    o_ref[...] = (_rms(o, gain_ref[...]) * (1.0 - lam_init)).astype(BF16)


def _attn_meta(q_s, kb_s, vb_s, slopes, lams, gain, *, lam_init):
    m, c = kb_s.shape
    vec = lambda n: pl.BlockSpec((1, n), lambda h: (0, 0))
    return pl.pallas_call(
        functools.partial(_attn_meta_body, lam_init=lam_init),
        grid=(N_HEADS,),
        in_specs=[
            pl.BlockSpec(memory_space=pltpu.SMEM),
            vec(QK_HALF), vec(QK_HALF), vec(QK_HALF), vec(QK_HALF), vec(HEAD_DIM),
            pl.BlockSpec((N_META, HEAD_DIM), lambda h: (0, h)),
            pl.BlockSpec((m, HEAD_DIM), lambda h: (0, h)),
            pl.BlockSpec((m, HEAD_DIM), lambda h: (0, h)),
        ],
        out_specs=pl.BlockSpec((N_META, HEAD_DIM), lambda h: (0, h)),
        out_shape=jax.ShapeDtypeStruct((N_META, c), BF16),
        compiler_params=_params("parallel"),
        name="attn_meta",
    )(slopes, *lams, gain, q_s, kb_s, vb_s)


def _attn_sample_body(pt_ref, lq1_ref, lk1_ref, lq2_ref, lk2_ref, gain_ref,
                      q_ref, kn_ref, vn_ref, *rest, lam_init, t_dec, past_len):
    n_pg = PAGES_PER_STEP
    k_refs, v_refs = rest[:n_pg], rest[n_pg:2 * n_pg]
    o_ref, qm_ref, m_ref, l_ref, acc_ref, bias_ref, p_ref, al_ref = rest[2 * n_pg:]
    pg = pl.program_id(1)
    n_steps = pl.num_programs(1) - 1
    page = k_refs[0].shape[0]
    grp = 2 * t_dec
    rows = grp * N_HEADS
    cols = page * N_HEADS
    row1 = lax.broadcasted_iota(jnp.int32, (rows, 1), 0)
    r_q = row1 % t_dec
    slope = jnp.exp2(-((row1 // grp) + 1).astype(F32))

    def update(kb, vb, bias, shift):
        s = _nt_dot(qm_ref[...], kb) + bias
        m_prev = m_ref[...]
        m_new = jnp.maximum(m_prev, jnp.max(s, axis=-1, keepdims=True) + shift)
        alpha = jnp.exp(m_prev - m_new)
        p = jnp.exp(s - (m_new - shift))
        l_ref[...] = alpha * l_ref[...] + jnp.sum(p, axis=-1, keepdims=True)
        acc_ref[...] = alpha * acc_ref[...] + jnp.dot(p.astype(BF16), vb, preferred_element_type=F32)
        m_ref[...] = m_new

    @pl.when(pg == 0)
    def _():
        q = q_ref[0]
        r2 = lax.broadcasted_iota(jnp.int32, q.shape, 0)
        c2 = lax.broadcasted_iota(jnp.int32, q.shape, 1)
        qm_ref[...] = jnp.where((c2 // QK_HALF) == (r2 % grp) // t_dec, q, jnp.zeros_like(q))
        m_ref[...] = jnp.full(m_ref.shape, NEG_INF, F32)
        l_ref[...] = jnp.zeros(l_ref.shape, F32)
        acc_ref[...] = jnp.zeros(acc_ref.shape, F32)
        p_ref[...] = jnp.zeros(p_ref.shape, BF16)
        al_ref[...] = jnp.ones(al_ref.shape, F32)
        rw = lax.broadcasted_iota(jnp.int32, (rows, cols), 0)
        cl = lax.broadcasted_iota(jnp.int32, (rows, cols), 1)
        bias_ref[...] = jnp.where((cl % N_HEADS) == (rw // grp),
                                  slope * (cl // N_HEADS).astype(F32), NEG_INF)
        rw = lax.broadcasted_iota(jnp.int32, (rows, t_dec * N_HEADS), 0)
        cl = lax.broadcasted_iota(jnp.int32, (rows, t_dec * N_HEADS), 1)
        r_k = cl // N_HEADS
        ok = ((cl % N_HEADS) == (rw // grp)) & (r_k <= rw % t_dec)
        bias_new = jnp.where(ok, -slope * (rw % t_dec - r_k).astype(F32), NEG_INF)
        update(kn_ref[0], vn_ref[0], bias_new, 0.0)

    n_grp = n_pg // SAMPLE_GROUP

    def values():
        acc = acc_ref[...]
        for g in range(n_grp):
            acc = al_ref[g] * acc
            for gi in range(g * SAMPLE_GROUP, (g + 1) * SAMPLE_GROUP):
                vb = v_refs[gi][...].reshape(cols, HEAD_DIM).astype(BF16)
                acc = acc + jnp.dot(p_ref[:, gi * cols:(gi + 1) * cols], vb, preferred_element_type=F32)
        acc_ref[...] = acc

    @pl.when(pg < n_steps)
    def _():
        values()
        qm = qm_ref[...]
        m_run, l_run = m_ref[...], l_ref[...]
        for g in range(n_grp):
            s_pg, top = [], None
            for gi in range(g * SAMPLE_GROUP, (g + 1) * SAMPLE_GROUP):
                start = (pg * n_pg + gi) * page
                shift = slope * (start - past_len - r_q).astype(F32)
                kb = k_refs[gi][...].reshape(cols, HEAD_DIM).astype(BF16)
                s = _nt_dot(qm, kb) + bias_ref[...]
                s_pg.append((gi, s, shift))
                top_g = jnp.max(s, axis=-1, keepdims=True) + shift
                top = top_g if top is None else jnp.maximum(top, top_g)
            m_new = jnp.maximum(m_run, top)
            alpha = jnp.exp(m_run - m_new)
            l_run = alpha * l_run
            for gi, s, shift in s_pg:
                p = jnp.exp(s + (shift - m_new))
                l_run = l_run + jnp.sum(p, axis=-1, keepdims=True)
                p_ref[:, gi * cols:(gi + 1) * cols] = p.astype(BF16)
            al_ref[g] = alpha
            m_run = m_new
        m_ref[...] = m_run
        l_ref[...] = l_run

    @pl.when(pg == n_steps)
    def _():
        values()
        lam = _lambda(lq1_ref, lk1_ref, lq2_ref, lk2_ref, lam_init)
        for hd in range(N_HEADS):
            rs = slice(hd * grp, (hd + 1) * grp)
            n = acc_ref[rs, :] / l_ref[rs, :]
            o = n - lam * pltpu.roll(n, t_dec, 0)
            o_ref[0, rs, :] = _rms(o, gain_ref[...]) * (1.0 - lam_init)


def _attn_sample(page_table, q_rows, k_new, v_new, cache_k, cache_v, lams, gain, *, lam_init, t_dec):
    n_seq, n_pages = page_table.shape
    _, _, page, n_heads, hd = cache_k.shape
    n_pg = PAGES_PER_STEP
    assert n_pages % n_pg == 0 and n_heads == N_HEADS and hd == HEAD_DIM and 2 * t_dec == 8
    rows = 2 * t_dec * N_HEADS
    vec = lambda n: pl.BlockSpec((1, n), lambda s, p, pt: (0, 0))

    n_steps = n_pages // n_pg

    def page_spec(gi, lag):
        def index(s, p, pt):
            grp = jnp.maximum(p - 1, 0) if lag else jnp.minimum(p, n_steps - 1)
            return (0, pt[s, grp * n_pg + gi], 0, 0, 0)
        return pl.BlockSpec((None, None, page, n_heads, hd), index)

    grid_spec = pltpu.PrefetchScalarGridSpec(
        num_scalar_prefetch=1,
        grid=(n_seq, n_steps + 1),
        in_specs=[
            vec(QK_HALF), vec(QK_HALF), vec(QK_HALF), vec(QK_HALF), vec(HEAD_DIM),
            pl.BlockSpec((1, rows, hd), lambda s, p, pt: (s, 0, 0)),
            pl.BlockSpec((1, t_dec * n_heads, hd), lambda s, p, pt: (s, 0, 0)),
            pl.BlockSpec((1, t_dec * n_heads, hd), lambda s, p, pt: (s, 0, 0)),
        ] + [page_spec(gi, False) for gi in range(n_pg)] + [page_spec(gi, True) for gi in range(n_pg)],
        out_specs=pl.BlockSpec((1, rows, hd), lambda s, p, pt: (s, 0, 0)),
        scratch_shapes=[
            pltpu.VMEM((rows, hd), BF16),
            pltpu.VMEM((rows, 1), F32),
            pltpu.VMEM((rows, 1), F32),
            pltpu.VMEM((rows, hd), F32),
            pltpu.VMEM((rows, page * n_heads), F32),
            pltpu.VMEM((rows, n_pg * page * n_heads), BF16),
            pltpu.VMEM((n_pg // SAMPLE_GROUP, rows, 1), F32),
        ],
    )
    return pl.pallas_call(
        functools.partial(_attn_sample_body, lam_init=lam_init, t_dec=t_dec, past_len=n_pages * page),
        grid_spec=grid_spec,
        out_shape=jax.ShapeDtypeStruct((n_seq, rows, hd), F32),
        compiler_params=_params("parallel", "arbitrary"),
        name="attn_sample",
    )(page_table, *lams, gain, q_rows, k_new, v_new,
      *([cache_k] * n_pg), *([cache_v] * n_pg))


def _outproj_body(x_ref, a_ref, b_ref, w_ref, o_ref):
    half = a_ref.shape[1]
    o_ref[...] = (x_ref[...]
                  + jnp.dot(a_ref[...], w_ref[0:half, :], preferred_element_type=F32)
                  + jnp.dot(b_ref[...], w_ref[half:, :], preferred_element_type=F32))


def _outproj(x, pool_out, attn_out, w_out, *, tm):
    m, d = x.shape
    c = pool_out.shape[1]
    return pl.pallas_call(
        _outproj_body,
        grid=(m // tm,),
        in_specs=[
            pl.BlockSpec((tm, d), lambda i: (i, 0)),
            pl.BlockSpec((tm, c), lambda i: (i, 0)),
            pl.BlockSpec((tm, c), lambda i: (i, 0)),
            pl.BlockSpec(w_out.shape, lambda i: (0, 0)),
        ],
        out_specs=pl.BlockSpec((tm, d), lambda i: (i, 0)),
        out_shape=jax.ShapeDtypeStruct((m, d), F32),
        compiler_params=_params("parallel"),
        name="outproj",
    )(x, pool_out, attn_out, w_out)


def kernel(x_prompt, x_sample, cache_k, cache_v, state_pool, page_table, meta_tokens,
           norm_ffn1, w_gate1, w_up1, w_down1, norm_mix, w_in, w_pool, pool_scale,
           lambda_q1, lambda_k1, lambda_q2, lambda_k2, subln_gain, w_out,
           norm_ffn2, w_gate2, w_up2, w_down2, norm_final):
    depth = w_in.shape[0]
    assert depth == 1
    bsz, seq, d = x_prompt.shape
    n_seq, t_dec, _ = x_sample.shape
    c = N_HEADS * HEAD_DIM
    n_small = N_META + n_seq * t_dec
    assert n_small <= SMALL_ROWS and 2 * t_dec == 8
    lam_init = 0.8 - 0.6 * math.exp(-0.3 * 0)

    row = lambda a: a.reshape(1, -1)
    w_in_b, w_out_b, w_pool_b = w_in[0].astype(BF16), w_out[0].astype(BF16), w_pool[0].astype(BF16)
    g1, gm, g2, gf = row(norm_ffn1[0]), row(norm_mix[0]), row(norm_ffn2[0]), row(norm_final)
    pscale, gain = row(pool_scale[0]), row(subln_gain[0])
    lams = (row(lambda_q1[0]), row(lambda_k1[0]), row(lambda_q2[0]), row(lambda_k2[0]))
    slopes = jnp.exp2(-jnp.arange(1, N_HEADS + 1, dtype=F32))

    x_big = x_prompt.reshape(bsz * seq, d)
    x_small = jnp.concatenate([meta_tokens, x_sample.reshape(n_seq * t_dec, d),
                               jnp.zeros((SMALL_ROWS - n_small, d), F32)], axis=0)

    xb1, xs1 = _ffn(x_big, x_small, g1, w_gate1[0], w_up1[0], w_down1[0], gf, final_norm=False)

    ps, ks, vs, kbs, qs, vbs = _proj(xs1, gm, w_in_b, tm=SMALL_ROWS, transposed=False)
    state16 = jnp.pad(state_pool[0], ((0, 0), (HALO - POOL_BUF, 0), (0, 0)))
    pool_s = _pool_small(ps, state16, w_pool_b, pscale, n_seq=n_seq, t_dec=t_dec).astype(BF16)
    o_meta = _attn_meta(qs, kbs, vbs, slopes, lams, gain, lam_init=lam_init)
    q_dec = qs[N_META:n_small].reshape(n_seq, t_dec, N_HEADS, HEAD_DIM).transpose(0, 2, 1, 3)
    q_rows = jnp.broadcast_to(q_dec[:, :, None], (n_seq, N_HEADS, 2, t_dec, HEAD_DIM))
    q_rows = q_rows.reshape(n_seq, 2 * t_dec * N_HEADS, HEAD_DIM)
    k_new = kbs[N_META:n_small].reshape(n_seq, t_dec * N_HEADS, HEAD_DIM)
    v_new = vbs[N_META:n_small].reshape(n_seq, t_dec * N_HEADS, HEAD_DIM)
    o_dec = _attn_sample(page_table, q_rows, k_new, v_new, cache_k, cache_v, lams, gain,
                         lam_init=lam_init, t_dec=t_dec)
    o_dec = o_dec.reshape(n_seq, N_HEADS, 2, t_dec, HEAD_DIM)[:, :, 0].transpose(0, 2, 1, 3)
    o_small = jnp.concatenate([o_meta, o_dec.reshape(n_seq * t_dec, c).astype(BF16),
                               jnp.zeros((SMALL_ROWS - n_small, c), BF16)], axis=0)
    xs2 = _outproj(xs1, pool_s, o_small, w_out_b, tm=SMALL_ROWS)

    pb, kb, vb, kbb, q_t, v_t = _proj(xb1, gm, w_in_b, tm=512, transposed=True)
    pool_b = _pool_big(pb.reshape(bsz, seq, c), ps[:N_META], w_pool_b, pscale, tm=512)
    vm_t = vbs[:N_META].reshape(N_META, N_HEADS, HEAD_DIM).transpose(1, 2, 0)
    ones_rows = jnp.zeros((N_HEADS, V_ROWS - HEAD_DIM, N_META), BF16).at[:, 0].set(1.0)
    vm_t = jnp.concatenate([vm_t, ones_rows], axis=1)
    o_big = _attn_big(q_t, kbb, v_t, kbs[:N_META], vm_t, slopes, lams, gain.reshape(HEAD_DIM, 1),
                      lam_init=lam_init, bsz=bsz)
    xb2 = _outproj(xb1, pool_b.reshape(bsz * seq, c), o_big, w_out_b, tm=512)
    yb, ys = _ffn(xb2, xs2, g2, w_gate2[0], w_up2[0], w_down2[0], gf, final_norm=True)

    def with_meta(small, big):
        meta = jnp.broadcast_to(small[None, :N_META], (bsz, N_META, c))
        full = jnp.concatenate([meta, big.reshape(bsz, seq, c)], axis=1)
        return full.reshape(1, bsz, N_META + seq, N_HEADS, HEAD_DIM)

    dec = lambda a: a[N_META:n_small].reshape(1, n_seq, t_dec, N_HEADS, HEAD_DIM)
    y_prompt = yb.reshape(bsz, seq, d)
    y_sample = ys[N_META:n_small].reshape(n_seq, t_dec, d)
    pool_prompt = pb.reshape(bsz, seq, c)[:, seq - POOL_BUF:][None]
    pool_sample = jnp.concatenate([state_pool[0][:, t_dec:], ps[N_META:n_small].reshape(n_seq, t_dec, c)],
                                  axis=1)[None]
    return (y_prompt, y_sample, with_meta(ks, kb), with_meta(vs, vb), pool_prompt,
            dec(ks), dec(vs), pool_sample)
```

```python
import functools
import math

import jax
import jax.numpy as jnp
from jax import lax
from jax.experimental import pallas as pl
from jax.experimental.pallas import tpu as pltpu

F32 = jnp.float32
BF16 = jnp.bfloat16

N_META = 16
POOL_WINDOWS = (2, 4, 8, 16)
POOL_BUF = max(POOL_WINDOWS) - 1
HALO = 16
N_HEADS = 8
HEAD_DIM = 128
QK_HALF = HEAD_DIM // 2
EPS = 1e-6
NEG_INF = -1e30
SMALL_ROWS = 128
ATTN_BLOCK = 256
V_ROWS = HEAD_DIM + 16
Q_GROUP = 4
KEY_SUB = 128
FFN_ROWS = 1024
FFN_COLS = 256
SAMPLE_GROUP = 2
PAGES_PER_STEP = 8

VMEM_LIMIT = 60 * 1024 * 1024


def _params(*sem):
    return pltpu.CompilerParams(dimension_semantics=sem, vmem_limit_bytes=VMEM_LIMIT)


def _rms(x, g):
    return x * lax.rsqrt(jnp.mean(x * x, axis=-1, keepdims=True) + EPS) * g


def _ffn_body(xb_ref, xs_ref, g_ref, wg_ref, wu_ref, wd_ref, gf_ref, ob_ref, os_ref,
              hb_ref, hs_ref, wgb_ref, wub_ref, wdb_ref, *, final_norm):
    i, s = pl.program_id(0), pl.program_id(1)
    n_blk = pl.num_programs(1) - 1

    def round_weights():
        slot = s % 2
        wgb_ref[slot] = wg_ref[...].astype(BF16)
        wub_ref[slot] = wu_ref[...].astype(BF16)
        wdb_ref[slot] = wd_ref[...].astype(BF16)

    def start(x_ref, o_ref, h_ref):
        h_ref[...] = _rms(x_ref[...], g_ref[...]).astype(BF16)
        o_ref[...] = jnp.zeros_like(o_ref)

    def accumulate(o_ref, h_ref):
        slot = (s + 1) % 2
        h = h_ref[...]
        g = jnp.dot(h, wgb_ref[slot], preferred_element_type=F32)
        u = jnp.dot(h, wub_ref[slot], preferred_element_type=F32)
        a = (g * jax.nn.sigmoid(g) * u).astype(BF16)
        o_ref[...] += jnp.dot(a, wdb_ref[slot], preferred_element_type=F32)

    def finish(x_ref, o_ref):
        y = x_ref[...] + 0.5 * o_ref[...]
        if final_norm:
            y = _rms(y, gf_ref[...])
        o_ref[...] = y

    last_tile = i == pl.num_programs(0) - 1

    @pl.when(s == 0)
    def _():
        round_weights()
        start(xb_ref, ob_ref, hb_ref)

        @pl.when(last_tile)
        def _():
            start(xs_ref, os_ref, hs_ref)

    @pl.when(s > 0)
    def _():
        accumulate(ob_ref, hb_ref)
        round_weights()

        @pl.when(last_tile)
        def _():
            accumulate(os_ref, hs_ref)

    @pl.when(s == n_blk)
    def _():
        finish(xb_ref, ob_ref)

        @pl.when(last_tile)
        def _():
            finish(xs_ref, os_ref)


def _ffn(x_big, x_small, g, wg, wu, wd, gf, *, final_norm):
    m, d = x_big.shape
    ms = x_small.shape[0]
    f = wg.shape[1]
    tm, tf = FFN_ROWS, FFN_COLS
    assert m % tm == 0 and f % tf == 0
    n_blk = f // tf
    const = lambda i, s: (0, 0)
    blk = lambda s: jnp.minimum(s, n_blk - 1)
    return pl.pallas_call(
        functools.partial(_ffn_body, final_norm=final_norm),
        grid=(m // tm, n_blk + 1),
        in_specs=[
            pl.BlockSpec((tm, d), lambda i, s: (i, 0), pipeline_mode=pl.Buffered(1)),
            pl.BlockSpec((ms, d), const),
            pl.BlockSpec((1, d), const),
            pl.BlockSpec((d, tf), lambda i, s: (0, blk(s))),
            pl.BlockSpec((d, tf), lambda i, s: (0, blk(s))),
            pl.BlockSpec((tf, d), lambda i, s: (blk(s), 0)),
            pl.BlockSpec((1, d), const),
        ],
        out_specs=[pl.BlockSpec((tm, d), lambda i, s: (i, 0)), pl.BlockSpec((ms, d), const)],
        out_shape=[jax.ShapeDtypeStruct((m, d), F32), jax.ShapeDtypeStruct((ms, d), F32)],
        scratch_shapes=[
            pltpu.VMEM((tm, d), BF16), pltpu.VMEM((ms, d), BF16),
            pltpu.VMEM((2, d, tf), BF16), pltpu.VMEM((2, d, tf), BF16), pltpu.VMEM((2, tf, d), BF16),
        ],
        compiler_params=_params("arbitrary", "arbitrary"),
        name="ffn",
    )(x_big, x_small, g, wg, wu, wd, gf)


def _proj_body(x_ref, g_ref, w_ref, p_ref, k_ref, v_ref, kb_ref, a_ref, b_ref, h_ref, *, transposed):
    j = pl.program_id(1)

    @pl.when(j == 0)
    def _():
        h_ref[...] = _rms(x_ref[...], g_ref[...]).astype(BF16)

    r = jnp.dot(h_ref[...], w_ref[...], preferred_element_type=F32)

    def put_t(dst, val, extra_ones):
        tc = dst.shape[3]
        for hd in range(N_HEADS):
            for ci in range(dst.shape[1]):
                blk = val[ci * tc:(ci + 1) * tc, hd * HEAD_DIM:(hd + 1) * HEAD_DIM]
                dst[hd, ci, 0:HEAD_DIM, :] = blk.T.astype(BF16)
                if extra_ones:
                    rid = lax.broadcasted_iota(jnp.int32, (dst.shape[2] - HEAD_DIM, tc), 0)
                    dst[hd, ci, HEAD_DIM:, :] = jnp.where(rid == 0, 1.0, 0.0).astype(BF16)

    @pl.when(j == 0)
    def _():
        p_ref[...] = r

    @pl.when(j == 1)
    def _():
        q = r * (QK_HALF ** -0.5)
        if transposed:
            put_t(a_ref, q, False)
        else:
            a_ref[...] = q.astype(BF16)

    @pl.when(j == 2)
    def _():
        k_ref[...] = r
        kb_ref[...] = r.astype(BF16)

    @pl.when(j == 3)
    def _():
        v_ref[...] = r
        if transposed:
            put_t(b_ref, r, True)
        else:
            b_ref[...] = r.astype(BF16)


def _proj(x, g, w_in, *, tm, transposed, seq=None):
    m, d = x.shape
    w = w_in.shape[1] // 4
    row = lambda i, j: (i, 0)
    rm_spec = pl.BlockSpec((tm, w), row)
    rm_shape = jax.ShapeDtypeStruct((m, w), BF16)
    f32_shape = jax.ShapeDtypeStruct((m, w), F32)
    kv_spec, kv_shape = rm_spec, f32_shape
    if transposed:
        tc = ATTN_BLOCK
        t_spec = lambda n: pl.BlockSpec((N_HEADS, tm // tc, n, tc), lambda i, j: (0, i, 0, 0))
        t_shape = lambda n: jax.ShapeDtypeStruct((N_HEADS, m // tc, n, tc), BF16)
        extra_specs = [rm_spec, t_spec(HEAD_DIM), t_spec(V_ROWS)]
        extra_shapes = [rm_shape, t_shape(HEAD_DIM), t_shape(V_ROWS)]
        per_seq = seq // tm
        kv_spec = pl.BlockSpec(
            (pl.Element(tm), pl.Element(w)),
            lambda i, j: (pl.multiple_of(
                (i // per_seq) * (N_META + seq) + N_META + (i % per_seq) * tm, N_META), 0))
        kv_shape = jax.ShapeDtypeStruct((m // seq * (N_META + seq), w), F32)
    else:
        extra_specs = [rm_spec] * 3
        extra_shapes = [rm_shape] * 3
    return pl.pallas_call(
        functools.partial(_proj_body, transposed=transposed),
        grid=(m // tm, 4),
        in_specs=[
            pl.BlockSpec((tm, d), row),
            pl.BlockSpec((1, d), lambda i, j: (0, 0)),
            pl.BlockSpec((d, w), lambda i, j: (0, j)),
        ],
        out_specs=[rm_spec, kv_spec, kv_spec] + extra_specs,
        out_shape=[f32_shape, kv_shape, kv_shape] + extra_shapes,
        scratch_shapes=[pltpu.VMEM((tm, d), BF16)],
        compiler_params=_params("parallel", "arbitrary"),
        name="proj",
    )(x, g, w_in)


def _pool_groups(buf_ref, rows, wp_ref, sc_ref, count_fn, write):
    gd = wp_ref.shape[1]
    for gi, w in enumerate(POOL_WINDOWS):
        cs = slice(gi * gd, (gi + 1) * gd)
        cur = buf_ref[HALO:HALO + rows, cs]
        acc = cur
        for dlt in range(1, w):
            acc = acc + buf_ref[HALO - dlt:HALO - dlt + rows, cs]
        feat = acc / count_fn(w) - cur
        out = jnp.dot(feat.astype(BF16), wp_ref[gi], preferred_element_type=F32) * sc_ref[:, cs]
        write(cs, out)


def _pool_big_body(p_ref, halo_ref, pm_ref, wp_ref, sc_ref, o_ref, buf_ref):
    i = pl.program_id(1)
    rows = p_ref.shape[1]

    @pl.when(i == 0)
    def _():
        buf_ref[0:HALO, :] = pm_ref[...]

    @pl.when(i > 0)
    def _():
        buf_ref[0:HALO, :] = halo_ref[0]

    buf_ref[HALO:HALO + rows, :] = p_ref[0]

    def write(cs, val):
        o_ref[0, :, cs] = val.astype(BF16)

    _pool_groups(buf_ref, rows, wp_ref, sc_ref, lambda w: float(w), write)


def _pool_big(p, p_meta, w_pool, pool_scale, *, tm):
    b, t, c = p.shape
    hb = tm // HALO
    return pl.pallas_call(
        _pool_big_body,
        grid=(b, t // tm),
        in_specs=[
            pl.BlockSpec((1, tm, c), lambda bi, i: (bi, i, 0)),
            pl.BlockSpec((1, HALO, c), lambda bi, i: (bi, jnp.maximum(i * hb - 1, 0), 0)),
            pl.BlockSpec((HALO, c), lambda bi, i: (0, 0)),
            pl.BlockSpec(w_pool.shape, lambda bi, i: (0, 0, 0)),
            pl.BlockSpec((1, c), lambda bi, i: (0, 0)),
        ],
        out_specs=pl.BlockSpec((1, tm, c), lambda bi, i: (bi, i, 0)),
        out_shape=jax.ShapeDtypeStruct((b, t, c), BF16),
        scratch_shapes=[pltpu.VMEM((HALO + tm, c), F32)],
        compiler_params=_params("parallel", "arbitrary"),
        name="pool_big",
    )(p, p, p_meta, w_pool, pool_scale)


def _pool_small_body(p_ref, st_ref, wp_ref, sc_ref, o_ref, buf_ref, *, n_seq, t_dec):
    buf_ref[0:HALO, :] = jnp.zeros((HALO, buf_ref.shape[1]), F32)
    buf_ref[HALO:2 * HALO, :] = p_ref[0:N_META, :]
    pos1 = lax.broadcasted_iota(jnp.int32, (N_META, 1), 0).astype(F32) + 1.0

    def write_meta(cs, val):
        o_ref[0:N_META, cs] = val

    _pool_groups(buf_ref, N_META, wp_ref, sc_ref, lambda w: jnp.minimum(pos1, float(w)), write_meta)

    o_ref[N_META + n_seq * t_dec:, :] = jnp.zeros(
        (o_ref.shape[0] - N_META - n_seq * t_dec, o_ref.shape[1]), F32)
    for s in range(n_seq):
        r0 = N_META + s * t_dec
        buf_ref[0:HALO, :] = st_ref[s]
        buf_ref[HALO:HALO + 8, :] = p_ref[r0:r0 + 8, :]

        def write_s(cs, val, r0=r0):
            o_ref[r0:r0 + t_dec, cs] = val[0:t_dec]

        _pool_groups(buf_ref, 8, wp_ref, sc_ref, lambda w: float(w), write_s)


def _pool_small(p_small, state16, w_pool, pool_scale, *, n_seq, t_dec):
    m, c = p_small.shape
    return pl.pallas_call(
        functools.partial(_pool_small_body, n_seq=n_seq, t_dec=t_dec),
        out_shape=jax.ShapeDtypeStruct((m, c), F32),
        scratch_shapes=[pltpu.VMEM((2 * HALO, c), F32)],
        compiler_params=pltpu.CompilerParams(vmem_limit_bytes=VMEM_LIMIT),
        name="pool_small",
    )(p_small, state16, w_pool, pool_scale)


def _lambda(lq1_ref, lk1_ref, lq2_ref, lk2_ref, lam_init):
    a = jnp.sum(lq1_ref[...] * lk1_ref[...], axis=-1, keepdims=True)
    b = jnp.sum(lq2_ref[...] * lk2_ref[...], axis=-1, keepdims=True)
    return jnp.exp(a) - jnp.exp(b) + lam_init


def _split_halves(q, axis):
    idx = lax.broadcasted_iota(jnp.int32, q.shape, axis) % HEAD_DIM
    zero = jnp.zeros_like(q)
    return jnp.where(idx < QK_HALF, q, zero), jnp.where(idx >= QK_HALF, q, zero)


def _nt_dot(a, b):
    return lax.dot_general(a, b, (((1,), (1,)), ((), ())), preferred_element_type=F32)


def _attn_big_body(lq1_ref, lk1_ref, lq2_ref, lk2_ref, gain_ref, pos_ref, posm_ref,
                   qt_ref, k_ref, vt_ref, km_ref, vmt_ref, o_ref,
                   m_ref, acc_ref, p_ref, al_ref, *, lam_init):
    blk = ATTN_BLOCK
    n_blk = qt_ref.shape[1]
    lam = _lambda(lq1_ref, lk1_ref, lq2_ref, lk2_ref, lam_init)
    diag = (lax.broadcasted_iota(jnp.int32, (blk, blk), 0)
            <= lax.broadcasted_iota(jnp.int32, (blk, blk), 1))
    nq = Q_GROUP

    def q_group(g, carry):
        q_ext = []
        for qh in range(nq):
            q = qt_ref[0, nq * g + qh].astype(F32)
            rid = lax.broadcasted_iota(jnp.int32, q.shape, 0)
            ones = jnp.where(rid < 2, 1.0, 0.0).astype(BF16)
            q_ext.append(tuple(
                jnp.concatenate([jnp.where(keep, q, 0.0).astype(BF16), ones], axis=0)
                for keep in (rid < QK_HALF, rid >= QK_HALF)))
        m_ref[...] = jnp.full(m_ref.shape, NEG_INF, F32)
        acc_ref[...] = jnp.zeros(acc_ref.shape, F32)
        p_ref[...] = jnp.zeros(p_ref.shape, BF16)
        al_ref[...] = jnp.ones(al_ref.shape, F32)

        def scores(slot, kb, kpos, targets):
            n_keys = kb.shape[0]
            sub = min(KEY_SUB, n_keys)
            for ki in range(n_keys // sub):
                rs = slice(ki * sub, (ki + 1) * sub)
                k_ext = jnp.concatenate([kb[rs], kpos[rs]], axis=1)
                for qh, mask in targets:
                    for mi in range(2):
                        st = 2 * qh + mi
                        s = jnp.dot(k_ext, q_ext[qh][mi], preferred_element_type=F32)
                        if mask is not None:
                            s = jnp.where(mask[rs], s, NEG_INF)
                        m_prev = m_ref[st]
                        m_new = jnp.maximum(m_prev, jnp.max(s, axis=0, keepdims=True))
                        al_ref[slot, ki, st] = jnp.exp(m_prev - m_new)
                        p_ref[slot, st, rs, :] = jnp.exp(s - m_new).astype(BF16)
                        m_ref[st] = m_new

        def values(slot, vt, first_q):
            n_keys = vt.shape[1]
            sub = min(KEY_SUB, n_keys)
            for ki in range(n_keys // sub):
                rs = slice(ki * sub, (ki + 1) * sub)
                for st in range(2 * first_q, 2 * nq):
                    acc_ref[st] = al_ref[slot, ki, st] * acc_ref[st] + jnp.dot(
                        vt[:, rs], p_ref[slot, st, rs, :], preferred_element_type=F32)

        def keys(kj):
            k0 = pl.multiple_of(kj * blk, blk)
            return k_ref[pl.ds(k0, blk), :], pos_ref[pl.ds(k0, blk), :]

        everyone = [(qh, None) for qh in range(nq)]
        first_key = nq * g

        def block_pair(t, c2):
            values(1, vt_ref[0, jnp.maximum(2 * t - 1, 0)], 0)
            scores(0, *keys(2 * t), everyone)
            values(0, vt_ref[0, 2 * t], 0)
            scores(1, *keys(2 * t + 1), everyone)
            return c2

        lax.fori_loop(0, first_key // 2, block_pair, 0)
        values(1, vt_ref[0, jnp.maximum(first_key - 1, 0)], 0)
        for j in range(nq):
            scores(j % 2, *keys(first_key + j), [(j, diag)] + everyone[j + 1:])
            values(j % 2, vt_ref[0, first_key + j], j)
        scores(nq % 2, km_ref[...], posm_ref[...], everyone)
        values(nq % 2, vmt_ref[0], 0)

        for qh in range(nq):
            a1, a2 = acc_ref[2 * qh], acc_ref[2 * qh + 1]
            o = (a1[0:HEAD_DIM] / a1[HEAD_DIM:HEAD_DIM + 1]
                 - lam * (a2[0:HEAD_DIM] / a2[HEAD_DIM:HEAD_DIM + 1]))
            o = o * lax.rsqrt(jnp.mean(o * o, axis=0, keepdims=True) + EPS) * gain_ref[...]
            r0 = pl.multiple_of((first_key + qh) * blk, blk)
            o_ref[pl.ds(r0, blk), :] = (o * (1.0 - lam_init)).T.astype(BF16)
        return carry

    assert n_blk % nq == 0 and nq % 2 == 0
    lax.fori_loop(0, n_blk // nq, q_group, 0)


def _position_blocks(slopes, first, n):
    pos = jnp.arange(first, first + n, dtype=jnp.int32)
    coarse = slopes[:, None] * ((pos // QK_HALF) * QK_HALF).astype(F32)[None, :]
    fine = slopes[:, None] * (pos % QK_HALF).astype(F32)[None, :]
    blocks = jnp.zeros((slopes.shape[0], n, HEAD_DIM), F32)
    return blocks.at[:, :, 0].set(coarse).at[:, :, 1].set(fine).astype(BF16)


def _attn_big(q_t, kb, v_t, k_meta, vm_t, slopes, lams, gain_col, *, lam_init, bsz):
    n_heads, n_chunks, hd, blk = q_t.shape
    m, c = kb.shape
    seq = m // bsz
    n_blk = seq // blk
    vec = lambda n: pl.BlockSpec((1, n), lambda bi, h: (0, 0))
    pos = _position_blocks(slopes, 0, seq)
    pos_meta = _position_blocks(slopes, -N_META, N_META)
    return pl.pallas_call(
        functools.partial(_attn_big_body, lam_init=lam_init),
        grid=(bsz, n_heads),
        in_specs=[
            vec(QK_HALF), vec(QK_HALF), vec(QK_HALF), vec(QK_HALF),
            pl.BlockSpec((hd, 1), lambda bi, h: (0, 0)),
            pl.BlockSpec((None, seq, hd), lambda bi, h: (h, 0, 0)),
            pl.BlockSpec((None, N_META, hd), lambda bi, h: (h, 0, 0)),
            pl.BlockSpec((1, n_blk, hd, blk), lambda bi, h: (h, bi, 0, 0)),
            pl.BlockSpec((seq, hd), lambda bi, h: (bi, h)),
            pl.BlockSpec((1, n_blk, V_ROWS, blk), lambda bi, h: (h, bi, 0, 0)),
            pl.BlockSpec((N_META, hd), lambda bi, h: (0, h)),
            pl.BlockSpec((1, V_ROWS, N_META), lambda bi, h: (h, 0, 0)),
        ],
        out_specs=pl.BlockSpec((seq, hd), lambda bi, h: (bi, h)),
        out_shape=jax.ShapeDtypeStruct((m, c), BF16),
        scratch_shapes=[
            pltpu.VMEM((2 * Q_GROUP, 1, blk), F32),
            pltpu.VMEM((2 * Q_GROUP, V_ROWS, blk), F32),
            pltpu.VMEM((2, 2 * Q_GROUP, blk, blk), BF16),
            pltpu.VMEM((2, blk // KEY_SUB, 2 * Q_GROUP, 1, blk), F32),
        ],
        compiler_params=_params("parallel", "arbitrary"),
        name="attn_big",
    )(*lams, gain_col, pos, pos_meta, q_t, kb, v_t, k_meta, vm_t)


def _attn_meta_body(slope_ref, lq1_ref, lk1_ref, lq2_ref, lk2_ref, gain_ref,
                    q_ref, k_ref, v_ref, o_ref, *, lam_init):
    h = pl.program_id(0)
    slope = slope_ref[h]
    lam = _lambda(lq1_ref, lk1_ref, lq2_ref, lk2_ref, lam_init)
    qs = _split_halves(q_ref[...], 1)
    kb = k_ref[...]
    shape = (N_META, kb.shape[0])
    qpos = lax.broadcasted_iota(jnp.int32, shape, 0)
    kpos = lax.broadcasted_iota(jnp.int32, shape, 1)
    dist = (qpos - kpos).astype(F32)
    mask = (qpos >= kpos) & (kpos < N_META)
    outs = []
    for mi in range(2):
        s = jnp.where(mask, _nt_dot(qs[mi], kb) - slope * dist, NEG_INF)
        p = jnp.exp(s - jnp.max(s, axis=-1, keepdims=True))
        p = p / jnp.sum(p, axis=-1, keepdims=True)
        outs.append(p)
    a = (outs[0] - lam * outs[1]).astype(BF16)
    o = jnp.dot(a, v_ref[...], preferred_element_type=F32)
    o_ref[...] = (_rms(o, gain_ref[...]) * (1.0 - lam_init)).astype(BF16)


def _attn_meta(q_s, kb_s, vb_s, slopes, lams, gain, *, lam_init):
    m, c = kb_s.shape
    vec = lambda n: pl.BlockSpec((1, n), lambda h: (0, 0))
    return pl.pallas_call(
        functools.partial(_attn_meta_body, lam_init=lam_init),
        grid=(N_HEADS,),
        in_specs=[
            pl.BlockSpec(memory_space=pltpu.SMEM),
            vec(QK_HALF), vec(QK_HALF), vec(QK_HALF), vec(QK_HALF), vec(HEAD_DIM),
            pl.BlockSpec((N_META, HEAD_DIM), lambda h: (0, h)),
            pl.BlockSpec((m, HEAD_DIM), lambda h: (0, h)),
            pl.BlockSpec((m, HEAD_DIM), lambda h: (0, h)),
        ],
        out_specs=pl.BlockSpec((N_META, HEAD_DIM), lambda h: (0, h)),
        out_shape=jax.ShapeDtypeStruct((N_META, c), BF16),
        compiler_params=_params("parallel"),
        name="attn_meta",
    )(slopes, *lams, gain, q_s, kb_s, vb_s)


def _attn_sample_body(pt_ref, lq1_ref, lk1_ref, lq2_ref, lk2_ref, gain_ref,
                      q_ref, kn_ref, vn_ref, *rest, lam_init, t_dec, past_len):
    n_pg = PAGES_PER_STEP
    k_refs, v_refs = rest[:n_pg], rest[n_pg:2 * n_pg]
    o_ref, qm_ref, m_ref, l_ref, acc_ref, bias_ref, p_ref, al_ref = rest[2 * n_pg:]
    pg = pl.program_id(1)
    n_steps = pl.num_programs(1) - 1
    page = k_refs[0].shape[0]
    grp = 2 * t_dec
    rows = grp * N_HEADS
    cols = page * N_HEADS
    row1 = lax.broadcasted_iota(jnp.int32, (rows, 1), 0)
    r_q = row1 % t_dec
    slope = jnp.exp2(-((row1 // grp) + 1).astype(F32))

    def update(kb, vb, bias, shift):
        s = _nt_dot(qm_ref[...], kb) + bias
        m_prev = m_ref[...]
        m_new = jnp.maximum(m_prev, jnp.max(s, axis=-1, keepdims=True) + shift)
        alpha = jnp.exp(m_prev - m_new)
        p = jnp.exp(s - (m_new - shift))
        l_ref[...] = alpha * l_ref[...] + jnp.sum(p, axis=-1, keepdims=True)
        acc_ref[...] = alpha * acc_ref[...] + jnp.dot(p.astype(BF16), vb, preferred_element_type=F32)
        m_ref[...] = m_new

    @pl.when(pg == 0)
    def _():
        q = q_ref[0]
        r2 = lax.broadcasted_iota(jnp.int32, q.shape, 0)
        c2 = lax.broadcasted_iota(jnp.int32, q.shape, 1)
        qm_ref[...] = jnp.where((c2 // QK_HALF) == (r2 % grp) // t_dec, q, jnp.zeros_like(q))
        m_ref[...] = jnp.full(m_ref.shape, NEG_INF, F32)
        l_ref[...] = jnp.zeros(l_ref.shape, F32)
        acc_ref[...] = jnp.zeros(acc_ref.shape, F32)
        p_ref[...] = jnp.zeros(p_ref.shape, BF16)
        al_ref[...] = jnp.ones(al_ref.shape, F32)
        rw = lax.broadcasted_iota(jnp.int32, (rows, cols), 0)
        cl = lax.broadcasted_iota(jnp.int32, (rows, cols), 1)
        bias_ref[...] = jnp.where((cl % N_HEADS) == (rw // grp),
                                  slope * (cl // N_HEADS).astype(F32), NEG_INF)
        rw = lax.broadcasted_iota(jnp.int32, (rows, t_dec * N_HEADS), 0)
        cl = lax.broadcasted_iota(jnp.int32, (rows, t_dec * N_HEADS), 1)
        r_k = cl // N_HEADS
        ok = ((cl % N_HEADS) == (rw // grp)) & (r_k <= rw % t_dec)
        bias_new = jnp.where(ok, -slope * (rw % t_dec - r_k).astype(F32), NEG_INF)
        update(kn_ref[0], vn_ref[0], bias_new, 0.0)

    n_grp = n_pg // SAMPLE_GROUP

    def values():
        acc = acc_ref[...]
        for g in range(n_grp):
            acc = al_ref[g] * acc
            for gi in range(g * SAMPLE_GROUP, (g + 1) * SAMPLE_GROUP):
                vb = v_refs[gi][...].reshape(cols, HEAD_DIM).astype(BF16)
                acc = acc + jnp.dot(p_ref[:, gi * cols:(gi + 1) * cols], vb, preferred_element_type=F32)
        acc_ref[...] = acc

    @pl.when(pg < n_steps)
    def _():
        values()
        qm = qm_ref[...]
        m_run, l_run = m_ref[...], l_ref[...]
        for g in range(n_grp):
            s_pg, top = [], None
            for gi in range(g * SAMPLE_GROUP, (g + 1) * SAMPLE_GROUP):
                start = (pg * n_pg + gi) * page
                shift = slope * (start - past_len - r_q).astype(F32)
                kb = k_refs[gi][...].reshape(cols, HEAD_DIM).astype(BF16)
                s = _nt_dot(qm, kb) + bias_ref[...]
                s_pg.append((gi, s, shift))
                top_g = jnp.max(s, axis=-1, keepdims=True) + shift
                top = top_g if top is None else jnp.maximum(top, top_g)
            m_new = jnp.maximum(m_run, top)
            alpha = jnp.exp(m_run - m_new)
            l_run = alpha * l_run
            for gi, s, shift in s_pg:
                p = jnp.exp(s + (shift - m_new))
                l_run = l_run + jnp.sum(p, axis=-1, keepdims=True)
                p_ref[:, gi * cols:(gi + 1) * cols] = p.astype(BF16)
            al_ref[g] = alpha
            m_run = m_new
        m_ref[...] = m_run
        l_ref[...] = l_run

    @pl.when(pg == n_steps)
    def _():
        values()
        lam = _lambda(lq1_ref, lk1_ref, lq2_ref, lk2_ref, lam_init)
        for hd in range(N_HEADS):
            rs = slice(hd * grp, (hd + 1) * grp)
            n = acc_ref[rs, :] / l_ref[rs, :]
            o = n - lam * pltpu.roll(n, t_dec, 0)
            o_ref[0, rs, :] = _rms(o, gain_ref[...]) * (1.0 - lam_init)


def _attn_sample(page_table, q_rows, k_new, v_new, cache_k, cache_v, lams, gain, *, lam_init, t_dec):
    n_seq, n_pages = page_table.shape
    _, _, page, n_heads, hd = cache_k.shape
    n_pg = PAGES_PER_STEP
    assert n_pages % n_pg == 0 and n_heads == N_HEADS and hd == HEAD_DIM and 2 * t_dec == 8
    rows = 2 * t_dec * N_HEADS
    vec = lambda n: pl.BlockSpec((1, n), lambda s, p, pt: (0, 0))

    n_steps = n_pages // n_pg

    def page_spec(gi, lag):
        def index(s, p, pt):
            grp = jnp.maximum(p - 1, 0) if lag else jnp.minimum(p, n_steps - 1)
            return (0, pt[s, grp * n_pg + gi], 0, 0, 0)
        return pl.BlockSpec((None, None, page, n_heads, hd), index)

    grid_spec = pltpu.PrefetchScalarGridSpec(
        num_scalar_prefetch=1,
        grid=(n_seq, n_steps + 1),
        in_specs=[
            vec(QK_HALF), vec(QK_HALF), vec(QK_HALF), vec(QK_HALF), vec(HEAD_DIM),
            pl.BlockSpec((1, rows, hd), lambda s, p, pt: (s, 0, 0)),
            pl.BlockSpec((1, t_dec * n_heads, hd), lambda s, p, pt: (s, 0, 0)),
            pl.BlockSpec((1, t_dec * n_heads, hd), lambda s, p, pt: (s, 0, 0)),
        ] + [page_spec(gi, False) for gi in range(n_pg)] + [page_spec(gi, True) for gi in range(n_pg)],
        out_specs=pl.BlockSpec((1, rows, hd), lambda s, p, pt: (s, 0, 0)),
        scratch_shapes=[
            pltpu.VMEM((rows, hd), BF16),
            pltpu.VMEM((rows, 1), F32),
            pltpu.VMEM((rows, 1), F32),
            pltpu.VMEM((rows, hd), F32),
            pltpu.VMEM((rows, page * n_heads), F32),
            pltpu.VMEM((rows, n_pg * page * n_heads), BF16),
            pltpu.VMEM((n_pg // SAMPLE_GROUP, rows, 1), F32),
        ],
    )
    return pl.pallas_call(
        functools.partial(_attn_sample_body, lam_init=lam_init, t_dec=t_dec, past_len=n_pages * page),
        grid_spec=grid_spec,
        out_shape=jax.ShapeDtypeStruct((n_seq, rows, hd), F32),
        compiler_params=_params("parallel", "arbitrary"),
        name="attn_sample",
    )(page_table, *lams, gain, q_rows, k_new, v_new,
      *([cache_k] * n_pg), *([cache_v] * n_pg))


def _outproj_body(x_ref, a_ref, b_ref, w_ref, o_ref):
    half = a_ref.shape[1]
    o_ref[...] = (x_ref[...]
                  + jnp.dot(a_ref[...], w_ref[0:half, :], preferred_element_type=F32)
                  + jnp.dot(b_ref[...], w_ref[half:, :], preferred_element_type=F32))


def _outproj(x, pool_out, attn_out, w_out, *, tm):
    m, d = x.shape
    c = pool_out.shape[1]
    return pl.pallas_call(
        _outproj_body,
        grid=(m // tm,),
        in_specs=[
            pl.BlockSpec((tm, d), lambda i: (i, 0)),
            pl.BlockSpec((tm, c), lambda i: (i, 0)),
            pl.BlockSpec((tm, c), lambda i: (i, 0)),
            pl.BlockSpec(w_out.shape, lambda i: (0, 0)),
        ],
        out_specs=pl.BlockSpec((tm, d), lambda i: (i, 0)),
        out_shape=jax.ShapeDtypeStruct((m, d), F32),
        compiler_params=_params("parallel"),
        name="outproj",
    )(x, pool_out, attn_out, w_out)


def kernel(x_prompt, x_sample, cache_k, cache_v, state_pool, page_table, meta_tokens,
           norm_ffn1, w_gate1, w_up1, w_down1, norm_mix, w_in, w_pool, pool_scale,
           lambda_q1, lambda_k1, lambda_q2, lambda_k2, subln_gain, w_out,
           norm_ffn2, w_gate2, w_up2, w_down2, norm_final):
    depth = w_in.shape[0]
    assert depth == 1
    bsz, seq, d = x_prompt.shape
    n_seq, t_dec, _ = x_sample.shape
    c = N_HEADS * HEAD_DIM
    n_small = N_META + n_seq * t_dec
    assert n_small <= SMALL_ROWS and 2 * t_dec == 8
    lam_init = 0.8 - 0.6 * math.exp(-0.3 * 0)

    row = lambda a: a.reshape(1, -1)
    w_in_b, w_out_b, w_pool_b = w_in[0].astype(BF16), w_out[0].astype(BF16), w_pool[0].astype(BF16)
    g1, gm, g2, gf = row(norm_ffn1[0]), row(norm_mix[0]), row(norm_ffn2[0]), row(norm_final)
    pscale, gain = row(pool_scale[0]), row(subln_gain[0])
    lams = (row(lambda_q1[0]), row(lambda_k1[0]), row(lambda_q2[0]), row(lambda_k2[0]))
    slopes = jnp.exp2(-jnp.arange(1, N_HEADS + 1, dtype=F32))

    x_big = x_prompt.reshape(bsz * seq, d)
    x_small = jnp.concatenate([meta_tokens, x_sample.reshape(n_seq * t_dec, d),
                               jnp.zeros((SMALL_ROWS - n_small, d), F32)], axis=0)

    xb1, xs1 = _ffn(x_big, x_small, g1, w_gate1[0], w_up1[0], w_down1[0], gf, final_norm=False)

    ps, ks, vs, kbs, qs, vbs = _proj(xs1, gm, w_in_b, tm=SMALL_ROWS, transposed=False)
    state16 = jnp.pad(state_pool[0], ((0, 0), (HALO - POOL_BUF, 0), (0, 0)))
    pool_s = _pool_small(ps, state16, w_pool_b, pscale, n_seq=n_seq, t_dec=t_dec).astype(BF16)
    o_meta = _attn_meta(qs, kbs, vbs, slopes, lams, gain, lam_init=lam_init)
    q_dec = qs[N_META:n_small].reshape(n_seq, t_dec, N_HEADS, HEAD_DIM).transpose(0, 2, 1, 3)
    q_rows = jnp.broadcast_to(q_dec[:, :, None], (n_seq, N_HEADS, 2, t_dec, HEAD_DIM))
    q_rows = q_rows.reshape(n_seq, 2 * t_dec * N_HEADS, HEAD_DIM)
    k_new = kbs[N_META:n_small].reshape(n_seq, t_dec * N_HEADS, HEAD_DIM)
    v_new = vbs[N_META:n_small].reshape(n_seq, t_dec * N_HEADS, HEAD_DIM)
    o_dec = _attn_sample(page_table, q_rows, k_new, v_new, cache_k, cache_v, lams, gain,
                         lam_init=lam_init, t_dec=t_dec)
    o_dec = o_dec.reshape(n_seq, N_HEADS, 2, t_dec, HEAD_DIM)[:, :, 0].transpose(0, 2, 1, 3)
    o_small = jnp.concatenate([o_meta, o_dec.reshape(n_seq * t_dec, c).astype(BF16),
                               jnp.zeros((SMALL_ROWS - n_small, c), BF16)], axis=0)
    xs2 = _outproj(xs1, pool_s, o_small, w_out_b, tm=SMALL_ROWS)

    pb, kb, vb, kbb, q_t, v_t = _proj(xb1, gm, w_in_b, tm=512, transposed=True, seq=seq)
    pool_b = _pool_big(pb.reshape(bsz, seq, c), ps[:N_META], w_pool_b, pscale, tm=512)
    vm_t = vbs[:N_META].reshape(N_META, N_HEADS, HEAD_DIM).transpose(1, 2, 0)
    ones_rows = jnp.zeros((N_HEADS, V_ROWS - HEAD_DIM, N_META), BF16).at[:, 0].set(1.0)
    vm_t = jnp.concatenate([vm_t, ones_rows], axis=1)
    o_big = _attn_big(q_t, kbb, v_t, kbs[:N_META], vm_t, slopes, lams, gain.reshape(HEAD_DIM, 1),
                      lam_init=lam_init, bsz=bsz)
    xb2 = _outproj(xb1, pool_b.reshape(bsz * seq, c), o_big, w_out_b, tm=512)
    yb, ys = _ffn(xb2, xs2, g2, w_gate2[0], w_up2[0], w_down2[0], gf, final_norm=True)

    def with_meta(small, big):
        meta = jnp.broadcast_to(small[None, :N_META], (bsz, N_META, c))
        full = lax.dynamic_update_slice(big.reshape(bsz, N_META + seq, c), meta, (0, 0, 0))
        return full.reshape(1, bsz, N_META + seq, N_HEADS, HEAD_DIM)

    dec = lambda a: a[N_META:n_small].reshape(1, n_seq, t_dec, N_HEADS, HEAD_DIM)
    y_prompt = yb.reshape(bsz, seq, d)
    y_sample = ys[N_META:n_small].reshape(n_seq, t_dec, d)
    pool_prompt = pb.reshape(bsz, seq, c)[:, seq - POOL_BUF:][None]
    pool_sample = jnp.concatenate([state_pool[0][:, t_dec:], ps[N_META:n_small].reshape(n_seq, t_dec, c)],
                                  axis=1)[None]
    return (y_prompt, y_sample, with_meta(ks, kb), with_meta(vs, vb), pool_prompt,
            dec(ks), dec(vs), pool_sample)
```

```python
import functools
import math

import jax
import jax.numpy as jnp
from jax import lax
from jax.experimental import pallas as pl
from jax.experimental.pallas import tpu as pltpu

F32 = jnp.float32
BF16 = jnp.bfloat16

N_META = 16
POOL_WINDOWS = (2, 4, 8, 16)
POOL_BUF = max(POOL_WINDOWS) - 1
HALO = 16
N_HEADS = 8
HEAD_DIM = 128
QK_HALF = HEAD_DIM // 2
EPS = 1e-6
NEG_INF = -1e30
SMALL_ROWS = 128
ATTN_BLOCK = 256
V_ROWS = HEAD_DIM + 16
Q_GROUP = 4
KEY_SUB = 128
FFN_ROWS = 1024
FFN_COLS = 256
SAMPLE_GROUP = 2
PAGES_PER_STEP = 16

VMEM_LIMIT = 60 * 1024 * 1024


def _params(*sem):
    return pltpu.CompilerParams(dimension_semantics=sem, vmem_limit_bytes=VMEM_LIMIT)


def _rms(x, g):
    return x * lax.rsqrt(jnp.mean(x * x, axis=-1, keepdims=True) + EPS) * g


def _ffn_body(xb_ref, xs_ref, g_ref, wg_ref, wu_ref, wd_ref, gf_ref, ob_ref, os_ref,
              hb_ref, hs_ref, wgb_ref, wub_ref, wdb_ref, *, final_norm):
    i, s = pl.program_id(0), pl.program_id(1)
    n_blk = pl.num_programs(1) - 1

    def round_weights():
        slot = s % 2
        wgb_ref[slot] = wg_ref[...].astype(BF16)
        wub_ref[slot] = wu_ref[...].astype(BF16)
        wdb_ref[slot] = wd_ref[...].astype(BF16)

    def start(x_ref, o_ref, h_ref):
        h_ref[...] = _rms(x_ref[...], g_ref[...]).astype(BF16)
        o_ref[...] = jnp.zeros_like(o_ref)

    def accumulate(o_ref, h_ref):
        slot = (s + 1) % 2
        h = h_ref[...]
        g = jnp.dot(h, wgb_ref[slot], preferred_element_type=F32)
        u = jnp.dot(h, wub_ref[slot], preferred_element_type=F32)
        a = (g * jax.nn.sigmoid(g) * u).astype(BF16)
        o_ref[...] += jnp.dot(a, wdb_ref[slot], preferred_element_type=F32)

    def finish(x_ref, o_ref):
        y = x_ref[...] + 0.5 * o_ref[...]
        if final_norm:
            y = _rms(y, gf_ref[...])
        o_ref[...] = y

    last_tile = i == pl.num_programs(0) - 1

    @pl.when(s == 0)
    def _():
        round_weights()
        start(xb_ref, ob_ref, hb_ref)

        @pl.when(last_tile)
        def _():
            start(xs_ref, os_ref, hs_ref)

    @pl.when(s > 0)
    def _():
        accumulate(ob_ref, hb_ref)
        round_weights()

        @pl.when(last_tile)
        def _():
            accumulate(os_ref, hs_ref)

    @pl.when(s == n_blk)
    def _():
        finish(xb_ref, ob_ref)

        @pl.when(last_tile)
        def _():
            finish(xs_ref, os_ref)


def _ffn(x_big, x_small, g, wg, wu, wd, gf, *, final_norm):
    m, d = x_big.shape
    ms = x_small.shape[0]
    f = wg.shape[1]
    tm, tf = FFN_ROWS, FFN_COLS
    assert m % tm == 0 and f % tf == 0
    n_blk = f // tf
    const = lambda i, s: (0, 0)
    blk = lambda s: jnp.minimum(s, n_blk - 1)
    return pl.pallas_call(
        functools.partial(_ffn_body, final_norm=final_norm),
        grid=(m // tm, n_blk + 1),
        in_specs=[
            pl.BlockSpec((tm, d), lambda i, s: (i, 0), pipeline_mode=pl.Buffered(1)),
            pl.BlockSpec((ms, d), const),
            pl.BlockSpec((1, d), const),
            pl.BlockSpec((d, tf), lambda i, s: (0, blk(s))),
            pl.BlockSpec((d, tf), lambda i, s: (0, blk(s))),
            pl.BlockSpec((tf, d), lambda i, s: (blk(s), 0)),
            pl.BlockSpec((1, d), const),
        ],
        out_specs=[pl.BlockSpec((tm, d), lambda i, s: (i, 0)), pl.BlockSpec((ms, d), const)],
        out_shape=[jax.ShapeDtypeStruct((m, d), F32), jax.ShapeDtypeStruct((ms, d), F32)],
        scratch_shapes=[
            pltpu.VMEM((tm, d), BF16), pltpu.VMEM((ms, d), BF16),
            pltpu.VMEM((2, d, tf), BF16), pltpu.VMEM((2, d, tf), BF16), pltpu.VMEM((2, tf, d), BF16),
        ],
        compiler_params=_params("arbitrary", "arbitrary"),
        name="ffn",
    )(x_big, x_small, g, wg, wu, wd, gf)


def _proj_body(x_ref, g_ref, w_ref, p_ref, k_ref, v_ref, kb_ref, a_ref, b_ref, *, transposed):
    h = _rms(x_ref[...], g_ref[...]).astype(BF16)
    w = p_ref.shape[1]

    def block(j):
        return jnp.dot(h, w_ref[:, j * w:(j + 1) * w], preferred_element_type=F32)

    def put_t(dst, val, extra_ones):
        tc = dst.shape[3]
        for hd in range(N_HEADS):
            for ci in range(dst.shape[1]):
                blk = val[ci * tc:(ci + 1) * tc, hd * HEAD_DIM:(hd + 1) * HEAD_DIM]
                dst[hd, ci, 0:HEAD_DIM, :] = blk.T.astype(BF16)
                if extra_ones:
                    rid = lax.broadcasted_iota(jnp.int32, (dst.shape[2] - HEAD_DIM, tc), 0)
                    dst[hd, ci, HEAD_DIM:, :] = jnp.where(rid == 0, 1.0, 0.0).astype(BF16)

    p_ref[...] = block(0)
    q = block(1) * (QK_HALF ** -0.5)
    if transposed:
        put_t(a_ref, q, False)
    else:
        a_ref[...] = q.astype(BF16)
    k = block(2)
    k_ref[...] = k
    kb_ref[...] = k.astype(BF16)
    v = block(3)
    v_ref[...] = v
    if transposed:
        put_t(b_ref, v, True)
    else:
        b_ref[...] = v.astype(BF16)


def _proj(x, g, w_in, *, tm, transposed, seq=None):
    m, d = x.shape
    w = w_in.shape[1] // 4
    row = lambda i: (i, 0)
    rm_spec = pl.BlockSpec((tm, w), row)
    rm_shape = jax.ShapeDtypeStruct((m, w), BF16)
    f32_shape = jax.ShapeDtypeStruct((m, w), F32)
    kv_spec, kv_shape = rm_spec, f32_shape
    if transposed:
        tc = ATTN_BLOCK
        t_spec = lambda n: pl.BlockSpec((N_HEADS, tm // tc, n, tc), lambda i: (0, i, 0, 0))
        t_shape = lambda n: jax.ShapeDtypeStruct((N_HEADS, m // tc, n, tc), BF16)
        extra_specs = [rm_spec, t_spec(HEAD_DIM), t_spec(V_ROWS)]
        extra_shapes = [rm_shape, t_shape(HEAD_DIM), t_shape(V_ROWS)]
        per_seq = seq // tm
        kv_spec = pl.BlockSpec(
            (pl.Element(tm), pl.Element(w)),
            lambda i: (pl.multiple_of(
                (i // per_seq) * (N_META + seq) + N_META + (i % per_seq) * tm, N_META), 0))
        kv_shape = jax.ShapeDtypeStruct((m // seq * (N_META + seq), w), F32)
    else:
        extra_specs = [rm_spec] * 3
        extra_shapes = [rm_shape] * 3
    return pl.pallas_call(
        functools.partial(_proj_body, transposed=transposed),
        grid=(m // tm,),
        in_specs=[
            pl.BlockSpec((tm, d), row),
            pl.BlockSpec((1, d), lambda i: (0, 0)),
            pl.BlockSpec(w_in.shape, lambda i: (0, 0), pipeline_mode=pl.Buffered(1)),
        ],
        out_specs=[rm_spec, kv_spec, kv_spec] + extra_specs,
        out_shape=[f32_shape, kv_shape, kv_shape] + extra_shapes,
        compiler_params=_params("parallel"),
        name="proj",
    )(x, g, w_in)


def _fill_meta_body(km_ref, vm_ref, k_in_ref, v_in_ref, k_ref, v_ref):
    del k_in_ref, v_in_ref
    k_ref[...] = km_ref[...]
    v_ref[...] = vm_ref[...]


def _fill_meta(k_rows, v_rows, k_meta, v_meta, *, bsz):
    m, w = k_rows.shape
    per_seq = m // bsz
    assert per_seq % N_META == 0
    meta_spec = pl.BlockSpec((N_META, w), lambda b: (0, 0))
    out_spec = pl.BlockSpec((N_META, w), lambda b: (b * (per_seq // N_META), 0))
    shape = jax.ShapeDtypeStruct((m, w), k_rows.dtype)
    return pl.pallas_call(
        _fill_meta_body,
        grid=(bsz,),
        in_specs=[meta_spec, meta_spec, pl.BlockSpec(memory_space=pl.ANY), pl.BlockSpec(memory_space=pl.ANY)],
        out_specs=[out_spec, out_spec],
        out_shape=[shape, shape],
        input_output_aliases={2: 0, 3: 1},
        compiler_params=_params("arbitrary"),
        name="fill_meta",
    )(k_meta, v_meta, k_rows, v_rows)


def _pool_groups(buf_ref, rows, wp_ref, sc_ref, count_fn, write):
    gd = wp_ref.shape[1]
    for gi, w in enumerate(POOL_WINDOWS):
        cs = slice(gi * gd, (gi + 1) * gd)
        cur = buf_ref[HALO:HALO + rows, cs]
        acc = cur
        for dlt in range(1, w):
            acc = acc + buf_ref[HALO - dlt:HALO - dlt + rows, cs]
        feat = acc / count_fn(w) - cur
        out = jnp.dot(feat.astype(BF16), wp_ref[gi], preferred_element_type=F32) * sc_ref[:, cs]
        write(cs, out)


def _pool_big_body(p_ref, halo_ref, pm_ref, wp_ref, sc_ref, o_ref, buf_ref):
    i = pl.program_id(1)
    rows = p_ref.shape[1]

    @pl.when(i == 0)
    def _():
        buf_ref[0:HALO, :] = pm_ref[...]

    @pl.when(i > 0)
    def _():
        buf_ref[0:HALO, :] = halo_ref[0]

    buf_ref[HALO:HALO + rows, :] = p_ref[0]

    def write(cs, val):
        o_ref[0, :, cs] = val.astype(BF16)

    _pool_groups(buf_ref, rows, wp_ref, sc_ref, lambda w: float(w), write)


def _pool_big(p, p_meta, w_pool, pool_scale, *, tm):
    b, t, c = p.shape
    hb = tm // HALO
    return pl.pallas_call(
        _pool_big_body,
        grid=(b, t // tm),
        in_specs=[
            pl.BlockSpec((1, tm, c), lambda bi, i: (bi, i, 0)),
            pl.BlockSpec((1, HALO, c), lambda bi, i: (bi, jnp.maximum(i * hb - 1, 0), 0)),
            pl.BlockSpec((HALO, c), lambda bi, i: (0, 0)),
            pl.BlockSpec(w_pool.shape, lambda bi, i: (0, 0, 0)),
            pl.BlockSpec((1, c), lambda bi, i: (0, 0)),
        ],
        out_specs=pl.BlockSpec((1, tm, c), lambda bi, i: (bi, i, 0)),
        out_shape=jax.ShapeDtypeStruct((b, t, c), BF16),
        scratch_shapes=[pltpu.VMEM((HALO + tm, c), F32)],
        compiler_params=_params("parallel", "arbitrary"),
        name="pool_big",
    )(p, p, p_meta, w_pool, pool_scale)


def _pool_small_body(p_ref, st_ref, wp_ref, sc_ref, o_ref, buf_ref, *, n_seq, t_dec):
    buf_ref[0:HALO, :] = jnp.zeros((HALO, buf_ref.shape[1]), F32)
    buf_ref[HALO:2 * HALO, :] = p_ref[0:N_META, :]
    pos1 = lax.broadcasted_iota(jnp.int32, (N_META, 1), 0).astype(F32) + 1.0

    def write_meta(cs, val):
        o_ref[0:N_META, cs] = val

    _pool_groups(buf_ref, N_META, wp_ref, sc_ref, lambda w: jnp.minimum(pos1, float(w)), write_meta)

    o_ref[N_META + n_seq * t_dec:, :] = jnp.zeros(
        (o_ref.shape[0] - N_META - n_seq * t_dec, o_ref.shape[1]), F32)
    for s in range(n_seq):
        r0 = N_META + s * t_dec
        buf_ref[0:HALO, :] = st_ref[s]
        buf_ref[HALO:HALO + 8, :] = p_ref[r0:r0 + 8, :]

        def write_s(cs, val, r0=r0):
            o_ref[r0:r0 + t_dec, cs] = val[0:t_dec]

        _pool_groups(buf_ref, 8, wp_ref, sc_ref, lambda w: float(w), write_s)


def _pool_small(p_small, state16, w_pool, pool_scale, *, n_seq, t_dec):
    m, c = p_small.shape
    return pl.pallas_call(
        functools.partial(_pool_small_body, n_seq=n_seq, t_dec=t_dec),
        out_shape=jax.ShapeDtypeStruct((m, c), F32),
        scratch_shapes=[pltpu.VMEM((2 * HALO, c), F32)],
        compiler_params=pltpu.CompilerParams(vmem_limit_bytes=VMEM_LIMIT),
        name="pool_small",
    )(p_small, state16, w_pool, pool_scale)


def _lambda(lq1_ref, lk1_ref, lq2_ref, lk2_ref, lam_init):
    a = jnp.sum(lq1_ref[...] * lk1_ref[...], axis=-1, keepdims=True)
    b = jnp.sum(lq2_ref[...] * lk2_ref[...], axis=-1, keepdims=True)
    return jnp.exp(a) - jnp.exp(b) + lam_init


def _split_halves(q, axis):
    idx = lax.broadcasted_iota(jnp.int32, q.shape, axis) % HEAD_DIM
    zero = jnp.zeros_like(q)
    return jnp.where(idx < QK_HALF, q, zero), jnp.where(idx >= QK_HALF, q, zero)


def _nt_dot(a, b):
    return lax.dot_general(a, b, (((1,), (1,)), ((), ())), preferred_element_type=F32)


def _attn_big_body(lq1_ref, lk1_ref, lq2_ref, lk2_ref, gain_ref, pos_ref, posm_ref,
                   qt_ref, k_ref, vt_ref, km_ref, vmt_ref, o_ref,
                   m_ref, acc_ref, p_ref, al_ref, *, lam_init):
    blk = ATTN_BLOCK
    n_blk = qt_ref.shape[1]
    lam = _lambda(lq1_ref, lk1_ref, lq2_ref, lk2_ref, lam_init)
    diag = (lax.broadcasted_iota(jnp.int32, (blk, blk), 0)
            <= lax.broadcasted_iota(jnp.int32, (blk, blk), 1))
    nq = Q_GROUP

    def q_group(g, carry):
        q_ext = []
        for qh in range(nq):
            q = qt_ref[0, nq * g + qh].astype(F32)
            rid = lax.broadcasted_iota(jnp.int32, q.shape, 0)
            ones = jnp.where(rid < 2, 1.0, 0.0).astype(BF16)
            q_ext.append(tuple(
                jnp.concatenate([jnp.where(keep, q, 0.0).astype(BF16), ones], axis=0)
                for keep in (rid < QK_HALF, rid >= QK_HALF)))
        m_ref[...] = jnp.full(m_ref.shape, NEG_INF, F32)
        acc_ref[...] = jnp.zeros(acc_ref.shape, F32)
        p_ref[...] = jnp.zeros(p_ref.shape, BF16)
        al_ref[...] = jnp.ones(al_ref.shape, F32)

        def scores(slot, kb, kpos, targets):
            n_keys = kb.shape[0]
            sub = min(KEY_SUB, n_keys)
            for ki in range(n_keys // sub):
                rs = slice(ki * sub, (ki + 1) * sub)
                k_ext = jnp.concatenate([kb[rs], kpos[rs]], axis=1)
                for qh, mask in targets:
                    for mi in range(2):
                        st = 2 * qh + mi
                        s = jnp.dot(k_ext, q_ext[qh][mi], preferred_element_type=F32)
                        if mask is not None:
                            s = jnp.where(mask[rs], s, NEG_INF)
                        m_prev = m_ref[st]
                        m_new = jnp.maximum(m_prev, jnp.max(s, axis=0, keepdims=True))
                        al_ref[slot, ki, st] = jnp.exp(m_prev - m_new)
                        p_ref[slot, st, rs, :] = jnp.exp(s - m_new).astype(BF16)
                        m_ref[st] = m_new

        def values(slot, vt, first_q):
            n_keys = vt.shape[1]
            sub = min(KEY_SUB, n_keys)
            for ki in range(n_keys // sub):
                rs = slice(ki * sub, (ki + 1) * sub)
                for st in range(2 * first_q, 2 * nq):
                    acc_ref[st] = al_ref[slot, ki, st] * acc_ref[st] + jnp.dot(
                        vt[:, rs], p_ref[slot, st, rs, :], preferred_element_type=F32)

        def keys(kj):
            k0 = pl.multiple_of(kj * blk, blk)
            return k_ref[pl.ds(k0, blk), :], pos_ref[pl.ds(k0, blk), :]

        everyone = [(qh, None) for qh in range(nq)]
        first_key = nq * g

        def block_pair(t, c2):
            values(1, vt_ref[0, jnp.maximum(2 * t - 1, 0)], 0)
            scores(0, *keys(2 * t), everyone)
            values(0, vt_ref[0, 2 * t], 0)
            scores(1, *keys(2 * t + 1), everyone)
            return c2

        lax.fori_loop(0, first_key // 2, block_pair, 0)
        values(1, vt_ref[0, jnp.maximum(first_key - 1, 0)], 0)
        for j in range(nq):
            scores(j % 2, *keys(first_key + j), [(j, diag)] + everyone[j + 1:])
            values(j % 2, vt_ref[0, first_key + j], j)
        scores(nq % 2, km_ref[...], posm_ref[...], everyone)
        values(nq % 2, vmt_ref[0], 0)

        for qh in range(nq):
            a1, a2 = acc_ref[2 * qh], acc_ref[2 * qh + 1]
            o = (a1[0:HEAD_DIM] / a1[HEAD_DIM:HEAD_DIM + 1]
                 - lam * (a2[0:HEAD_DIM] / a2[HEAD_DIM:HEAD_DIM + 1]))
            o = o * lax.rsqrt(jnp.mean(o * o, axis=0, keepdims=True) + EPS) * gain_ref[...]
            r0 = pl.multiple_of((first_key + qh) * blk, blk)
            o_ref[pl.ds(r0, blk), :] = (o * (1.0 - lam_init)).T.astype(BF16)
        return carry

    assert n_blk % nq == 0 and nq % 2 == 0
    lax.fori_loop(0, n_blk // nq, q_group, 0)


def _position_blocks(slopes, first, n):
    pos = jnp.arange(first, first + n, dtype=jnp.int32)
    coarse = slopes[:, None] * ((pos // QK_HALF) * QK_HALF).astype(F32)[None, :]
    fine = slopes[:, None] * (pos % QK_HALF).astype(F32)[None, :]
    blocks = jnp.zeros((slopes.shape[0], n, HEAD_DIM), F32)
    return blocks.at[:, :, 0].set(coarse).at[:, :, 1].set(fine).astype(BF16)


def _attn_big(q_t, kb, v_t, k_meta, vm_t, slopes, lams, gain_col, *, lam_init, bsz):
    n_heads, n_chunks, hd, blk = q_t.shape
    m, c = kb.shape
    seq = m // bsz
    n_blk = seq // blk
    vec = lambda n: pl.BlockSpec((1, n), lambda bi, h: (0, 0))
    pos = _position_blocks(slopes, 0, seq)
    pos_meta = _position_blocks(slopes, -N_META, N_META)
    return pl.pallas_call(
        functools.partial(_attn_big_body, lam_init=lam_init),
        grid=(bsz, n_heads),
        in_specs=[
            vec(QK_HALF), vec(QK_HALF), vec(QK_HALF), vec(QK_HALF),
            pl.BlockSpec((hd, 1), lambda bi, h: (0, 0)),
            pl.BlockSpec((None, seq, hd), lambda bi, h: (h, 0, 0)),
            pl.BlockSpec((None, N_META, hd), lambda bi, h: (h, 0, 0)),
            pl.BlockSpec((1, n_blk, hd, blk), lambda bi, h: (h, bi, 0, 0)),
            pl.BlockSpec((seq, hd), lambda bi, h: (bi, h)),
            pl.BlockSpec((1, n_blk, V_ROWS, blk), lambda bi, h: (h, bi, 0, 0)),
            pl.BlockSpec((N_META, hd), lambda bi, h: (0, h)),
            pl.BlockSpec((1, V_ROWS, N_META), lambda bi, h: (h, 0, 0)),
        ],
        out_specs=pl.BlockSpec((seq, hd), lambda bi, h: (bi, h)),
        out_shape=jax.ShapeDtypeStruct((m, c), BF16),
        scratch_shapes=[
            pltpu.VMEM((2 * Q_GROUP, 1, blk), F32),
            pltpu.VMEM((2 * Q_GROUP, V_ROWS, blk), F32),
            pltpu.VMEM((2, 2 * Q_GROUP, blk, blk), BF16),
            pltpu.VMEM((2, blk // KEY_SUB, 2 * Q_GROUP, 1, blk), F32),
        ],
        compiler_params=_params("parallel", "arbitrary"),
        name="attn_big",
    )(*lams, gain_col, pos, pos_meta, q_t, kb, v_t, k_meta, vm_t)


def _attn_meta_body(slope_ref, lq1_ref, lk1_ref, lq2_ref, lk2_ref, gain_ref,
                    q_ref, k_ref, v_ref, o_ref, *, lam_init):
    h = pl.program_id(0)
    slope = slope_ref[h]
    lam = _lambda(lq1_ref, lk1_ref, lq2_ref, lk2_ref, lam_init)
    qs = _split_halves(q_ref[...], 1)
    kb = k_ref[...]
    shape = (N_META, kb.shape[0])
    qpos = lax.broadcasted_iota(jnp.int32, shape, 0)
    kpos = lax.broadcasted_iota(jnp.int32, shape, 1)
    dist = (qpos - kpos).astype(F32)
    mask = (qpos >= kpos) & (kpos < N_META)
    outs = []
    for mi in range(2):
        s = jnp.where(mask, _nt_dot(qs[mi], kb) - slope * dist, NEG_INF)
        p = jnp.exp(s - jnp.max(s, axis=-1, keepdims=True))
        p = p / jnp.sum(p, axis=-1, keepdims=True)
        outs.append(p)
    a = (outs[0] - lam * outs[1]).astype(BF16)
    o = jnp.dot(a, v_ref[...], preferred_element_type=F32)
    o_ref[...] = (_rms(o, gain_ref[...]) * (1.0 - lam_init)).astype(BF16)


def _attn_meta(q_s, kb_s, vb_s, slopes, lams, gain, *, lam_init):
    m, c = kb_s.shape
    vec = lambda n: pl.BlockSpec((1, n), lambda h: (0, 0))
    return pl.pallas_call(
        functools.partial(_attn_meta_body, lam_init=lam_init),
        grid=(N_HEADS,),
        in_specs=[
            pl.BlockSpec(memory_space=pltpu.SMEM),
            vec(QK_HALF), vec(QK_HALF), vec(QK_HALF), vec(QK_HALF), vec(HEAD_DIM),
            pl.BlockSpec((N_META, HEAD_DIM), lambda h: (0, h)),
            pl.BlockSpec((m, HEAD_DIM), lambda h: (0, h)),
            pl.BlockSpec((m, HEAD_DIM), lambda h: (0, h)),
        ],
        out_specs=pl.BlockSpec((N_META, HEAD_DIM), lambda h: (0, h)),
        out_shape=jax.ShapeDtypeStruct((N_META, c), BF16),
        compiler_params=_params("parallel"),
        name="attn_meta",
    )(slopes, *lams, gain, q_s, kb_s, vb_s)


def _attn_sample_body(pt_ref, lq1_ref, lk1_ref, lq2_ref, lk2_ref, gain_ref,
                      q_ref, kn_ref, vn_ref, *rest, lam_init, t_dec, past_len):
    n_pg = PAGES_PER_STEP
    k_refs, v_refs = rest[:n_pg], rest[n_pg:2 * n_pg]
    o_ref, qm_ref, m_ref, l_ref, acc_ref, bias_ref, p_ref, al_ref = rest[2 * n_pg:]
    pg = pl.program_id(1)
    n_steps = pl.num_programs(1) - 1
    page = k_refs[0].shape[0]
    grp = 2 * t_dec
    rows = grp * N_HEADS
    cols = page * N_HEADS
    row1 = lax.broadcasted_iota(jnp.int32, (rows, 1), 0)
    r_q = row1 % t_dec
    slope = jnp.exp2(-((row1 // grp) + 1).astype(F32))

    def update(kb, vb, bias, shift):
        s = _nt_dot(qm_ref[...], kb) + bias
        m_prev = m_ref[...]
        m_new = jnp.maximum(m_prev, jnp.max(s, axis=-1, keepdims=True) + shift)
        alpha = jnp.exp(m_prev - m_new)
        p = jnp.exp(s - (m_new - shift))
        l_ref[...] = alpha * l_ref[...] + jnp.sum(p, axis=-1, keepdims=True)
        acc_ref[...] = alpha * acc_ref[...] + jnp.dot(p.astype(BF16), vb, preferred_element_type=F32)
        m_ref[...] = m_new

    @pl.when(pg == 0)
    def _():
        q = q_ref[0]
        r2 = lax.broadcasted_iota(jnp.int32, q.shape, 0)
        c2 = lax.broadcasted_iota(jnp.int32, q.shape, 1)
        qm_ref[...] = jnp.where((c2 // QK_HALF) == (r2 % grp) // t_dec, q, jnp.zeros_like(q))
        m_ref[...] = jnp.full(m_ref.shape, NEG_INF, F32)
        l_ref[...] = jnp.zeros(l_ref.shape, F32)
        acc_ref[...] = jnp.zeros(acc_ref.shape, F32)
        p_ref[...] = jnp.zeros(p_ref.shape, BF16)
        al_ref[...] = jnp.ones(al_ref.shape, F32)
        rw = lax.broadcasted_iota(jnp.int32, (rows, cols), 0)
        cl = lax.broadcasted_iota(jnp.int32, (rows, cols), 1)
        bias_ref[...] = jnp.where((cl % N_HEADS) == (rw // grp),
                                  slope * (cl // N_HEADS).astype(F32), NEG_INF)
        rw = lax.broadcasted_iota(jnp.int32, (rows, t_dec * N_HEADS), 0)
        cl = lax.broadcasted_iota(jnp.int32, (rows, t_dec * N_HEADS), 1)
        r_k = cl // N_HEADS
        ok = ((cl % N_HEADS) == (rw // grp)) & (r_k <= rw % t_dec)
        bias_new = jnp.where(ok, -slope * (rw % t_dec - r_k).astype(F32), NEG_INF)
        update(kn_ref[0], vn_ref[0], bias_new, 0.0)

    n_grp = n_pg // SAMPLE_GROUP

    def values():
        acc = acc_ref[...]
        for g in range(n_grp):
            acc = al_ref[g] * acc
            for gi in range(g * SAMPLE_GROUP, (g + 1) * SAMPLE_GROUP):
                vb = v_refs[gi][...].reshape(cols, HEAD_DIM).astype(BF16)
                acc = acc + jnp.dot(p_ref[:, gi * cols:(gi + 1) * cols], vb, preferred_element_type=F32)
        acc_ref[...] = acc

    @pl.when(pg < n_steps)
    def _():
        values()
        qm = qm_ref[...]
        m_run, l_run = m_ref[...], l_ref[...]
        for g in range(n_grp):
            s_pg, top = [], None
            for gi in range(g * SAMPLE_GROUP, (g + 1) * SAMPLE_GROUP):
                start = (pg * n_pg + gi) * page
                shift = slope * (start - past_len - r_q).astype(F32)
                kb = k_refs[gi][...].reshape(cols, HEAD_DIM).astype(BF16)
                s = _nt_dot(qm, kb) + bias_ref[...]
                s_pg.append((gi, s, shift))
                top_g = jnp.max(s, axis=-1, keepdims=True) + shift
                top = top_g if top is None else jnp.maximum(top, top_g)
            m_new = jnp.maximum(m_run, top)
            alpha = jnp.exp(m_run - m_new)
            l_run = alpha * l_run
            for gi, s, shift in s_pg:
                p = jnp.exp(s + (shift - m_new))
                l_run = l_run + jnp.sum(p, axis=-1, keepdims=True)
                p_ref[:, gi * cols:(gi + 1) * cols] = p.astype(BF16)
            al_ref[g] = alpha
            m_run = m_new
        m_ref[...] = m_run
        l_ref[...] = l_run

    @pl.when(pg == n_steps)
    def _():
        values()
        lam = _lambda(lq1_ref, lk1_ref, lq2_ref, lk2_ref, lam_init)
        for hd in range(N_HEADS):
            rs = slice(hd * grp, (hd + 1) * grp)
            n = acc_ref[rs, :] / l_ref[rs, :]
            o = n - lam * pltpu.roll(n, t_dec, 0)
            o_ref[0, rs, :] = _rms(o, gain_ref[...]) * (1.0 - lam_init)


def _attn_sample(page_table, q_rows, k_new, v_new, cache_k, cache_v, lams, gain, *, lam_init, t_dec):
    n_seq, n_pages = page_table.shape
    _, _, page, n_heads, hd = cache_k.shape
    n_pg = PAGES_PER_STEP
    assert n_pages % n_pg == 0 and n_heads == N_HEADS and hd == HEAD_DIM and 2 * t_dec == 8
    rows = 2 * t_dec * N_HEADS
    vec = lambda n: pl.BlockSpec((1, n), lambda s, p, pt: (0, 0))

    n_steps = n_pages // n_pg

    def page_spec(gi, lag):
        def index(s, p, pt):
            grp = jnp.maximum(p - 1, 0) if lag else jnp.minimum(p, n_steps - 1)
            return (0, pt[s, grp * n_pg + gi], 0, 0, 0)
        return pl.BlockSpec((None, None, page, n_heads, hd), index)

    grid_spec = pltpu.PrefetchScalarGridSpec(
        num_scalar_prefetch=1,
        grid=(n_seq, n_steps + 1),
        in_specs=[
            vec(QK_HALF), vec(QK_HALF), vec(QK_HALF), vec(QK_HALF), vec(HEAD_DIM),
            pl.BlockSpec((1, rows, hd), lambda s, p, pt: (s, 0, 0)),
            pl.BlockSpec((1, t_dec * n_heads, hd), lambda s, p, pt: (s, 0, 0)),
            pl.BlockSpec((1, t_dec * n_heads, hd), lambda s, p, pt: (s, 0, 0)),
        ] + [page_spec(gi, False) for gi in range(n_pg)] + [page_spec(gi, True) for gi in range(n_pg)],
        out_specs=pl.BlockSpec((1, rows, hd), lambda s, p, pt: (s, 0, 0)),
        scratch_shapes=[
            pltpu.VMEM((rows, hd), BF16),
            pltpu.VMEM((rows, 1), F32),
            pltpu.VMEM((rows, 1), F32),
            pltpu.VMEM((rows, hd), F32),
            pltpu.VMEM((rows, page * n_heads), F32),
            pltpu.VMEM((rows, n_pg * page * n_heads), BF16),
            pltpu.VMEM((n_pg // SAMPLE_GROUP, rows, 1), F32),
        ],
    )
    return pl.pallas_call(
        functools.partial(_attn_sample_body, lam_init=lam_init, t_dec=t_dec, past_len=n_pages * page),
        grid_spec=grid_spec,
        out_shape=jax.ShapeDtypeStruct((n_seq, rows, hd), F32),
        compiler_params=_params("parallel", "arbitrary"),
        name="attn_sample",
    )(page_table, *lams, gain, q_rows, k_new, v_new,
      *([cache_k] * n_pg), *([cache_v] * n_pg))


def _outproj_body(x_ref, a_ref, b_ref, w_ref, o_ref):
    half = a_ref.shape[1]
    o_ref[...] = (x_ref[...]
                  + jnp.dot(a_ref[...], w_ref[0:half, :], preferred_element_type=F32)
                  + jnp.dot(b_ref[...], w_ref[half:, :], preferred_element_type=F32))


def _outproj(x, pool_out, attn_out, w_out, *, tm):
    m, d = x.shape
    c = pool_out.shape[1]
    return pl.pallas_call(
        _outproj_body,
        grid=(m // tm,),
        in_specs=[
            pl.BlockSpec((tm, d), lambda i: (i, 0)),
            pl.BlockSpec((tm, c), lambda i: (i, 0)),
            pl.BlockSpec((tm, c), lambda i: (i, 0)),
            pl.BlockSpec(w_out.shape, lambda i: (0, 0)),
        ],
        out_specs=pl.BlockSpec((tm, d), lambda i: (i, 0)),
        out_shape=jax.ShapeDtypeStruct((m, d), F32),
        compiler_params=_params("parallel"),
        name="outproj",
    )(x, pool_out, attn_out, w_out)


def kernel(x_prompt, x_sample, cache_k, cache_v, state_pool, page_table, meta_tokens,
           norm_ffn1, w_gate1, w_up1, w_down1, norm_mix, w_in, w_pool, pool_scale,
           lambda_q1, lambda_k1, lambda_q2, lambda_k2, subln_gain, w_out,
           norm_ffn2, w_gate2, w_up2, w_down2, norm_final):
    depth = w_in.shape[0]
    assert depth == 1
    bsz, seq, d = x_prompt.shape
    n_seq, t_dec, _ = x_sample.shape
    c = N_HEADS * HEAD_DIM
    n_small = N_META + n_seq * t_dec
    assert n_small <= SMALL_ROWS and 2 * t_dec == 8
    lam_init = 0.8 - 0.6 * math.exp(-0.3 * 0)

    row = lambda a: a.reshape(1, -1)
    w_in_b, w_out_b, w_pool_b = w_in[0].astype(BF16), w_out[0].astype(BF16), w_pool[0].astype(BF16)
    g1, gm, g2, gf = row(norm_ffn1[0]), row(norm_mix[0]), row(norm_ffn2[0]), row(norm_final)
    pscale, gain = row(pool_scale[0]), row(subln_gain[0])
    lams = (row(lambda_q1[0]), row(lambda_k1[0]), row(lambda_q2[0]), row(lambda_k2[0]))
    slopes = jnp.exp2(-jnp.arange(1, N_HEADS + 1, dtype=F32))

    x_big = x_prompt.reshape(bsz * seq, d)
    x_small = jnp.concatenate([meta_tokens, x_sample.reshape(n_seq * t_dec, d),
                               jnp.zeros((SMALL_ROWS - n_small, d), F32)], axis=0)

    xb1, xs1 = _ffn(x_big, x_small, g1, w_gate1[0], w_up1[0], w_down1[0], gf, final_norm=False)

    ps, ks, vs, kbs, qs, vbs = _proj(xs1, gm, w_in_b, tm=SMALL_ROWS, transposed=False)
    state16 = jnp.pad(state_pool[0], ((0, 0), (HALO - POOL_BUF, 0), (0, 0)))
    pool_s = _pool_small(ps, state16, w_pool_b, pscale, n_seq=n_seq, t_dec=t_dec).astype(BF16)
    o_meta = _attn_meta(qs, kbs, vbs, slopes, lams, gain, lam_init=lam_init)
    q_dec = qs[N_META:n_small].reshape(n_seq, t_dec, N_HEADS, HEAD_DIM).transpose(0, 2, 1, 3)
    q_rows = jnp.broadcast_to(q_dec[:, :, None], (n_seq, N_HEADS, 2, t_dec, HEAD_DIM))
    q_rows = q_rows.reshape(n_seq, 2 * t_dec * N_HEADS, HEAD_DIM)
    k_new = kbs[N_META:n_small].reshape(n_seq, t_dec * N_HEADS, HEAD_DIM)
    v_new = vbs[N_META:n_small].reshape(n_seq, t_dec * N_HEADS, HEAD_DIM)
    o_dec = _attn_sample(page_table, q_rows, k_new, v_new, cache_k, cache_v, lams, gain,
                         lam_init=lam_init, t_dec=t_dec)
    o_dec = o_dec.reshape(n_seq, N_HEADS, 2, t_dec, HEAD_DIM)[:, :, 0].transpose(0, 2, 1, 3)
    o_small = jnp.concatenate([o_meta, o_dec.reshape(n_seq * t_dec, c).astype(BF16),
                               jnp.zeros((SMALL_ROWS - n_small, c), BF16)], axis=0)
    xs2 = _outproj(xs1, pool_s, o_small, w_out_b, tm=SMALL_ROWS)

    pb, kb, vb, kbb, q_t, v_t = _proj(xb1, gm, w_in_b, tm=512, transposed=True, seq=seq)
    pool_b = _pool_big(pb.reshape(bsz, seq, c), ps[:N_META], w_pool_b, pscale, tm=512)
    vm_t = vbs[:N_META].reshape(N_META, N_HEADS, HEAD_DIM).transpose(1, 2, 0)
    ones_rows = jnp.zeros((N_HEADS, V_ROWS - HEAD_DIM, N_META), BF16).at[:, 0].set(1.0)
    vm_t = jnp.concatenate([vm_t, ones_rows], axis=1)
    o_big = _attn_big(q_t, kbb, v_t, kbs[:N_META], vm_t, slopes, lams, gain.reshape(HEAD_DIM, 1),
                      lam_init=lam_init, bsz=bsz)
    xb2 = _outproj(xb1, pool_b.reshape(bsz * seq, c), o_big, w_out_b, tm=512)
    yb, ys = _ffn(xb2, xs2, g2, w_gate2[0], w_up2[0], w_down2[0], gf, final_norm=True)

    k_rows, v_rows = _fill_meta(kb, vb, ks[:N_META], vs[:N_META], bsz=bsz)
    prompt_rows = lambda a: a.reshape(1, bsz, N_META + seq, N_HEADS, HEAD_DIM)
    dec = lambda a: a[N_META:n_small].reshape(1, n_seq, t_dec, N_HEADS, HEAD_DIM)
    y_prompt = yb.reshape(bsz, seq, d)
    y_sample = ys[N_META:n_small].reshape(n_seq, t_dec, d)
    pool_prompt = pb.reshape(bsz, seq, c)[:, seq - POOL_BUF:][None]
    pool_sample = jnp.concatenate([state_pool[0][:, t_dec:], ps[N_META:n_small].reshape(n_seq, t_dec, c)],
                                  axis=1)[None]
    return (y_prompt, y_sample, prompt_rows(k_rows), prompt_rows(v_rows), pool_prompt,
            dec(ks), dec(vs), pool_sample)
```

```python
import functools
import math

import jax
import jax.numpy as jnp
from jax import lax
from jax.experimental import pallas as pl
from jax.experimental.pallas import tpu as pltpu

F32 = jnp.float32
BF16 = jnp.bfloat16

N_META = 16
POOL_WINDOWS = (2, 4, 8, 16)
POOL_BUF = max(POOL_WINDOWS) - 1
HALO = 16
N_HEADS = 8
HEAD_DIM = 128
QK_HALF = HEAD_DIM // 2
EPS = 1e-6
NEG_INF = -1e30
SMALL_ROWS = 128
ATTN_BLOCK = 256
V_ROWS = HEAD_DIM + 16
Q_GROUP = 4
KEY_SUB = 256
FFN_ROWS = 1024
FFN_CHUNK = 128
FFN_COLS = 256
SAMPLE_GROUP = 2
PAGES_PER_STEP = 16

VMEM_LIMIT = 60 * 1024 * 1024


def _params(*sem):
    return pltpu.CompilerParams(dimension_semantics=sem, vmem_limit_bytes=VMEM_LIMIT)


def _rms(x, g):
    return x * lax.rsqrt(jnp.mean(x * x, axis=-1, keepdims=True) + EPS) * g


def _ffn_body(xb_ref, xs_ref, g_ref, wg_ref, wu_ref, wd_ref, gf_ref, ob_ref, os_ref,
              hb_ref, hs_ref, wgb_ref, wub_ref, wdb_ref, *, final_norm):
    i, s = pl.program_id(0), pl.program_id(1)
    n_blk = pl.num_programs(1) - 1

    def round_weights():
        slot = s % 2
        wgb_ref[slot] = wg_ref[...].astype(BF16)
        wub_ref[slot] = wu_ref[...].astype(BF16)
        wdb_ref[slot] = wd_ref[...].astype(BF16)

    def row_chunks(ref):
        n = ref.shape[0]
        step = min(n, FFN_CHUNK)
        return [slice(r, r + step) for r in range(0, n, step)]

    def start(x_ref, o_ref, h_ref):
        for rs in row_chunks(x_ref):
            h_ref[rs, :] = _rms(x_ref[rs, :], g_ref[...]).astype(BF16)
            o_ref[rs, :] = jnp.zeros((rs.stop - rs.start, o_ref.shape[1]), F32)

    def accumulate(o_ref, h_ref):
        slot = (s + 1) % 2
        h = h_ref[...]
        g = jnp.dot(h, wgb_ref[slot], preferred_element_type=F32)
        u = jnp.dot(h, wub_ref[slot], preferred_element_type=F32)
        a = (g * jax.nn.sigmoid(g) * u).astype(BF16)
        o_ref[...] += jnp.dot(a, wdb_ref[slot], preferred_element_type=F32)

    def finish(x_ref, o_ref):
        for rs in row_chunks(x_ref):
            y = x_ref[rs, :] + 0.5 * o_ref[rs, :]
            if final_norm:
                y = _rms(y, gf_ref[...])
            o_ref[rs, :] = y

    last_tile = i == pl.num_programs(0) - 1

    @pl.when(s == 0)
    def _():
        round_weights()
        start(xb_ref, ob_ref, hb_ref)

        @pl.when(last_tile)
        def _():
            start(xs_ref, os_ref, hs_ref)

    @pl.when(s > 0)
    def _():
        accumulate(ob_ref, hb_ref)
        round_weights()

        @pl.when(last_tile)
        def _():
            accumulate(os_ref, hs_ref)

    @pl.when(s == n_blk)
    def _():
        finish(xb_ref, ob_ref)

        @pl.when(last_tile)
        def _():
            finish(xs_ref, os_ref)


def _ffn(x_big, x_small, g, wg, wu, wd, gf, *, final_norm):
    m, d = x_big.shape
    ms = x_small.shape[0]
    f = wg.shape[1]
    tm, tf = FFN_ROWS, FFN_COLS
    assert m % tm == 0 and f % tf == 0
    n_blk = f // tf
    const = lambda i, s: (0, 0)
    blk = lambda s: jnp.minimum(s, n_blk - 1)
    return pl.pallas_call(
        functools.partial(_ffn_body, final_norm=final_norm),
        grid=(m // tm, n_blk + 1),
        in_specs=[
            pl.BlockSpec((tm, d), lambda i, s: (i, 0), pipeline_mode=pl.Buffered(1)),
            pl.BlockSpec((ms, d), const),
            pl.BlockSpec((1, d), const),
            pl.BlockSpec((d, tf), lambda i, s: (0, blk(s))),
            pl.BlockSpec((d, tf), lambda i, s: (0, blk(s))),
            pl.BlockSpec((tf, d), lambda i, s: (blk(s), 0)),
            pl.BlockSpec((1, d), const),
        ],
        out_specs=[pl.BlockSpec((tm, d), lambda i, s: (i, 0)), pl.BlockSpec((ms, d), const)],
        out_shape=[jax.ShapeDtypeStruct((m, d), F32), jax.ShapeDtypeStruct((ms, d), F32)],
        scratch_shapes=[
            pltpu.VMEM((tm, d), BF16), pltpu.VMEM((ms, d), BF16),
            pltpu.VMEM((2, d, tf), BF16), pltpu.VMEM((2, d, tf), BF16), pltpu.VMEM((2, tf, d), BF16),
        ],
        compiler_params=_params("arbitrary", "arbitrary"),
        name="ffn",
    )(x_big, x_small, g, wg, wu, wd, gf)


def _proj_body(x_ref, g_ref, w_ref, p_ref, k_ref, v_ref, kb_ref, a_ref, b_ref, *, transposed):
    h = _rms(x_ref[...], g_ref[...]).astype(BF16)
    w = p_ref.shape[1]

    def block(j):
        return jnp.dot(h, w_ref[:, j * w:(j + 1) * w], preferred_element_type=F32)

    def put_t(dst, val, extra_ones):
        tc = dst.shape[3]
        for hd in range(N_HEADS):
            for ci in range(dst.shape[1]):
                blk = val[ci * tc:(ci + 1) * tc, hd * HEAD_DIM:(hd + 1) * HEAD_DIM]
                dst[hd, ci, 0:HEAD_DIM, :] = blk.T.astype(BF16)
                if extra_ones:
                    rid = lax.broadcasted_iota(jnp.int32, (dst.shape[2] - HEAD_DIM, tc), 0)
                    dst[hd, ci, HEAD_DIM:, :] = jnp.where(rid == 0, 1.0, 0.0).astype(BF16)

    p_ref[...] = block(0)
    q = block(1) * (QK_HALF ** -0.5)
    if transposed:
        put_t(a_ref, q, False)
    else:
        a_ref[...] = q.astype(BF16)
    k = block(2)
    k_ref[...] = k
    kb_ref[...] = k.astype(BF16)
    v = block(3)
    v_ref[...] = v
    if transposed:
        put_t(b_ref, v, True)
    else:
        b_ref[...] = v.astype(BF16)


def _proj(x, g, w_in, *, tm, transposed, seq=None):
    m, d = x.shape
    w = w_in.shape[1] // 4
    row = lambda i: (i, 0)
    rm_spec = pl.BlockSpec((tm, w), row)
    rm_shape = jax.ShapeDtypeStruct((m, w), BF16)
    f32_shape = jax.ShapeDtypeStruct((m, w), F32)
    kv_spec, kv_shape = rm_spec, f32_shape
    if transposed:
        tc = ATTN_BLOCK
        t_spec = lambda n: pl.BlockSpec((N_HEADS, tm // tc, n, tc), lambda i: (0, i, 0, 0))
        t_shape = lambda n: jax.ShapeDtypeStruct((N_HEADS, m // tc, n, tc), BF16)
        extra_specs = [rm_spec, t_spec(HEAD_DIM), t_spec(V_ROWS)]
        extra_shapes = [rm_shape, t_shape(HEAD_DIM), t_shape(V_ROWS)]
        per_seq = seq // tm
        kv_spec = pl.BlockSpec(
            (pl.Element(tm), pl.Element(w)),
            lambda i: (pl.multiple_of(
                (i // per_seq) * (N_META + seq) + N_META + (i % per_seq) * tm, N_META), 0))
        kv_shape = jax.ShapeDtypeStruct((m // seq * (N_META + seq), w), F32)
    else:
        extra_specs = [rm_spec] * 3
        extra_shapes = [rm_shape] * 3
    return pl.pallas_call(
        functools.partial(_proj_body, transposed=transposed),
        grid=(m // tm,),
        in_specs=[
            pl.BlockSpec((tm, d), row),
            pl.BlockSpec((1, d), lambda i: (0, 0)),
            pl.BlockSpec(w_in.shape, lambda i: (0, 0), pipeline_mode=pl.Buffered(1)),
        ],
        out_specs=[rm_spec, kv_spec, kv_spec] + extra_specs,
        out_shape=[f32_shape, kv_shape, kv_shape] + extra_shapes,
        compiler_params=_params("parallel"),
        name="proj",
    )(x, g, w_in)


def _fill_meta_body(km_ref, vm_ref, k_in_ref, v_in_ref, k_ref, v_ref):
    del k_in_ref, v_in_ref
    k_ref[...] = km_ref[...]
    v_ref[...] = vm_ref[...]


def _fill_meta(k_rows, v_rows, k_meta, v_meta, *, bsz):
    m, w = k_rows.shape
    per_seq = m // bsz
    assert per_seq % N_META == 0
    meta_spec = pl.BlockSpec((N_META, w), lambda b: (0, 0))
    out_spec = pl.BlockSpec((N_META, w), lambda b: (b * (per_seq // N_META), 0))
    shape = jax.ShapeDtypeStruct((m, w), k_rows.dtype)
    return pl.pallas_call(
        _fill_meta_body,
        grid=(bsz,),
        in_specs=[meta_spec, meta_spec, pl.BlockSpec(memory_space=pl.ANY), pl.BlockSpec(memory_space=pl.ANY)],
        out_specs=[out_spec, out_spec],
        out_shape=[shape, shape],
        input_output_aliases={2: 0, 3: 1},
        compiler_params=_params("arbitrary"),
        name="fill_meta",
    )(k_meta, v_meta, k_rows, v_rows)


def _pool_groups(buf_ref, rows, wp_ref, sc_ref, count_fn, write):
    gd = wp_ref.shape[1]
    for gi, w in enumerate(POOL_WINDOWS):
        cs = slice(gi * gd, (gi + 1) * gd)
        cur = buf_ref[HALO:HALO + rows, cs]
        acc = cur
        for dlt in range(1, w):
            acc = acc + buf_ref[HALO - dlt:HALO - dlt + rows, cs]
        feat = acc / count_fn(w) - cur
        out = jnp.dot(feat.astype(BF16), wp_ref[gi], preferred_element_type=F32) * sc_ref[:, cs]
        write(cs, out)


def _pool_big_body(p_ref, halo_ref, pm_ref, wp_ref, sc_ref, o_ref, buf_ref):
    i = pl.program_id(1)
    rows = p_ref.shape[1]

    @pl.when(i == 0)
    def _():
        buf_ref[0:HALO, :] = pm_ref[...]

    @pl.when(i > 0)
    def _():
        buf_ref[0:HALO, :] = halo_ref[0]

    buf_ref[HALO:HALO + rows, :] = p_ref[0]

    def write(cs, val):
        o_ref[0, :, cs] = val.astype(BF16)

    _pool_groups(buf_ref, rows, wp_ref, sc_ref, lambda w: float(w), write)


def _pool_big(p, p_meta, w_pool, pool_scale, *, tm):
    b, t, c = p.shape
    hb = tm // HALO
    return pl.pallas_call(
        _pool_big_body,
        grid=(b, t // tm),
        in_specs=[
            pl.BlockSpec((1, tm, c), lambda bi, i: (bi, i, 0)),
            pl.BlockSpec((1, HALO, c), lambda bi, i: (bi, jnp.maximum(i * hb - 1, 0), 0)),
            pl.BlockSpec((HALO, c), lambda bi, i: (0, 0)),
            pl.BlockSpec(w_pool.shape, lambda bi, i: (0, 0, 0)),
            pl.BlockSpec((1, c), lambda bi, i: (0, 0)),
        ],
        out_specs=pl.BlockSpec((1, tm, c), lambda bi, i: (bi, i, 0)),
        out_shape=jax.ShapeDtypeStruct((b, t, c), BF16),
        scratch_shapes=[pltpu.VMEM((HALO + tm, c), F32)],
        compiler_params=_params("parallel", "arbitrary"),
        name="pool_big",
    )(p, p, p_meta, w_pool, pool_scale)


def _pool_small_body(p_ref, st_ref, wp_ref, sc_ref, o_ref, buf_ref, *, n_seq, t_dec):
    buf_ref[0:HALO, :] = jnp.zeros((HALO, buf_ref.shape[1]), F32)
    buf_ref[HALO:2 * HALO, :] = p_ref[0:N_META, :]
    pos1 = lax.broadcasted_iota(jnp.int32, (N_META, 1), 0).astype(F32) + 1.0

    def write_meta(cs, val):
        o_ref[0:N_META, cs] = val

    _pool_groups(buf_ref, N_META, wp_ref, sc_ref, lambda w: jnp.minimum(pos1, float(w)), write_meta)

    o_ref[N_META + n_seq * t_dec:, :] = jnp.zeros(
        (o_ref.shape[0] - N_META - n_seq * t_dec, o_ref.shape[1]), F32)
    for s in range(n_seq):
        r0 = N_META + s * t_dec
        buf_ref[0:HALO, :] = st_ref[s]
        buf_ref[HALO:HALO + 8, :] = p_ref[r0:r0 + 8, :]

        def write_s(cs, val, r0=r0):
            o_ref[r0:r0 + t_dec, cs] = val[0:t_dec]

        _pool_groups(buf_ref, 8, wp_ref, sc_ref, lambda w: float(w), write_s)


def _pool_small(p_small, state16, w_pool, pool_scale, *, n_seq, t_dec):
    m, c = p_small.shape
    return pl.pallas_call(
        functools.partial(_pool_small_body, n_seq=n_seq, t_dec=t_dec),
        out_shape=jax.ShapeDtypeStruct((m, c), F32),
        scratch_shapes=[pltpu.VMEM((2 * HALO, c), F32)],
        compiler_params=pltpu.CompilerParams(vmem_limit_bytes=VMEM_LIMIT),
        name="pool_small",
    )(p_small, state16, w_pool, pool_scale)


def _lambda(lq1_ref, lk1_ref, lq2_ref, lk2_ref, lam_init):
    a = jnp.sum(lq1_ref[...] * lk1_ref[...], axis=-1, keepdims=True)
    b = jnp.sum(lq2_ref[...] * lk2_ref[...], axis=-1, keepdims=True)
    return jnp.exp(a) - jnp.exp(b) + lam_init


def _split_halves(q, axis):
    idx = lax.broadcasted_iota(jnp.int32, q.shape, axis) % HEAD_DIM
    zero = jnp.zeros_like(q)
    return jnp.where(idx < QK_HALF, q, zero), jnp.where(idx >= QK_HALF, q, zero)


def _nt_dot(a, b):
    return lax.dot_general(a, b, (((1,), (1,)), ((), ())), preferred_element_type=F32)


def _attn_big_body(lq1_ref, lk1_ref, lq2_ref, lk2_ref, gain_ref, pos_ref, posm_ref,
                   qt_ref, k_ref, vt_ref, km_ref, vmt_ref, o_ref,
                   m_ref, acc_ref, p_ref, al_ref, *, lam_init):
    blk = ATTN_BLOCK
    n_blk = qt_ref.shape[1]
    lam = _lambda(lq1_ref, lk1_ref, lq2_ref, lk2_ref, lam_init)
    diag = (lax.broadcasted_iota(jnp.int32, (blk, blk), 0)
            <= lax.broadcasted_iota(jnp.int32, (blk, blk), 1))
    nq = Q_GROUP

    def q_group(g, carry):
        q_ext = []
        for qh in range(nq):
            q = qt_ref[0, nq * g + qh].astype(F32)
            rid = lax.broadcasted_iota(jnp.int32, q.shape, 0)
            ones = jnp.where(rid < 2, 1.0, 0.0).astype(BF16)
            q_ext.append(tuple(
                jnp.concatenate([jnp.where(keep, q, 0.0).astype(BF16), ones], axis=0)
                for keep in (rid < QK_HALF, rid >= QK_HALF)))
        m_ref[...] = jnp.full(m_ref.shape, NEG_INF, F32)
        acc_ref[...] = jnp.zeros(acc_ref.shape, F32)
        p_ref[...] = jnp.zeros(p_ref.shape, BF16)
        al_ref[...] = jnp.ones(al_ref.shape, F32)

        def scores(slot, kb, kpos, targets):
            n_keys = kb.shape[0]
            sub = min(KEY_SUB, n_keys)
            for ki in range(n_keys // sub):
                rs = slice(ki * sub, (ki + 1) * sub)
                k_ext = jnp.concatenate([kb[rs], kpos[rs]], axis=1)
                for qh, mask in targets:
                    for mi in range(2):
                        st = 2 * qh + mi
                        s = jnp.dot(k_ext, q_ext[qh][mi], preferred_element_type=F32)
                        if mask is not None:
                            s = jnp.where(mask[rs], s, NEG_INF)
                        m_prev = m_ref[st]
                        m_new = jnp.maximum(m_prev, jnp.max(s, axis=0, keepdims=True))
                        al_ref[slot, ki, st] = jnp.exp(m_prev - m_new)
                        p_ref[slot, st, rs, :] = jnp.exp(s - m_new).astype(BF16)
                        m_ref[st] = m_new

        def values(slot, vt, first_q):
            n_keys = vt.shape[1]
            sub = min(KEY_SUB, n_keys)
            for ki in range(n_keys // sub):
                rs = slice(ki * sub, (ki + 1) * sub)
                for st in range(2 * first_q, 2 * nq):
                    acc_ref[st] = al_ref[slot, ki, st] * acc_ref[st] + jnp.dot(
                        vt[:, rs], p_ref[slot, st, rs, :], preferred_element_type=F32)

        def keys(kj):
            k0 = pl.multiple_of(kj * blk, blk)
            return k_ref[pl.ds(k0, blk), :], pos_ref[pl.ds(k0, blk), :]

        everyone = [(qh, None) for qh in range(nq)]
        first_key = nq * g

        def block_pair(t, c2):
            values(1, vt_ref[0, jnp.maximum(2 * t - 1, 0)], 0)
            scores(0, *keys(2 * t), everyone)
            values(0, vt_ref[0, 2 * t], 0)
            scores(1, *keys(2 * t + 1), everyone)
            return c2

        lax.fori_loop(0, first_key // 2, block_pair, 0)
        values(1, vt_ref[0, jnp.maximum(first_key - 1, 0)], 0)
        for j in range(nq):
            scores(j % 2, *keys(first_key + j), [(j, diag)] + everyone[j + 1:])
            values(j % 2, vt_ref[0, first_key + j], j)
        scores(nq % 2, km_ref[...], posm_ref[...], everyone)
        values(nq % 2, vmt_ref[0], 0)

        for qh in range(nq):
            a1, a2 = acc_ref[2 * qh], acc_ref[2 * qh + 1]
            o = (a1[0:HEAD_DIM] / a1[HEAD_DIM:HEAD_DIM + 1]
                 - lam * (a2[0:HEAD_DIM] / a2[HEAD_DIM:HEAD_DIM + 1]))
            o = o * lax.rsqrt(jnp.mean(o * o, axis=0, keepdims=True) + EPS) * gain_ref[...]
            r0 = pl.multiple_of((first_key + qh) * blk, blk)
            o_ref[pl.ds(r0, blk), :] = (o * (1.0 - lam_init)).T.astype(BF16)
        return carry

    assert n_blk % nq == 0 and nq % 2 == 0
    lax.fori_loop(0, n_blk // nq, q_group, 0)


def _position_blocks(slopes, first, n):
    shape = (slopes.shape[0], n, HEAD_DIM)
    pos = lax.broadcasted_iota(jnp.int32, shape, 1) + first
    col = lax.broadcasted_iota(jnp.int32, shape, 2)
    slope = slopes[:, None, None]
    coarse = slope * ((pos // QK_HALF) * QK_HALF).astype(F32)
    fine = slope * (pos % QK_HALF).astype(F32)
    return jnp.where(col == 0, coarse, jnp.where(col == 1, fine, 0.0)).astype(BF16)


def _attn_big(q_t, kb, v_t, k_meta, vm_t, slopes, lams, gain_col, *, lam_init, bsz):
    n_heads, n_chunks, hd, blk = q_t.shape
    m, c = kb.shape
    seq = m // bsz
    n_blk = seq // blk
    vec = lambda n: pl.BlockSpec((1, n), lambda bi, h: (0, 0))
    pos = _position_blocks(slopes, 0, seq)
    pos_meta = _position_blocks(slopes, -N_META, N_META)
    return pl.pallas_call(
        functools.partial(_attn_big_body, lam_init=lam_init),
        grid=(bsz, n_heads),
        in_specs=[
            vec(QK_HALF), vec(QK_HALF), vec(QK_HALF), vec(QK_HALF),
            pl.BlockSpec((hd, 1), lambda bi, h: (0, 0)),
            pl.BlockSpec((None, seq, hd), lambda bi, h: (h, 0, 0)),
            pl.BlockSpec((None, N_META, hd), lambda bi, h: (h, 0, 0)),
            pl.BlockSpec((1, n_blk, hd, blk), lambda bi, h: (h, bi, 0, 0)),
            pl.BlockSpec((seq, hd), lambda bi, h: (bi, h)),
            pl.BlockSpec((1, n_blk, V_ROWS, blk), lambda bi, h: (h, bi, 0, 0)),
            pl.BlockSpec((N_META, hd), lambda bi, h: (0, h)),
            pl.BlockSpec((1, V_ROWS, N_META), lambda bi, h: (h, 0, 0)),
        ],
        out_specs=pl.BlockSpec((seq, hd), lambda bi, h: (bi, h)),
        out_shape=jax.ShapeDtypeStruct((m, c), BF16),
        scratch_shapes=[
            pltpu.VMEM((2 * Q_GROUP, 1, blk), F32),
            pltpu.VMEM((2 * Q_GROUP, V_ROWS, blk), F32),
            pltpu.VMEM((2, 2 * Q_GROUP, blk, blk), BF16),
            pltpu.VMEM((2, blk // KEY_SUB, 2 * Q_GROUP, 1, blk), F32),
        ],
        compiler_params=_params("parallel", "arbitrary"),
        name="attn_big",
    )(*lams, gain_col, pos, pos_meta, q_t, kb, v_t, k_meta, vm_t)


def _attn_meta_body(slope_ref, lq1_ref, lk1_ref, lq2_ref, lk2_ref, gain_ref,
                    q_ref, k_ref, v_ref, o_ref, *, lam_init):
    h = pl.program_id(0)
    slope = slope_ref[h]
    lam = _lambda(lq1_ref, lk1_ref, lq2_ref, lk2_ref, lam_init)
    qs = _split_halves(q_ref[...], 1)
    kb = k_ref[...]
    shape = (N_META, kb.shape[0])
    qpos = lax.broadcasted_iota(jnp.int32, shape, 0)
    kpos = lax.broadcasted_iota(jnp.int32, shape, 1)
    dist = (qpos - kpos).astype(F32)
    mask = (qpos >= kpos) & (kpos < N_META)
    outs = []
    for mi in range(2):
        s = jnp.where(mask, _nt_dot(qs[mi], kb) - slope * dist, NEG_INF)
        p = jnp.exp(s - jnp.max(s, axis=-1, keepdims=True))
        p = p / jnp.sum(p, axis=-1, keepdims=True)
        outs.append(p)
    a = (outs[0] - lam * outs[1]).astype(BF16)
    o = jnp.dot(a, v_ref[...], preferred_element_type=F32)
    o_ref[...] = (_rms(o, gain_ref[...]) * (1.0 - lam_init)).astype(BF16)


def _attn_meta(q_s, kb_s, vb_s, slopes, lams, gain, *, lam_init):
    m, c = kb_s.shape
    vec = lambda n: pl.BlockSpec((1, n), lambda h: (0, 0))
    return pl.pallas_call(
        functools.partial(_attn_meta_body, lam_init=lam_init),
        grid=(N_HEADS,),
        in_specs=[
            pl.BlockSpec(memory_space=pltpu.SMEM),
            vec(QK_HALF), vec(QK_HALF), vec(QK_HALF), vec(QK_HALF), vec(HEAD_DIM),
            pl.BlockSpec((N_META, HEAD_DIM), lambda h: (0, h)),
            pl.BlockSpec((m, HEAD_DIM), lambda h: (0, h)),
            pl.BlockSpec((m, HEAD_DIM), lambda h: (0, h)),
        ],
        out_specs=pl.BlockSpec((N_META, HEAD_DIM), lambda h: (0, h)),
        out_shape=jax.ShapeDtypeStruct((N_META, c), BF16),
        compiler_params=_params("parallel"),
        name="attn_meta",
    )(slopes, *lams, gain, q_s, kb_s, vb_s)


def _attn_sample_body(pt_ref, lq1_ref, lk1_ref, lq2_ref, lk2_ref, gain_ref,
                      q_ref, kn_ref, vn_ref, *rest, lam_init, t_dec, past_len):
    n_pg = PAGES_PER_STEP
    k_refs, v_refs = rest[:n_pg], rest[n_pg:2 * n_pg]
    o_ref, qm_ref, m_ref, l_ref, acc_ref, bias_ref, p_ref, al_ref = rest[2 * n_pg:]
    pg = pl.program_id(1)
    n_steps = pl.num_programs(1) - 1
    page = k_refs[0].shape[0]
    grp = 2 * t_dec
    rows = grp * N_HEADS
    cols = page * N_HEADS
    row1 = lax.broadcasted_iota(jnp.int32, (rows, 1), 0)
    r_q = row1 % t_dec
    slope = jnp.exp2(-((row1 // grp) + 1).astype(F32))

    def update(kb, vb, bias, shift):
        s = _nt_dot(qm_ref[...], kb) + bias
        m_prev = m_ref[...]
        m_new = jnp.maximum(m_prev, jnp.max(s, axis=-1, keepdims=True) + shift)
        alpha = jnp.exp(m_prev - m_new)
        p = jnp.exp(s - (m_new - shift))
        l_ref[...] = alpha * l_ref[...] + jnp.sum(p, axis=-1, keepdims=True)
        acc_ref[...] = alpha * acc_ref[...] + jnp.dot(p.astype(BF16), vb, preferred_element_type=F32)
        m_ref[...] = m_new

    @pl.when(pg == 0)
    def _():
        q = q_ref[0]
        r2 = lax.broadcasted_iota(jnp.int32, q.shape, 0)
        c2 = lax.broadcasted_iota(jnp.int32, q.shape, 1)
        qm_ref[...] = jnp.where((c2 // QK_HALF) == (r2 % grp) // t_dec, q, jnp.zeros_like(q))
        m_ref[...] = jnp.full(m_ref.shape, NEG_INF, F32)
        l_ref[...] = jnp.zeros(l_ref.shape, F32)
        acc_ref[...] = jnp.zeros(acc_ref.shape, F32)
        p_ref[...] = jnp.zeros(p_ref.shape, BF16)
        al_ref[...] = jnp.ones(al_ref.shape, F32)
        rw = lax.broadcasted_iota(jnp.int32, (rows, cols), 0)
        cl = lax.broadcasted_iota(jnp.int32, (rows, cols), 1)
        bias_ref[...] = jnp.where((cl % N_HEADS) == (rw // grp),
                                  slope * (cl // N_HEADS).astype(F32), NEG_INF)
        rw = lax.broadcasted_iota(jnp.int32, (rows, t_dec * N_HEADS), 0)
        cl = lax.broadcasted_iota(jnp.int32, (rows, t_dec * N_HEADS), 1)
        r_k = cl // N_HEADS
        ok = ((cl % N_HEADS) == (rw // grp)) & (r_k <= rw % t_dec)
        bias_new = jnp.where(ok, -slope * (rw % t_dec - r_k).astype(F32), NEG_INF)
        update(kn_ref[0], vn_ref[0], bias_new, 0.0)

    n_grp = n_pg // SAMPLE_GROUP

    def values():
        acc = acc_ref[...]
        for g in range(n_grp):
            acc = al_ref[g] * acc
            for gi in range(g * SAMPLE_GROUP, (g + 1) * SAMPLE_GROUP):
                vb = v_refs[gi][...].reshape(cols, HEAD_DIM).astype(BF16)
                acc = acc + jnp.dot(p_ref[:, gi * cols:(gi + 1) * cols], vb, preferred_element_type=F32)
        acc_ref[...] = acc

    @pl.when(pg < n_steps)
    def _():
        values()
        qm = qm_ref[...]
        m_run, l_run = m_ref[...], l_ref[...]
        for g in range(n_grp):
            s_pg, top = [], None
            for gi in range(g * SAMPLE_GROUP, (g + 1) * SAMPLE_GROUP):
                start = (pg * n_pg + gi) * page
                shift = slope * (start - past_len - r_q).astype(F32)
                kb = k_refs[gi][...].reshape(cols, HEAD_DIM).astype(BF16)
                s = _nt_dot(qm, kb) + bias_ref[...]
                s_pg.append((gi, s, shift))
                top_g = jnp.max(s, axis=-1, keepdims=True) + shift
                top = top_g if top is None else jnp.maximum(top, top_g)
            m_new = jnp.maximum(m_run, top)
            alpha = jnp.exp(m_run - m_new)
            l_run = alpha * l_run
            for gi, s, shift in s_pg:
                p = jnp.exp(s + (shift - m_new))
                l_run = l_run + jnp.sum(p, axis=-1, keepdims=True)
                p_ref[:, gi * cols:(gi + 1) * cols] = p.astype(BF16)
            al_ref[g] = alpha
            m_run = m_new
        m_ref[...] = m_run
        l_ref[...] = l_run

    @pl.when(pg == n_steps)
    def _():
        values()
        lam = _lambda(lq1_ref, lk1_ref, lq2_ref, lk2_ref, lam_init)
        for hd in range(N_HEADS):
            rs = slice(hd * grp, (hd + 1) * grp)
            n = acc_ref[rs, :] / l_ref[rs, :]
            o = n - lam * pltpu.roll(n, t_dec, 0)
            o_ref[0, rs, :] = _rms(o, gain_ref[...]) * (1.0 - lam_init)


def _attn_sample(page_table, q_rows, k_new, v_new, cache_k, cache_v, lams, gain, *, lam_init, t_dec):
    n_seq, n_pages = page_table.shape
    _, _, page, n_heads, hd = cache_k.shape
    n_pg = PAGES_PER_STEP
    assert n_pages % n_pg == 0 and n_heads == N_HEADS and hd == HEAD_DIM and 2 * t_dec == 8
    rows = 2 * t_dec * N_HEADS
    vec = lambda n: pl.BlockSpec((1, n), lambda s, p, pt: (0, 0))

    n_steps = n_pages // n_pg

    def page_spec(gi, lag):
        def index(s, p, pt):
            grp = jnp.maximum(p - 1, 0) if lag else jnp.minimum(p, n_steps - 1)
            return (0, pt[s, grp * n_pg + gi], 0, 0, 0)
        return pl.BlockSpec((None, None, page, n_heads, hd), index)

    grid_spec = pltpu.PrefetchScalarGridSpec(
        num_scalar_prefetch=1,
        grid=(n_seq, n_steps + 1),
        in_specs=[
            vec(QK_HALF), vec(QK_HALF), vec(QK_HALF), vec(QK_HALF), vec(HEAD_DIM),
            pl.BlockSpec((1, rows, hd), lambda s, p, pt: (s, 0, 0)),
            pl.BlockSpec((1, t_dec * n_heads, hd), lambda s, p, pt: (s, 0, 0)),
            pl.BlockSpec((1, t_dec * n_heads, hd), lambda s, p, pt: (s, 0, 0)),
        ] + [page_spec(gi, False) for gi in range(n_pg)] + [page_spec(gi, True) for gi in range(n_pg)],
        out_specs=pl.BlockSpec((1, rows, hd), lambda s, p, pt: (s, 0, 0)),
        scratch_shapes=[
            pltpu.VMEM((rows, hd), BF16),
            pltpu.VMEM((rows, 1), F32),
            pltpu.VMEM((rows, 1), F32),
            pltpu.VMEM((rows, hd), F32),
            pltpu.VMEM((rows, page * n_heads), F32),
            pltpu.VMEM((rows, n_pg * page * n_heads), BF16),
            pltpu.VMEM((n_pg // SAMPLE_GROUP, rows, 1), F32),
        ],
    )
    return pl.pallas_call(
        functools.partial(_attn_sample_body, lam_init=lam_init, t_dec=t_dec, past_len=n_pages * page),
        grid_spec=grid_spec,
        out_shape=jax.ShapeDtypeStruct((n_seq, rows, hd), F32),
        compiler_params=_params("parallel", "arbitrary"),
        name="attn_sample",
    )(page_table, *lams, gain, q_rows, k_new, v_new,
      *([cache_k] * n_pg), *([cache_v] * n_pg))


def _outproj_body(x_ref, a_ref, b_ref, w_ref, o_ref):
    half = a_ref.shape[1]
    o_ref[...] = (x_ref[...]
                  + jnp.dot(a_ref[...], w_ref[0:half, :], preferred_element_type=F32)
                  + jnp.dot(b_ref[...], w_ref[half:, :], preferred_element_type=F32))


def _outproj(x, pool_out, attn_out, w_out, *, tm):
    m, d = x.shape
    c = pool_out.shape[1]
    return pl.pallas_call(
        _outproj_body,
        grid=(m // tm,),
        in_specs=[
            pl.BlockSpec((tm, d), lambda i: (i, 0)),
            pl.BlockSpec((tm, c), lambda i: (i, 0)),
            pl.BlockSpec((tm, c), lambda i: (i, 0)),
            pl.BlockSpec(w_out.shape, lambda i: (0, 0)),
        ],
        out_specs=pl.BlockSpec((tm, d), lambda i: (i, 0)),
        out_shape=jax.ShapeDtypeStruct((m, d), F32),
        compiler_params=_params("parallel"),
        name="outproj",
    )(x, pool_out, attn_out, w_out)


def kernel(x_prompt, x_sample, cache_k, cache_v, state_pool, page_table, meta_tokens,
           norm_ffn1, w_gate1, w_up1, w_down1, norm_mix, w_in, w_pool, pool_scale,
           lambda_q1, lambda_k1, lambda_q2, lambda_k2, subln_gain, w_out,
           norm_ffn2, w_gate2, w_up2, w_down2, norm_final):
    depth = w_in.shape[0]
    assert depth == 1
    bsz, seq, d = x_prompt.shape
    n_seq, t_dec, _ = x_sample.shape
    c = N_HEADS * HEAD_DIM
    n_small = N_META + n_seq * t_dec
    assert n_small <= SMALL_ROWS and 2 * t_dec == 8
    lam_init = 0.8 - 0.6 * math.exp(-0.3 * 0)

    row = lambda a: a.reshape(1, -1)
    w_in_b, w_out_b, w_pool_b = w_in[0].astype(BF16), w_out[0].astype(BF16), w_pool[0].astype(BF16)
    g1, gm, g2, gf = row(norm_ffn1[0]), row(norm_mix[0]), row(norm_ffn2[0]), row(norm_final)
    pscale, gain = row(pool_scale[0]), row(subln_gain[0])
    lams = (row(lambda_q1[0]), row(lambda_k1[0]), row(lambda_q2[0]), row(lambda_k2[0]))
    slopes = jnp.exp2(-jnp.arange(1, N_HEADS + 1, dtype=F32))

    x_big = x_prompt.reshape(bsz * seq, d)
    x_small = jnp.concatenate([meta_tokens, x_sample.reshape(n_seq * t_dec, d),
                               jnp.zeros((SMALL_ROWS - n_small, d), F32)], axis=0)

    xb1, xs1 = _ffn(x_big, x_small, g1, w_gate1[0], w_up1[0], w_down1[0], gf, final_norm=False)

    ps, ks, vs, kbs, qs, vbs = _proj(xs1, gm, w_in_b, tm=SMALL_ROWS, transposed=False)
    state16 = jnp.pad(state_pool[0], ((0, 0), (HALO - POOL_BUF, 0), (0, 0)))
    pool_s = _pool_small(ps, state16, w_pool_b, pscale, n_seq=n_seq, t_dec=t_dec).astype(BF16)
    o_meta = _attn_meta(qs, kbs, vbs, slopes, lams, gain, lam_init=lam_init)
    q_dec = qs[N_META:n_small].reshape(n_seq, t_dec, N_HEADS, HEAD_DIM).transpose(0, 2, 1, 3)
    q_rows = jnp.broadcast_to(q_dec[:, :, None], (n_seq, N_HEADS, 2, t_dec, HEAD_DIM))
    q_rows = q_rows.reshape(n_seq, 2 * t_dec * N_HEADS, HEAD_DIM)
    k_new = kbs[N_META:n_small].reshape(n_seq, t_dec * N_HEADS, HEAD_DIM)
    v_new = vbs[N_META:n_small].reshape(n_seq, t_dec * N_HEADS, HEAD_DIM)
    o_dec = _attn_sample(page_table, q_rows, k_new, v_new, cache_k, cache_v, lams, gain,
                         lam_init=lam_init, t_dec=t_dec)
    o_dec = o_dec.reshape(n_seq, N_HEADS, 2, t_dec, HEAD_DIM)[:, :, 0].transpose(0, 2, 1, 3)
    o_small = jnp.concatenate([o_meta, o_dec.reshape(n_seq * t_dec, c).astype(BF16),
                               jnp.zeros((SMALL_ROWS - n_small, c), BF16)], axis=0)
    xs2 = _outproj(xs1, pool_s, o_small, w_out_b, tm=SMALL_ROWS)

    pb, kb, vb, kbb, q_t, v_t = _proj(xb1, gm, w_in_b, tm=512, transposed=True, seq=seq)
    pool_b = _pool_big(pb.reshape(bsz, seq, c), ps[:N_META], w_pool_b, pscale, tm=512)
    vm_t = vbs[:N_META].reshape(N_META, N_HEADS, HEAD_DIM).transpose(1, 2, 0)
    ones_rows = jnp.zeros((N_HEADS, V_ROWS - HEAD_DIM, N_META), BF16).at[:, 0].set(1.0)
    vm_t = jnp.concatenate([vm_t, ones_rows], axis=1)
    o_big = _attn_big(q_t, kbb, v_t, kbs[:N_META], vm_t, slopes, lams, gain.reshape(HEAD_DIM, 1),
                      lam_init=lam_init, bsz=bsz)
    xb2 = _outproj(xb1, pool_b.reshape(bsz * seq, c), o_big, w_out_b, tm=512)
    yb, ys = _ffn(xb2, xs2, g2, w_gate2[0], w_up2[0], w_down2[0], gf, final_norm=True)

    k_rows, v_rows = _fill_meta(kb, vb, ks[:N_META], vs[:N_META], bsz=bsz)
    prompt_rows = lambda a: a.reshape(1, bsz, N_META + seq, N_HEADS, HEAD_DIM)
    dec = lambda a: a[N_META:n_small].reshape(1, n_seq, t_dec, N_HEADS, HEAD_DIM)
    y_prompt = yb.reshape(bsz, seq, d)
    y_sample = ys[N_META:n_small].reshape(n_seq, t_dec, d)
    pool_prompt = pb.reshape(bsz, seq, c)[:, seq - POOL_BUF:][None]
    pool_sample = jnp.concatenate([state_pool[0][:, t_dec:], ps[N_META:n_small].reshape(n_seq, t_dec, c)],
                                  axis=1)[None]
    return (y_prompt, y_sample, prompt_rows(k_rows), prompt_rows(v_rows), pool_prompt,
            dec(ks), dec(vs), pool_sample)
```

```python
import functools
import math

import jax
import jax.numpy as jnp
from jax import lax
from jax.experimental import pallas as pl
from jax.experimental.pallas import tpu as pltpu

F32 = jnp.float32
BF16 = jnp.bfloat16

N_META = 16
POOL_WINDOWS = (2, 4, 8, 16)
POOL_BUF = max(POOL_WINDOWS) - 1
HALO = 16
N_HEADS = 8
HEAD_DIM = 128
QK_HALF = HEAD_DIM // 2
EPS = 1e-6
NEG_INF = -1e30
SMALL_ROWS = 128
ATTN_BLOCK = 256
V_ROWS = HEAD_DIM + 16
Q_GROUP = 4
KEY_SUB = 128
FFN_ROWS = 1024
FFN_CHUNK = 128
FFN_COLS = 256
SAMPLE_GROUP = 2
PAGES_PER_STEP = 16

VMEM_LIMIT = 60 * 1024 * 1024


def _params(*sem):
    return pltpu.CompilerParams(dimension_semantics=sem, vmem_limit_bytes=VMEM_LIMIT)


def _rms(x, g):
    return x * lax.rsqrt(jnp.mean(x * x, axis=-1, keepdims=True) + EPS) * g


def _ffn_body(xb_ref, xs_ref, g_ref, wg_ref, wu_ref, wd_ref, gf_ref, ob_ref, os_ref,
              hb_ref, hs_ref, wgb_ref, wub_ref, wdb_ref, *, final_norm):
    i, s = pl.program_id(0), pl.program_id(1)
    n_blk = pl.num_programs(1) - 1

    def round_weights():
        slot = s % 2
        wgb_ref[slot] = wg_ref[...].astype(BF16)
        wub_ref[slot] = wu_ref[...].astype(BF16)
        wdb_ref[slot] = wd_ref[...].astype(BF16)

    def row_chunks(ref):
        n = ref.shape[0]
        step = min(n, FFN_CHUNK)
        return [slice(r, r + step) for r in range(0, n, step)]

    def start(x_ref, o_ref, h_ref):
        for rs in row_chunks(x_ref):
            h_ref[rs, :] = _rms(x_ref[rs, :], g_ref[...]).astype(BF16)
            o_ref[rs, :] = jnp.zeros((rs.stop - rs.start, o_ref.shape[1]), F32)

    def accumulate(o_ref, h_ref):
        slot = (s + 1) % 2
        h = h_ref[...]
        g = jnp.dot(h, wgb_ref[slot], preferred_element_type=F32)
        u = jnp.dot(h, wub_ref[slot], preferred_element_type=F32)
        a = (g * jax.nn.sigmoid(g) * u).astype(BF16)
        o_ref[...] += jnp.dot(a, wdb_ref[slot], preferred_element_type=F32)

    def finish(x_ref, o_ref):
        for rs in row_chunks(x_ref):
            y = x_ref[rs, :] + 0.5 * o_ref[rs, :]
            if final_norm:
                y = _rms(y, gf_ref[...])
            o_ref[rs, :] = y

    last_tile = i == pl.num_programs(0) - 1

    @pl.when(s == 0)
    def _():
        round_weights()
        start(xb_ref, ob_ref, hb_ref)

        @pl.when(last_tile)
        def _():
            start(xs_ref, os_ref, hs_ref)

    @pl.when(s > 0)
    def _():
        accumulate(ob_ref, hb_ref)
        round_weights()

        @pl.when(last_tile)
        def _():
            accumulate(os_ref, hs_ref)

    @pl.when(s == n_blk)
    def _():
        finish(xb_ref, ob_ref)

        @pl.when(last_tile)
        def _():
            finish(xs_ref, os_ref)


def _ffn(x_big, x_small, g, wg, wu, wd, gf, *, final_norm):
    m, d = x_big.shape
    ms = x_small.shape[0]
    f = wg.shape[1]
    tm, tf = FFN_ROWS, FFN_COLS
    assert m % tm == 0 and f % tf == 0
    n_blk = f // tf
    const = lambda i, s: (0, 0)
    blk = lambda s: jnp.minimum(s, n_blk - 1)
    return pl.pallas_call(
        functools.partial(_ffn_body, final_norm=final_norm),
        grid=(m // tm, n_blk + 1),
        in_specs=[
            pl.BlockSpec((tm, d), lambda i, s: (i, 0), pipeline_mode=pl.Buffered(1)),
            pl.BlockSpec((ms, d), const),
            pl.BlockSpec((1, d), const),
            pl.BlockSpec((d, tf), lambda i, s: (0, blk(s))),
            pl.BlockSpec((d, tf), lambda i, s: (0, blk(s))),
            pl.BlockSpec((tf, d), lambda i, s: (blk(s), 0)),
            pl.BlockSpec((1, d), const),
        ],
        out_specs=[pl.BlockSpec((tm, d), lambda i, s: (i, 0)), pl.BlockSpec((ms, d), const)],
        out_shape=[jax.ShapeDtypeStruct((m, d), F32), jax.ShapeDtypeStruct((ms, d), F32)],
        scratch_shapes=[
            pltpu.VMEM((tm, d), BF16), pltpu.VMEM((ms, d), BF16),
            pltpu.VMEM((2, d, tf), BF16), pltpu.VMEM((2, d, tf), BF16), pltpu.VMEM((2, tf, d), BF16),
        ],
        compiler_params=_params("arbitrary", "arbitrary"),
        name="ffn",
    )(x_big, x_small, g, wg, wu, wd, gf)


def _proj_body(x_ref, g_ref, w_ref, p_ref, k_ref, v_ref, kb_ref, a_ref, b_ref, *, transposed):
    h = _rms(x_ref[...], g_ref[...]).astype(BF16)
    w = p_ref.shape[1]

    def block(j):
        return jnp.dot(h, w_ref[:, j * w:(j + 1) * w], preferred_element_type=F32)

    def put_t(dst, val, extra_ones):
        tc = dst.shape[3]
        for hd in range(N_HEADS):
            for ci in range(dst.shape[1]):
                blk = val[ci * tc:(ci + 1) * tc, hd * HEAD_DIM:(hd + 1) * HEAD_DIM]
                dst[hd, ci, 0:HEAD_DIM, :] = blk.T.astype(BF16)
                if extra_ones:
                    rid = lax.broadcasted_iota(jnp.int32, (dst.shape[2] - HEAD_DIM, tc), 0)
                    dst[hd, ci, HEAD_DIM:, :] = jnp.where(rid == 0, 1.0, 0.0).astype(BF16)

    p_ref[...] = block(0)
    q = block(1) * (QK_HALF ** -0.5)
    if transposed:
        put_t(a_ref, q, False)
    else:
        a_ref[...] = q.astype(BF16)
    k = block(2)
    k_ref[...] = k
    kb_ref[...] = k.astype(BF16)
    v = block(3)
    v_ref[...] = v
    if transposed:
        put_t(b_ref, v, True)
    else:
        b_ref[...] = v.astype(BF16)


def _proj(x, g, w_in, *, tm, transposed, seq=None):
    m, d = x.shape
    w = w_in.shape[1] // 4
    row = lambda i: (i, 0)
    rm_spec = pl.BlockSpec((tm, w), row)
    rm_shape = jax.ShapeDtypeStruct((m, w), BF16)
    f32_shape = jax.ShapeDtypeStruct((m, w), F32)
    kv_spec, kv_shape = rm_spec, f32_shape
    if transposed:
        tc = ATTN_BLOCK
        t_spec = lambda n: pl.BlockSpec((N_HEADS, tm // tc, n, tc), lambda i: (0, i, 0, 0))
        t_shape = lambda n: jax.ShapeDtypeStruct((N_HEADS, m // tc, n, tc), BF16)
        extra_specs = [rm_spec, t_spec(HEAD_DIM), t_spec(V_ROWS)]
        extra_shapes = [rm_shape, t_shape(HEAD_DIM), t_shape(V_ROWS)]
        per_seq = seq // tm
        kv_spec = pl.BlockSpec(
            (pl.Element(tm), pl.Element(w)),
            lambda i: (pl.multiple_of(
                (i // per_seq) * (N_META + seq) + N_META + (i % per_seq) * tm, N_META), 0))
        kv_shape = jax.ShapeDtypeStruct((m // seq * (N_META + seq), w), F32)
    else:
        extra_specs = [rm_spec] * 3
        extra_shapes = [rm_shape] * 3
    return pl.pallas_call(
        functools.partial(_proj_body, transposed=transposed),
        grid=(m // tm,),
        in_specs=[
            pl.BlockSpec((tm, d), row),
            pl.BlockSpec((1, d), lambda i: (0, 0)),
            pl.BlockSpec(w_in.shape, lambda i: (0, 0), pipeline_mode=pl.Buffered(1)),
        ],
        out_specs=[rm_spec, kv_spec, kv_spec] + extra_specs,
        out_shape=[f32_shape, kv_shape, kv_shape] + extra_shapes,
        compiler_params=_params("parallel"),
        name="proj",
    )(x, g, w_in)


def _fill_meta_body(km_ref, vm_ref, k_in_ref, v_in_ref, k_ref, v_ref):
    del k_in_ref, v_in_ref
    k_ref[...] = km_ref[...]
    v_ref[...] = vm_ref[...]


def _fill_meta(k_rows, v_rows, k_meta, v_meta, *, bsz):
    m, w = k_rows.shape
    per_seq = m // bsz
    assert per_seq % N_META == 0
    meta_spec = pl.BlockSpec((N_META, w), lambda b: (0, 0))
    out_spec = pl.BlockSpec((N_META, w), lambda b: (b * (per_seq // N_META), 0))
    shape = jax.ShapeDtypeStruct((m, w), k_rows.dtype)
    return pl.pallas_call(
        _fill_meta_body,
        grid=(bsz,),
        in_specs=[meta_spec, meta_spec, pl.BlockSpec(memory_space=pl.ANY), pl.BlockSpec(memory_space=pl.ANY)],
        out_specs=[out_spec, out_spec],
        out_shape=[shape, shape],
        input_output_aliases={2: 0, 3: 1},
        compiler_params=_params("arbitrary"),
        name="fill_meta",
    )(k_meta, v_meta, k_rows, v_rows)


def _pool_groups(buf_ref, rows, wp_ref, sc_ref, count_fn, write):
    gd = wp_ref.shape[1]
    for gi, w in enumerate(POOL_WINDOWS):
        cs = slice(gi * gd, (gi + 1) * gd)
        cur = buf_ref[HALO:HALO + rows, cs]
        acc = cur
        for dlt in range(1, w):
            acc = acc + buf_ref[HALO - dlt:HALO - dlt + rows, cs]
        feat = acc / count_fn(w) - cur
        out = jnp.dot(feat.astype(BF16), wp_ref[gi], preferred_element_type=F32) * sc_ref[:, cs]
        write(cs, out)


def _pool_small_body(p_ref, st_ref, wp_ref, sc_ref, o_ref, buf_ref, *, n_seq, t_dec):
    buf_ref[0:HALO, :] = jnp.zeros((HALO, buf_ref.shape[1]), F32)
    buf_ref[HALO:2 * HALO, :] = p_ref[0:N_META, :]
    pos1 = lax.broadcasted_iota(jnp.int32, (N_META, 1), 0).astype(F32) + 1.0

    def write_meta(cs, val):
        o_ref[0:N_META, cs] = val

    _pool_groups(buf_ref, N_META, wp_ref, sc_ref, lambda w: jnp.minimum(pos1, float(w)), write_meta)

    o_ref[N_META + n_seq * t_dec:, :] = jnp.zeros(
        (o_ref.shape[0] - N_META - n_seq * t_dec, o_ref.shape[1]), F32)
    for s in range(n_seq):
        r0 = N_META + s * t_dec
        buf_ref[0:HALO, :] = st_ref[s]
        buf_ref[HALO:HALO + 8, :] = p_ref[r0:r0 + 8, :]

        def write_s(cs, val, r0=r0):
            o_ref[r0:r0 + t_dec, cs] = val[0:t_dec]

        _pool_groups(buf_ref, 8, wp_ref, sc_ref, lambda w: float(w), write_s)


def _pool_small(p_small, state16, w_pool, pool_scale, *, n_seq, t_dec):
    m, c = p_small.shape
    return pl.pallas_call(
        functools.partial(_pool_small_body, n_seq=n_seq, t_dec=t_dec),
        out_shape=jax.ShapeDtypeStruct((m, c), F32),
        scratch_shapes=[pltpu.VMEM((2 * HALO, c), F32)],
        compiler_params=pltpu.CompilerParams(vmem_limit_bytes=VMEM_LIMIT),
        name="pool_small",
    )(p_small, state16, w_pool, pool_scale)


def _lambda(lq1_ref, lk1_ref, lq2_ref, lk2_ref, lam_init):
    a = jnp.sum(lq1_ref[...] * lk1_ref[...], axis=-1, keepdims=True)
    b = jnp.sum(lq2_ref[...] * lk2_ref[...], axis=-1, keepdims=True)
    return jnp.exp(a) - jnp.exp(b) + lam_init


def _split_halves(q, axis):
    idx = lax.broadcasted_iota(jnp.int32, q.shape, axis) % HEAD_DIM
    zero = jnp.zeros_like(q)
    return jnp.where(idx < QK_HALF, q, zero), jnp.where(idx >= QK_HALF, q, zero)


def _nt_dot(a, b):
    return lax.dot_general(a, b, (((1,), (1,)), ((), ())), preferred_element_type=F32)


def _attn_big_body(lq1_ref, lk1_ref, lq2_ref, lk2_ref, gain_ref, pos_ref, posm_ref,
                   qt_ref, k_ref, vt_ref, km_ref, vmt_ref, o_ref,
                   m_ref, acc_ref, p_ref, al_ref, *, lam_init):
    blk = ATTN_BLOCK
    n_blk = qt_ref.shape[1]
    lam = _lambda(lq1_ref, lk1_ref, lq2_ref, lk2_ref, lam_init)
    diag = (lax.broadcasted_iota(jnp.int32, (blk, blk), 0)
            <= lax.broadcasted_iota(jnp.int32, (blk, blk), 1))
    nq = Q_GROUP

    def q_group(g, carry):
        q_ext = []
        for qh in range(nq):
            q = qt_ref[0, nq * g + qh].astype(F32)
            rid = lax.broadcasted_iota(jnp.int32, q.shape, 0)
            ones = jnp.where(rid < 2, 1.0, 0.0).astype(BF16)
            q_ext.append(tuple(
                jnp.concatenate([jnp.where(keep, q, 0.0).astype(BF16), ones], axis=0)
                for keep in (rid < QK_HALF, rid >= QK_HALF)))
        m_ref[...] = jnp.full(m_ref.shape, NEG_INF, F32)
        acc_ref[...] = jnp.zeros(acc_ref.shape, F32)
        p_ref[...] = jnp.zeros(p_ref.shape, BF16)
        al_ref[...] = jnp.ones(al_ref.shape, F32)

        def scores(slot, kb, kpos, targets):
            n_keys = kb.shape[0]
            sub = min(KEY_SUB, n_keys)
            for ki in range(n_keys // sub):
                rs = slice(ki * sub, (ki + 1) * sub)
                k_ext = jnp.concatenate([kb[rs], kpos[rs]], axis=1)
                for qh, mask in targets:
                    for mi in range(2):
                        st = 2 * qh + mi
                        s = jnp.dot(k_ext, q_ext[qh][mi], preferred_element_type=F32)
                        if mask is not None:
                            s = jnp.where(mask[rs], s, NEG_INF)
                        m_prev = m_ref[st]
                        m_new = jnp.maximum(m_prev, jnp.max(s, axis=0, keepdims=True))
                        al_ref[slot, ki, st] = jnp.exp(m_prev - m_new)
                        p_ref[slot, st, rs, :] = jnp.exp(s - m_new).astype(BF16)
                        m_ref[st] = m_new

        def values(slot, vt, first_q):
            n_keys = vt.shape[1]
            sub = min(KEY_SUB, n_keys)
            for ki in range(n_keys // sub):
                rs = slice(ki * sub, (ki + 1) * sub)
                for st in range(2 * first_q, 2 * nq):
                    acc_ref[st] = al_ref[slot, ki, st] * acc_ref[st] + jnp.dot(
                        vt[:, rs], p_ref[slot, st, rs, :], preferred_element_type=F32)

        def keys(kj):
            k0 = pl.multiple_of(kj * blk, blk)
            return k_ref[pl.ds(k0, blk), :], pos_ref[pl.ds(k0, blk), :]

        everyone = [(qh, None) for qh in range(nq)]
        first_key = nq * g

        def block_pair(t, c2):
            values(1, vt_ref[0, jnp.maximum(2 * t - 1, 0)], 0)
            scores(0, *keys(2 * t), everyone)
            values(0, vt_ref[0, 2 * t], 0)
            scores(1, *keys(2 * t + 1), everyone)
            return c2

        lax.fori_loop(0, first_key // 2, block_pair, 0)
        values(1, vt_ref[0, jnp.maximum(first_key - 1, 0)], 0)
        for j in range(nq):
            scores(j % 2, *keys(first_key + j), [(j, diag)] + everyone[j + 1:])
            values(j % 2, vt_ref[0, first_key + j], j)
        scores(nq % 2, km_ref[...], posm_ref[...], everyone)
        values(nq % 2, vmt_ref[0], 0)

        for qh in range(nq):
            a1, a2 = acc_ref[2 * qh], acc_ref[2 * qh + 1]
            o = (a1[0:HEAD_DIM] / a1[HEAD_DIM:HEAD_DIM + 1]
                 - lam * (a2[0:HEAD_DIM] / a2[HEAD_DIM:HEAD_DIM + 1]))
            o = o * lax.rsqrt(jnp.mean(o * o, axis=0, keepdims=True) + EPS) * gain_ref[...]
            r0 = pl.multiple_of((first_key + qh) * blk, blk)
            o_ref[pl.ds(r0, blk), :] = (o * (1.0 - lam_init)).T.astype(BF16)
        return carry

    assert n_blk % nq == 0 and nq % 2 == 0
    lax.fori_loop(0, n_blk // nq, q_group, 0)


def _position_blocks(slopes, first, n):
    shape = (slopes.shape[0], n, HEAD_DIM)
    pos = lax.broadcasted_iota(jnp.int32, shape, 1) + first
    col = lax.broadcasted_iota(jnp.int32, shape, 2)
    slope = slopes[:, None, None]
    coarse = slope * ((pos // QK_HALF) * QK_HALF).astype(F32)
    fine = slope * (pos % QK_HALF).astype(F32)
    return jnp.where(col == 0, coarse, jnp.where(col == 1, fine, 0.0)).astype(BF16)


def _attn_big(q_t, kb, v_t, k_meta, vm_t, slopes, lams, gain_col, *, lam_init, bsz):
    n_heads, n_chunks, hd, blk = q_t.shape
    m, c = kb.shape
    seq = m // bsz
    n_blk = seq // blk
    vec = lambda n: pl.BlockSpec((1, n), lambda bi, h: (0, 0))
    pos = _position_blocks(slopes, 0, seq)
    pos_meta = _position_blocks(slopes, -N_META, N_META)
    return pl.pallas_call(
        functools.partial(_attn_big_body, lam_init=lam_init),
        grid=(bsz, n_heads),
        in_specs=[
            vec(QK_HALF), vec(QK_HALF), vec(QK_HALF), vec(QK_HALF),
            pl.BlockSpec((hd, 1), lambda bi, h: (0, 0)),
            pl.BlockSpec((None, seq, hd), lambda bi, h: (h, 0, 0)),
            pl.BlockSpec((None, N_META, hd), lambda bi, h: (h, 0, 0)),
            pl.BlockSpec((1, n_blk, hd, blk), lambda bi, h: (h, bi, 0, 0)),
            pl.BlockSpec((seq, hd), lambda bi, h: (bi, h)),
            pl.BlockSpec((1, n_blk, V_ROWS, blk), lambda bi, h: (h, bi, 0, 0)),
            pl.BlockSpec((N_META, hd), lambda bi, h: (0, h)),
            pl.BlockSpec((1, V_ROWS, N_META), lambda bi, h: (h, 0, 0)),
        ],
        out_specs=pl.BlockSpec((seq, hd), lambda bi, h: (bi, h)),
        out_shape=jax.ShapeDtypeStruct((m, c), BF16),
        scratch_shapes=[
            pltpu.VMEM((2 * Q_GROUP, 1, blk), F32),
            pltpu.VMEM((2 * Q_GROUP, V_ROWS, blk), F32),
            pltpu.VMEM((2, 2 * Q_GROUP, blk, blk), BF16),
            pltpu.VMEM((2, blk // KEY_SUB, 2 * Q_GROUP, 1, blk), F32),
        ],
        compiler_params=_params("parallel", "arbitrary"),
        name="attn_big",
    )(*lams, gain_col, pos, pos_meta, q_t, kb, v_t, k_meta, vm_t)


def _attn_meta_body(slope_ref, lq1_ref, lk1_ref, lq2_ref, lk2_ref, gain_ref,
                    q_ref, k_ref, v_ref, o_ref, *, lam_init):
    h = pl.program_id(0)
    slope = slope_ref[h]
    lam = _lambda(lq1_ref, lk1_ref, lq2_ref, lk2_ref, lam_init)
    qs = _split_halves(q_ref[...], 1)
    kb = k_ref[...]
    shape = (N_META, kb.shape[0])
    qpos = lax.broadcasted_iota(jnp.int32, shape, 0)
    kpos = lax.broadcasted_iota(jnp.int32, shape, 1)
    dist = (qpos - kpos).astype(F32)
    mask = (qpos >= kpos) & (kpos < N_META)
    outs = []
    for mi in range(2):
        s = jnp.where(mask, _nt_dot(qs[mi], kb) - slope * dist, NEG_INF)
        p = jnp.exp(s - jnp.max(s, axis=-1, keepdims=True))
        p = p / jnp.sum(p, axis=-1, keepdims=True)
        outs.append(p)
    a = (outs[0] - lam * outs[1]).astype(BF16)
    o = jnp.dot(a, v_ref[...], preferred_element_type=F32)
    o_ref[...] = (_rms(o, gain_ref[...]) * (1.0 - lam_init)).astype(BF16)


def _attn_meta(q_s, kb_s, vb_s, slopes, lams, gain, *, lam_init):
    m, c = kb_s.shape
    vec = lambda n: pl.BlockSpec((1, n), lambda h: (0, 0))
    return pl.pallas_call(
        functools.partial(_attn_meta_body, lam_init=lam_init),
        grid=(N_HEADS,),
        in_specs=[
            pl.BlockSpec(memory_space=pltpu.SMEM),
            vec(QK_HALF), vec(QK_HALF), vec(QK_HALF), vec(QK_HALF), vec(HEAD_DIM),
            pl.BlockSpec((N_META, HEAD_DIM), lambda h: (0, h)),
            pl.BlockSpec((m, HEAD_DIM), lambda h: (0, h)),
            pl.BlockSpec((m, HEAD_DIM), lambda h: (0, h)),
        ],
        out_specs=pl.BlockSpec((N_META, HEAD_DIM), lambda h: (0, h)),
        out_shape=jax.ShapeDtypeStruct((N_META, c), BF16),
        compiler_params=_params("parallel"),
        name="attn_meta",
    )(slopes, *lams, gain, q_s, kb_s, vb_s)


def _attn_sample_body(pt_ref, lq1_ref, lk1_ref, lq2_ref, lk2_ref, gain_ref,
                      q_ref, kn_ref, vn_ref, *rest, lam_init, t_dec, past_len):
    n_pg = PAGES_PER_STEP
    k_refs, v_refs = rest[:n_pg], rest[n_pg:2 * n_pg]
    o_ref, qm_ref, m_ref, l_ref, acc_ref, bias_ref, p_ref, al_ref = rest[2 * n_pg:]
    pg = pl.program_id(1)
    n_steps = pl.num_programs(1) - 1
    page = k_refs[0].shape[0]
    grp = 2 * t_dec
    rows = grp * N_HEADS
    cols = page * N_HEADS
    row1 = lax.broadcasted_iota(jnp.int32, (rows, 1), 0)
    r_q = row1 % t_dec
    slope = jnp.exp2(-((row1 // grp) + 1).astype(F32))

    def update(kb, vb, bias, shift):
        s = _nt_dot(qm_ref[...], kb) + bias
        m_prev = m_ref[...]
        m_new = jnp.maximum(m_prev, jnp.max(s, axis=-1, keepdims=True) + shift)
        alpha = jnp.exp(m_prev - m_new)
        p = jnp.exp(s - (m_new - shift))
        l_ref[...] = alpha * l_ref[...] + jnp.sum(p, axis=-1, keepdims=True)
        acc_ref[...] = alpha * acc_ref[...] + jnp.dot(p.astype(BF16), vb, preferred_element_type=F32)
        m_ref[...] = m_new

    @pl.when(pg == 0)
    def _():
        q = q_ref[0]
        r2 = lax.broadcasted_iota(jnp.int32, q.shape, 0)
        c2 = lax.broadcasted_iota(jnp.int32, q.shape, 1)
        qm_ref[...] = jnp.where((c2 // QK_HALF) == (r2 % grp) // t_dec, q, jnp.zeros_like(q))
        m_ref[...] = jnp.full(m_ref.shape, NEG_INF, F32)
        l_ref[...] = jnp.zeros(l_ref.shape, F32)
        acc_ref[...] = jnp.zeros(acc_ref.shape, F32)
        p_ref[...] = jnp.zeros(p_ref.shape, BF16)
        al_ref[...] = jnp.ones(al_ref.shape, F32)
        rw = lax.broadcasted_iota(jnp.int32, (rows, cols), 0)
        cl = lax.broadcasted_iota(jnp.int32, (rows, cols), 1)
        bias_ref[...] = jnp.where((cl % N_HEADS) == (rw // grp),
                                  slope * (cl // N_HEADS).astype(F32), NEG_INF)
        rw = lax.broadcasted_iota(jnp.int32, (rows, t_dec * N_HEADS), 0)
        cl = lax.broadcasted_iota(jnp.int32, (rows, t_dec * N_HEADS), 1)
        r_k = cl // N_HEADS
        ok = ((cl % N_HEADS) == (rw // grp)) & (r_k <= rw % t_dec)
        bias_new = jnp.where(ok, -slope * (rw % t_dec - r_k).astype(F32), NEG_INF)
        update(kn_ref[0], vn_ref[0], bias_new, 0.0)

    n_grp = n_pg // SAMPLE_GROUP

    def values():
        acc = acc_ref[...]
        for g in range(n_grp):
            acc = al_ref[g] * acc
            for gi in range(g * SAMPLE_GROUP, (g + 1) * SAMPLE_GROUP):
                vb = v_refs[gi][...].reshape(cols, HEAD_DIM).astype(BF16)
                acc = acc + jnp.dot(p_ref[:, gi * cols:(gi + 1) * cols], vb, preferred_element_type=F32)
        acc_ref[...] = acc

    @pl.when(pg < n_steps)
    def _():
        values()
        qm = qm_ref[...]
        m_run, l_run = m_ref[...], l_ref[...]
        for g in range(n_grp):
            s_pg, top = [], None
            for gi in range(g * SAMPLE_GROUP, (g + 1) * SAMPLE_GROUP):
                start = (pg * n_pg + gi) * page
                shift = slope * (start - past_len - r_q).astype(F32)
                kb = k_refs[gi][...].reshape(cols, HEAD_DIM).astype(BF16)
                s = _nt_dot(qm, kb) + bias_ref[...]
                s_pg.append((gi, s, shift))
                top_g = jnp.max(s, axis=-1, keepdims=True) + shift
                top = top_g if top is None else jnp.maximum(top, top_g)
            m_new = jnp.maximum(m_run, top)
            alpha = jnp.exp(m_run - m_new)
            l_run = alpha * l_run
            for gi, s, shift in s_pg:
                p = jnp.exp(s + (shift - m_new))
                l_run = l_run + jnp.sum(p, axis=-1, keepdims=True)
                p_ref[:, gi * cols:(gi + 1) * cols] = p.astype(BF16)
            al_ref[g] = alpha
            m_run = m_new
        m_ref[...] = m_run
        l_ref[...] = l_run

    @pl.when(pg == n_steps)
    def _():
        values()
        lam = _lambda(lq1_ref, lk1_ref, lq2_ref, lk2_ref, lam_init)
        for hd in range(N_HEADS):
            rs = slice(hd * grp, (hd + 1) * grp)
            n = acc_ref[rs, :] / l_ref[rs, :]
            o = n - lam * pltpu.roll(n, t_dec, 0)
            o_ref[0, rs, :] = _rms(o, gain_ref[...]) * (1.0 - lam_init)


def _attn_sample(page_table, q_rows, k_new, v_new, cache_k, cache_v, lams, gain, *, lam_init, t_dec):
    n_seq, n_pages = page_table.shape
    _, _, page, n_heads, hd = cache_k.shape
    n_pg = PAGES_PER_STEP
    assert n_pages % n_pg == 0 and n_heads == N_HEADS and hd == HEAD_DIM and 2 * t_dec == 8
    rows = 2 * t_dec * N_HEADS
    vec = lambda n: pl.BlockSpec((1, n), lambda s, p, pt: (0, 0))

    n_steps = n_pages // n_pg

    def page_spec(gi, lag):
        def index(s, p, pt):
            grp = jnp.maximum(p - 1, 0) if lag else jnp.minimum(p, n_steps - 1)
            return (0, pt[s, grp * n_pg + gi], 0, 0, 0)
        return pl.BlockSpec((None, None, page, n_heads, hd), index)

    grid_spec = pltpu.PrefetchScalarGridSpec(
        num_scalar_prefetch=1,
        grid=(n_seq, n_steps + 1),
        in_specs=[
            vec(QK_HALF), vec(QK_HALF), vec(QK_HALF), vec(QK_HALF), vec(HEAD_DIM),
            pl.BlockSpec((1, rows, hd), lambda s, p, pt: (s, 0, 0)),
            pl.BlockSpec((1, t_dec * n_heads, hd), lambda s, p, pt: (s, 0, 0)),
            pl.BlockSpec((1, t_dec * n_heads, hd), lambda s, p, pt: (s, 0, 0)),
        ] + [page_spec(gi, False) for gi in range(n_pg)] + [page_spec(gi, True) for gi in range(n_pg)],
        out_specs=pl.BlockSpec((1, rows, hd), lambda s, p, pt: (s, 0, 0)),
        scratch_shapes=[
            pltpu.VMEM((rows, hd), BF16),
            pltpu.VMEM((rows, 1), F32),
            pltpu.VMEM((rows, 1), F32),
            pltpu.VMEM((rows, hd), F32),
            pltpu.VMEM((rows, page * n_heads), F32),
            pltpu.VMEM((rows, n_pg * page * n_heads), BF16),
            pltpu.VMEM((n_pg // SAMPLE_GROUP, rows, 1), F32),
        ],
    )
    return pl.pallas_call(
        functools.partial(_attn_sample_body, lam_init=lam_init, t_dec=t_dec, past_len=n_pages * page),
        grid_spec=grid_spec,
        out_shape=jax.ShapeDtypeStruct((n_seq, rows, hd), F32),
        compiler_params=_params("parallel", "arbitrary"),
        name="attn_sample",
    )(page_table, *lams, gain, q_rows, k_new, v_new,
      *([cache_k] * n_pg), *([cache_v] * n_pg))


def _outproj_body(x_ref, a_ref, b_ref, w_ref, o_ref):
    half = a_ref.shape[1]
    o_ref[...] = (x_ref[...]
                  + jnp.dot(a_ref[...], w_ref[0:half, :], preferred_element_type=F32)
                  + jnp.dot(b_ref[...], w_ref[half:, :], preferred_element_type=F32))


def _outproj(x, pool_out, attn_out, w_out, *, tm):
    m, d = x.shape
    c = pool_out.shape[1]
    return pl.pallas_call(
        _outproj_body,
        grid=(m // tm,),
        in_specs=[
            pl.BlockSpec((tm, d), lambda i: (i, 0)),
            pl.BlockSpec((tm, c), lambda i: (i, 0)),
            pl.BlockSpec((tm, c), lambda i: (i, 0)),
            pl.BlockSpec(w_out.shape, lambda i: (0, 0)),
        ],
        out_specs=pl.BlockSpec((tm, d), lambda i: (i, 0)),
        out_shape=jax.ShapeDtypeStruct((m, d), F32),
        compiler_params=_params("parallel"),
        name="outproj",
    )(x, pool_out, attn_out, w_out)


def _outproj_pool_body(x_ref, p_ref, halo_ref, pm_ref, wp_ref, sc_ref, b_ref, w_ref, o_ref,
                       buf_ref, pool_ref, *, per_seq):
    i = pl.program_id(0)
    rows = p_ref.shape[0]
    first = (i % per_seq) == 0

    @pl.when(first)
    def _():
        buf_ref[0:HALO, :] = pm_ref[...]

    @pl.when(jnp.logical_not(first))
    def _():
        buf_ref[0:HALO, :] = halo_ref[...]

    buf_ref[HALO:HALO + rows, :] = p_ref[...]

    def write(cs, val):
        pool_ref[:, cs] = val.astype(BF16)

    _pool_groups(buf_ref, rows, wp_ref, sc_ref, lambda w: float(w), write)
    half = pool_ref.shape[1]
    o_ref[...] = (x_ref[...]
                  + jnp.dot(pool_ref[...], w_ref[0:half, :], preferred_element_type=F32)
                  + jnp.dot(b_ref[...], w_ref[half:, :], preferred_element_type=F32))


def _outproj_pool(x, p, p_meta, w_pool, pool_scale, attn_out, w_out, *, tm, seq):
    m, d = x.shape
    c = p.shape[1]
    hb = tm // HALO
    const2 = lambda i: (0, 0)
    return pl.pallas_call(
        functools.partial(_outproj_pool_body, per_seq=seq // tm),
        grid=(m // tm,),
        in_specs=[
            pl.BlockSpec((tm, d), lambda i: (i, 0)),
            pl.BlockSpec((tm, c), lambda i: (i, 0)),
            pl.BlockSpec((HALO, c), lambda i: (jnp.maximum(i * hb - 1, 0), 0)),
            pl.BlockSpec((HALO, c), const2),
            pl.BlockSpec(w_pool.shape, lambda i: (0, 0, 0)),
            pl.BlockSpec((1, c), const2),
            pl.BlockSpec((tm, c), lambda i: (i, 0)),
            pl.BlockSpec(w_out.shape, const2),
        ],
        out_specs=pl.BlockSpec((tm, d), lambda i: (i, 0)),
        out_shape=jax.ShapeDtypeStruct((m, d), F32),
        scratch_shapes=[pltpu.VMEM((HALO + tm, c), F32), pltpu.VMEM((tm, c), BF16)],
        compiler_params=_params("arbitrary"),
        name="outproj_pool",
    )(x, p, p, p_meta, w_pool, pool_scale, attn_out, w_out)


def kernel(x_prompt, x_sample, cache_k, cache_v, state_pool, page_table, meta_tokens,
           norm_ffn1, w_gate1, w_up1, w_down1, norm_mix, w_in, w_pool, pool_scale,
           lambda_q1, lambda_k1, lambda_q2, lambda_k2, subln_gain, w_out,
           norm_ffn2, w_gate2, w_up2, w_down2, norm_final):
    depth = w_in.shape[0]
    assert depth == 1
    bsz, seq, d = x_prompt.shape
    n_seq, t_dec, _ = x_sample.shape
    c = N_HEADS * HEAD_DIM
    n_small = N_META + n_seq * t_dec
    assert n_small <= SMALL_ROWS and 2 * t_dec == 8
    lam_init = 0.8 - 0.6 * math.exp(-0.3 * 0)

    row = lambda a: a.reshape(1, -1)
    w_in_b, w_out_b, w_pool_b = w_in[0].astype(BF16), w_out[0].astype(BF16), w_pool[0].astype(BF16)
    g1, gm, g2, gf = row(norm_ffn1[0]), row(norm_mix[0]), row(norm_ffn2[0]), row(norm_final)
    pscale, gain = row(pool_scale[0]), row(subln_gain[0])
    lams = (row(lambda_q1[0]), row(lambda_k1[0]), row(lambda_q2[0]), row(lambda_k2[0]))
    slopes = jnp.exp2(-jnp.arange(1, N_HEADS + 1, dtype=F32))

    x_big = x_prompt.reshape(bsz * seq, d)
    x_small = jnp.concatenate([meta_tokens, x_sample.reshape(n_seq * t_dec, d),
                               jnp.zeros((SMALL_ROWS - n_small, d), F32)], axis=0)

    xb1, xs1 = _ffn(x_big, x_small, g1, w_gate1[0], w_up1[0], w_down1[0], gf, final_norm=False)

    ps, ks, vs, kbs, qs, vbs = _proj(xs1, gm, w_in_b, tm=SMALL_ROWS, transposed=False)
    state16 = jnp.pad(state_pool[0], ((0, 0), (HALO - POOL_BUF, 0), (0, 0)))
    pool_s = _pool_small(ps, state16, w_pool_b, pscale, n_seq=n_seq, t_dec=t_dec).astype(BF16)
    o_meta = _attn_meta(qs, kbs, vbs, slopes, lams, gain, lam_init=lam_init)
    q_dec = qs[N_META:n_small].reshape(n_seq, t_dec, N_HEADS, HEAD_DIM).transpose(0, 2, 1, 3)
    q_rows = jnp.broadcast_to(q_dec[:, :, None], (n_seq, N_HEADS, 2, t_dec, HEAD_DIM))
    q_rows = q_rows.reshape(n_seq, 2 * t_dec * N_HEADS, HEAD_DIM)
    k_new = kbs[N_META:n_small].reshape(n_seq, t_dec * N_HEADS, HEAD_DIM)
    v_new = vbs[N_META:n_small].reshape(n_seq, t_dec * N_HEADS, HEAD_DIM)
    o_dec = _attn_sample(page_table, q_rows, k_new, v_new, cache_k, cache_v, lams, gain,
                         lam_init=lam_init, t_dec=t_dec)
    o_dec = o_dec.reshape(n_seq, N_HEADS, 2, t_dec, HEAD_DIM)[:, :, 0].transpose(0, 2, 1, 3)
    o_small = jnp.concatenate([o_meta, o_dec.reshape(n_seq * t_dec, c).astype(BF16),
                               jnp.zeros((SMALL_ROWS - n_small, c), BF16)], axis=0)
    xs2 = _outproj(xs1, pool_s, o_small, w_out_b, tm=SMALL_ROWS)

    pb, kb, vb, kbb, q_t, v_t = _proj(xb1, gm, w_in_b, tm=512, transposed=True, seq=seq)
    vm_t =vbs[:N_META].reshape(N_META, N_HEADS, HEAD_DIM).transpose(1, 2, 0)
    ones_rows = jnp.zeros((N_HEADS, V_ROWS - HEAD_DIM, N_META), BF16).at[:, 0].set(1.0)
    vm_t = jnp.concatenate([vm_t, ones_rows], axis=1)
    o_big = _attn_big(q_t, kbb, v_t, kbs[:N_META], vm_t, slopes, lams, gain.reshape(HEAD_DIM, 1),
                      lam_init=lam_init, bsz=bsz)
    xb2 = _outproj_pool(xb1, pb, ps[:N_META], w_pool_b, pscale, o_big, w_out_b, tm=512, seq=seq)
    yb, ys = _ffn(xb2, xs2, g2, w_gate2[0], w_up2[0], w_down2[0], gf, final_norm=True)

    k_rows, v_rows = _fill_meta(kb, vb, ks[:N_META], vs[:N_META], bsz=bsz)
    prompt_rows = lambda a: a.reshape(1, bsz, N_META + seq, N_HEADS, HEAD_DIM)
    dec = lambda a: a[N_META:n_small].reshape(1, n_seq, t_dec, N_HEADS, HEAD_DIM)
    y_prompt = yb.reshape(bsz, seq, d)
    y_sample = ys[N_META:n_small].reshape(n_seq, t_dec, d)
    pool_prompt = pb.reshape(bsz, seq, c)[:, seq - POOL_BUF:][None]
    pool_sample = jnp.concatenate([state_pool[0][:, t_dec:], ps[N_META:n_small].reshape(n_seq, t_dec, c)],
                                  axis=1)[None]
    return (y_prompt, y_sample, prompt_rows(k_rows), prompt_rows(v_rows), pool_prompt,
            dec(ks), dec(vs), pool_sample)
```

```python
import functools
import math

import jax
import jax.numpy as jnp
from jax import lax
from jax.experimental import pallas as pl
from jax.experimental.pallas import tpu as pltpu

F32 = jnp.float32
BF16 = jnp.bfloat16

N_META = 16
POOL_WINDOWS = (2, 4, 8, 16)
POOL_BUF = max(POOL_WINDOWS) - 1
HALO = 16
N_HEADS = 8
HEAD_DIM = 128
QK_HALF = HEAD_DIM // 2
EPS = 1e-6
NEG_INF = -1e30
SMALL_ROWS = 128
ATTN_BLOCK = 256
V_ROWS = HEAD_DIM + 16
Q_GROUP = 4
KEY_SUB = 128
FFN_ROWS = 1024
FFN_CHUNK = 128
FFN_COLS = 256
SAMPLE_GROUP = 2
PAGES_PER_STEP = 16

VMEM_LIMIT = 60 * 1024 * 1024


def _params(*sem):
    return pltpu.CompilerParams(dimension_semantics=sem, vmem_limit_bytes=VMEM_LIMIT)


def _rms(x, g):
    return x * lax.rsqrt(jnp.mean(x * x, axis=-1, keepdims=True) + EPS) * g


def _ffn_body(xb_ref, xs_ref, g_ref, wg_ref, wu_ref, wd_ref, gf_ref, ob_ref, os_ref,
              hb_ref, hs_ref, wgb_ref, wub_ref, wdb_ref, *, final_norm):
    i, s = pl.program_id(0), pl.program_id(1)
    n_blk = pl.num_programs(1) - 1

    def round_weights():
        slot = s % 2
        wgb_ref[slot] = wg_ref[...].astype(BF16)
        wub_ref[slot] = wu_ref[...].astype(BF16)
        wdb_ref[slot] = wd_ref[...].astype(BF16)

    def row_chunks(ref):
        n = ref.shape[0]
        step = min(n, FFN_CHUNK)
        return [slice(r, r + step) for r in range(0, n, step)]

    def start(x_ref, o_ref, h_ref):
        for rs in row_chunks(x_ref):
            h_ref[rs, :] = _rms(x_ref[rs, :], g_ref[...]).astype(BF16)
            o_ref[rs, :] = jnp.zeros((rs.stop - rs.start, o_ref.shape[1]), F32)

    def accumulate(o_ref, h_ref):
        slot = (s + 1) % 2
        h = h_ref[...]
        g = jnp.dot(h, wgb_ref[slot], preferred_element_type=F32)
        u = jnp.dot(h, wub_ref[slot], preferred_element_type=F32)
        a = (g * jax.nn.sigmoid(g) * u).astype(BF16)
        o_ref[...] += jnp.dot(a, wdb_ref[slot], preferred_element_type=F32)

    def finish(x_ref, o_ref):
        for rs in row_chunks(x_ref):
            y = x_ref[rs, :] + 0.5 * o_ref[rs, :]
            if final_norm:
                y = _rms(y, gf_ref[...])
            o_ref[rs, :] = y

    last_tile = i == pl.num_programs(0) - 1

    @pl.when(s == 0)
    def _():
        round_weights()
        start(xb_ref, ob_ref, hb_ref)

        @pl.when(last_tile)
        def _():
            start(xs_ref, os_ref, hs_ref)

    @pl.when(s > 0)
    def _():
        accumulate(ob_ref, hb_ref)
        round_weights()

        @pl.when(last_tile)
        def _():
            accumulate(os_ref, hs_ref)

    @pl.when(s == n_blk)
    def _():
        finish(xb_ref, ob_ref)

        @pl.when(last_tile)
        def _():
            finish(xs_ref, os_ref)


def _ffn(x_big, x_small, g, wg, wu, wd, gf, *, final_norm):
    m, d = x_big.shape
    ms = x_small.shape[0]
    f = wg.shape[1]
    tm, tf = FFN_ROWS, FFN_COLS
    assert m % tm == 0 and f % tf == 0
    n_blk = f // tf
    const = lambda i, s: (0, 0)
    blk = lambda s: jnp.minimum(s, n_blk - 1)
    return pl.pallas_call(
        functools.partial(_ffn_body, final_norm=final_norm),
        grid=(m // tm, n_blk + 1),
        in_specs=[
            pl.BlockSpec((tm, d), lambda i, s: (i, 0), pipeline_mode=pl.Buffered(1)),
            pl.BlockSpec((ms, d), const),
            pl.BlockSpec((1, d), const),
            pl.BlockSpec((d, tf), lambda i, s: (0, blk(s))),
            pl.BlockSpec((d, tf), lambda i, s: (0, blk(s))),
            pl.BlockSpec((tf, d), lambda i, s: (blk(s), 0)),
            pl.BlockSpec((1, d), const),
        ],
        out_specs=[pl.BlockSpec((tm, d), lambda i, s: (i, 0)), pl.BlockSpec((ms, d), const)],
        out_shape=[jax.ShapeDtypeStruct((m, d), F32), jax.ShapeDtypeStruct((ms, d), F32)],
        scratch_shapes=[
            pltpu.VMEM((tm, d), BF16), pltpu.VMEM((ms, d), BF16),
            pltpu.VMEM((2, d, tf), BF16), pltpu.VMEM((2, d, tf), BF16), pltpu.VMEM((2, tf, d), BF16),
        ],
        compiler_params=_params("arbitrary", "arbitrary"),
        name="ffn",
    )(x_big, x_small, g, wg, wu, wd, gf)


def _proj_body(x_ref, g_ref, w_ref, p_ref, k_ref, v_ref, kb_ref, a_ref, b_ref, *, transposed):
    h = _rms(x_ref[...], g_ref[...]).astype(BF16)
    w = p_ref.shape[1]

    def block(j):
        return jnp.dot(h, w_ref[:, j * w:(j + 1) * w], preferred_element_type=F32)

    def put_t(dst, val, extra_ones):
        tc = dst.shape[3]
        for hd in range(N_HEADS):
            for ci in range(dst.shape[1]):
                blk = val[ci * tc:(ci + 1) * tc, hd * HEAD_DIM:(hd + 1) * HEAD_DIM]
                dst[hd, ci, 0:HEAD_DIM, :] = blk.T.astype(BF16)
                if extra_ones:
                    rid = lax.broadcasted_iota(jnp.int32, (dst.shape[2] - HEAD_DIM, tc), 0)
                    dst[hd, ci, HEAD_DIM:, :] = jnp.where(rid == 0, 1.0, 0.0).astype(BF16)

    p_ref[...] = block(0)
    q = block(1) * (QK_HALF ** -0.5)
    if transposed:
        put_t(a_ref, q, False)
    else:
        a_ref[...] = q.astype(BF16)
    k = block(2)
    k_ref[...] = k
    kb_ref[...] = k.astype(BF16)
    v = block(3)
    v_ref[...] = v
    if transposed:
        put_t(b_ref, v, True)
    else:
        b_ref[...] = v.astype(BF16)


def _proj(x, g, w_in, *, tm, transposed, seq=None):
    m, d = x.shape
    w = w_in.shape[1] // 4
    row = lambda i: (i, 0)
    rm_spec = pl.BlockSpec((tm, w), row)
    rm_shape = jax.ShapeDtypeStruct((m, w), BF16)
    f32_shape = jax.ShapeDtypeStruct((m, w), F32)
    kv_spec, kv_shape = rm_spec, f32_shape
    if transposed:
        tc = ATTN_BLOCK
        t_spec = lambda n: pl.BlockSpec((N_HEADS, tm // tc, n, tc), lambda i: (0, i, 0, 0))
        t_shape = lambda n: jax.ShapeDtypeStruct((N_HEADS, m // tc, n, tc), BF16)
        extra_specs = [rm_spec, t_spec(HEAD_DIM), t_spec(V_ROWS)]
        extra_shapes = [rm_shape, t_shape(HEAD_DIM), t_shape(V_ROWS)]
        per_seq = seq // tm
        kv_spec = pl.BlockSpec(
            (pl.Element(tm), pl.Element(w)),
            lambda i: (pl.multiple_of(
                (i // per_seq) * (N_META + seq) + N_META + (i % per_seq) * tm, N_META), 0))
        kv_shape = jax.ShapeDtypeStruct((m // seq * (N_META + seq), w), F32)
    else:
        extra_specs = [rm_spec] * 3
        extra_shapes = [rm_shape] * 3
    return pl.pallas_call(
        functools.partial(_proj_body, transposed=transposed),
        grid=(m // tm,),
        in_specs=[
            pl.BlockSpec((tm, d), row),
            pl.BlockSpec((1, d), lambda i: (0, 0)),
            pl.BlockSpec(w_in.shape, lambda i: (0, 0), pipeline_mode=pl.Buffered(1)),
        ],
        out_specs=[rm_spec, kv_spec, kv_spec] + extra_specs,
        out_shape=[f32_shape, kv_shape, kv_shape] + extra_shapes,
        compiler_params=_params("parallel"),
        name="proj",
    )(x, g, w_in)


def _fill_meta_body(km_ref, vm_ref, k_in_ref, v_in_ref, k_ref, v_ref):
    del k_in_ref, v_in_ref
    k_ref[...] = km_ref[...]
    v_ref[...] = vm_ref[...]


def _fill_meta(k_rows, v_rows, k_meta, v_meta, *, bsz):
    m, w = k_rows.shape
    per_seq = m // bsz
    assert per_seq % N_META == 0
    meta_spec = pl.BlockSpec((N_META, w), lambda b: (0, 0))
    out_spec = pl.BlockSpec((N_META, w), lambda b: (b * (per_seq // N_META), 0))
    shape = jax.ShapeDtypeStruct((m, w), k_rows.dtype)
    return pl.pallas_call(
        _fill_meta_body,
        grid=(bsz,),
        in_specs=[meta_spec, meta_spec, pl.BlockSpec(memory_space=pl.ANY), pl.BlockSpec(memory_space=pl.ANY)],
        out_specs=[out_spec, out_spec],
        out_shape=[shape, shape],
        input_output_aliases={2: 0, 3: 1},
        compiler_params=_params("arbitrary"),
        name="fill_meta",
    )(k_meta, v_meta, k_rows, v_rows)


def _pool_groups(buf_ref, rows, wp_ref, sc_ref, count_fn, write):
    gd = wp_ref.shape[1]
    for gi, w in enumerate(POOL_WINDOWS):
        cs = slice(gi * gd, (gi + 1) * gd)
        cur = buf_ref[HALO:HALO + rows, cs]
        acc = cur
        for dlt in range(1, w):
            acc = acc + buf_ref[HALO - dlt:HALO - dlt + rows, cs]
        feat = acc / count_fn(w) - cur
        out = jnp.dot(feat.astype(BF16), wp_ref[gi], preferred_element_type=F32) * sc_ref[:, cs]
        write(cs, out)


def _pool_small_body(p_ref, st_ref, wp_ref, sc_ref, o_ref, buf_ref, *, n_seq, t_dec):
    buf_ref[0:HALO, :] = jnp.zeros((HALO, buf_ref.shape[1]), F32)
    buf_ref[HALO:2 * HALO, :] = p_ref[0:N_META, :]
    pos1 = lax.broadcasted_iota(jnp.int32, (N_META, 1), 0).astype(F32) + 1.0

    def write_meta(cs, val):
        o_ref[0:N_META, cs] = val

    _pool_groups(buf_ref, N_META, wp_ref, sc_ref, lambda w: jnp.minimum(pos1, float(w)), write_meta)

    o_ref[N_META + n_seq * t_dec:, :] = jnp.zeros(
        (o_ref.shape[0] - N_META - n_seq * t_dec, o_ref.shape[1]), F32)
    for s in range(n_seq):
        r0 = N_META + s * t_dec
        buf_ref[0:HALO, :] = st_ref[s]
        buf_ref[HALO:HALO + 8, :] = p_ref[r0:r0 + 8, :]

        def write_s(cs, val, r0=r0):
            o_ref[r0:r0 + t_dec, cs] = val[0:t_dec]

        _pool_groups(buf_ref, 8, wp_ref, sc_ref, lambda w: float(w), write_s)


def _pool_small(p_small, state16, w_pool, pool_scale, *, n_seq, t_dec):
    m, c = p_small.shape
    return pl.pallas_call(
        functools.partial(_pool_small_body, n_seq=n_seq, t_dec=t_dec),
        out_shape=jax.ShapeDtypeStruct((m, c), F32),
        scratch_shapes=[pltpu.VMEM((2 * HALO, c), F32)],
        compiler_params=pltpu.CompilerParams(vmem_limit_bytes=VMEM_LIMIT),
        name="pool_small",
    )(p_small, state16, w_pool, pool_scale)


def _lambda(lq1_ref, lk1_ref, lq2_ref, lk2_ref, lam_init):
    a = jnp.sum(lq1_ref[...] * lk1_ref[...], axis=-1, keepdims=True)
    b = jnp.sum(lq2_ref[...] * lk2_ref[...], axis=-1, keepdims=True)
    return jnp.exp(a) - jnp.exp(b) + lam_init


def _split_halves(q, axis):
    idx = lax.broadcasted_iota(jnp.int32, q.shape, axis) % HEAD_DIM
    zero = jnp.zeros_like(q)
    return jnp.where(idx < QK_HALF, q, zero), jnp.where(idx >= QK_HALF, q, zero)


def _nt_dot(a, b):
    return lax.dot_general(a, b, (((1,), (1,)), ((), ())), preferred_element_type=F32)


def _attn_big_body(lq1_ref, lk1_ref, lq2_ref, lk2_ref, gain_ref, pos_ref, posm_ref,
                   qt_ref, k_ref, vt_ref, km_ref, vmt_ref, o_ref,
                   m_ref, acc_ref, p_ref, al_ref, *, lam_init):
    blk = ATTN_BLOCK
    n_blk = qt_ref.shape[1]
    lam = _lambda(lq1_ref, lk1_ref, lq2_ref, lk2_ref, lam_init)
    diag = (lax.broadcasted_iota(jnp.int32, (blk, blk), 0)
            <= lax.broadcasted_iota(jnp.int32, (blk, blk), 1))
    nq = Q_GROUP

    def q_group(g, carry):
        q_ext = []
        for qh in range(nq):
            q = qt_ref[0, nq * g + qh].astype(F32)
            rid = lax.broadcasted_iota(jnp.int32, q.shape, 0)
            ones = jnp.where(rid < 2, 1.0, 0.0).astype(BF16)
            q_ext.append(tuple(
                jnp.concatenate([jnp.where(keep, q, 0.0).astype(BF16), ones], axis=0)
                for keep in (rid < QK_HALF, rid >= QK_HALF)))
        m_ref[...] = jnp.full(m_ref.shape, NEG_INF, F32)
        acc_ref[...] = jnp.zeros(acc_ref.shape, F32)
        p_ref[1] = jnp.zeros(p_ref.shape[1:], BF16)
        al_ref[1] = jnp.ones(al_ref.shape[1:], F32)

        def scores(slot, kb, kpos, targets):
            n_keys = kb.shape[0]
            sub = min(KEY_SUB, n_keys)
            for ki in range(n_keys // sub):
                rs = slice(ki * sub, (ki + 1) * sub)
                k_ext = jnp.concatenate([kb[rs], kpos[rs]], axis=1)
                for qh, mask in targets:
                    for mi in range(2):
                        st = 2 * qh + mi
                        s = jnp.dot(k_ext, q_ext[qh][mi], preferred_element_type=F32)
                        if mask is not None:
                            s = jnp.where(mask[rs], s, NEG_INF)
                        m_prev = m_ref[st]
                        m_new = jnp.maximum(m_prev, jnp.max(s, axis=0, keepdims=True))
                        al_ref[slot, ki, st] = jnp.exp(m_prev - m_new)
                        p_ref[slot, st, rs, :] = jnp.exp(s - m_new).astype(BF16)
                        m_ref[st] = m_new

        def values(slot, vt, first_q):
            n_keys = vt.shape[1]
            sub = min(KEY_SUB, n_keys)
            for ki in range(n_keys // sub):
                rs = slice(ki * sub, (ki + 1) * sub)
                for st in range(2 * first_q, 2 * nq):
                    acc_ref[st] = al_ref[slot, ki, st] * acc_ref[st] + jnp.dot(
                        vt[:, rs], p_ref[slot, st, rs, :], preferred_element_type=F32)

        def keys(kj):
            k0 = pl.multiple_of(kj * blk, blk)
            return k_ref[pl.ds(k0, blk), :], pos_ref[pl.ds(k0, blk), :]

        everyone = [(qh, None) for qh in range(nq)]
        first_key = nq * g

        def block_pair(t, c2):
            values(1, vt_ref[0, jnp.maximum(2 * t - 1, 0)], 0)
            scores(0, *keys(2 * t), everyone)
            values(0, vt_ref[0, 2 * t], 0)
            scores(1, *keys(2 * t + 1), everyone)
            return c2

        lax.fori_loop(0, first_key // 2, block_pair, 0)
        values(1, vt_ref[0, jnp.maximum(first_key - 1, 0)], 0)
        for j in range(nq):
            scores(j % 2, *keys(first_key + j), [(j, diag)] + everyone[j + 1:])
            values(j % 2, vt_ref[0, first_key + j], j)
        scores(nq % 2, km_ref[...], posm_ref[...], everyone)
        values(nq % 2, vmt_ref[0], 0)

        for qh in range(nq):
            a1, a2 = acc_ref[2 * qh], acc_ref[2 * qh + 1]
            o = (a1[0:HEAD_DIM] / a1[HEAD_DIM:HEAD_DIM + 1]
                 - lam * (a2[0:HEAD_DIM] / a2[HEAD_DIM:HEAD_DIM + 1]))
            o = o * lax.rsqrt(jnp.mean(o * o, axis=0, keepdims=True) + EPS) * gain_ref[...]
            r0 = pl.multiple_of((first_key + qh) * blk, blk)
            o_ref[pl.ds(r0, blk), :] = (o * (1.0 - lam_init)).T.astype(BF16)
        return carry

    assert n_blk % nq == 0 and nq % 2 == 0
    lax.fori_loop(0, n_blk // nq, q_group, 0)


def _position_blocks(slopes, first, n):
    shape = (slopes.shape[0], n, HEAD_DIM)
    pos = lax.broadcasted_iota(jnp.int32, shape, 1) + first
    col = lax.broadcasted_iota(jnp.int32, shape, 2)
    slope = slopes[:, None, None]
    coarse = slope * ((pos // QK_HALF) * QK_HALF).astype(F32)
    fine = slope * (pos % QK_HALF).astype(F32)
    return jnp.where(col == 0, coarse, jnp.where(col == 1, fine, 0.0)).astype(BF16)


def _attn_big(q_t, kb, v_t, k_meta, vm_t, slopes, lams, gain_col, *, lam_init, bsz):
    n_heads, n_chunks, hd, blk = q_t.shape
    m, c = kb.shape
    seq = m // bsz
    n_blk = seq // blk
    vec = lambda n: pl.BlockSpec((1, n), lambda bi, h: (0, 0))
    pos = _position_blocks(slopes, 0, seq)
    pos_meta = _position_blocks(slopes, -N_META, N_META)
    return pl.pallas_call(
        functools.partial(_attn_big_body, lam_init=lam_init),
        grid=(bsz, n_heads),
        in_specs=[
            vec(QK_HALF), vec(QK_HALF), vec(QK_HALF), vec(QK_HALF),
            pl.BlockSpec((hd, 1), lambda bi, h: (0, 0)),
            pl.BlockSpec((None, seq, hd), lambda bi, h: (h, 0, 0)),
            pl.BlockSpec((None, N_META, hd), lambda bi, h: (h, 0, 0)),
            pl.BlockSpec((1, n_blk, hd, blk), lambda bi, h: (h, bi, 0, 0)),
            pl.BlockSpec((seq, hd), lambda bi, h: (bi, h)),
            pl.BlockSpec((1, n_blk, V_ROWS, blk), lambda bi, h: (h, bi, 0, 0)),
            pl.BlockSpec((N_META, hd), lambda bi, h: (0, h)),
            pl.BlockSpec((1, V_ROWS, N_META), lambda bi, h: (h, 0, 0)),
        ],
        out_specs=pl.BlockSpec((seq, hd), lambda bi, h: (bi, h)),
        out_shape=jax.ShapeDtypeStruct((m, c), BF16),
        scratch_shapes=[
            pltpu.VMEM((2 * Q_GROUP, 1, blk), F32),
            pltpu.VMEM((2 * Q_GROUP, V_ROWS, blk), F32),
            pltpu.VMEM((2, 2 * Q_GROUP, blk, blk), BF16),
            pltpu.VMEM((2, blk // KEY_SUB, 2 * Q_GROUP, 1, blk), F32),
        ],
        compiler_params=_params("parallel", "arbitrary"),
        name="attn_big",
    )(*lams, gain_col, pos, pos_meta, q_t, kb, v_t, k_meta, vm_t)


def _attn_meta_body(slope_ref, lq1_ref, lk1_ref, lq2_ref, lk2_ref, gain_ref,
                    q_ref, k_ref, v_ref, o_ref, *, lam_init):
    lam = _lambda(lq1_ref, lk1_ref, lq2_ref, lk2_ref, lam_init)
    shape = (N_META, k_ref.shape[0])
    qpos = lax.broadcasted_iota(jnp.int32, shape, 0)
    kpos = lax.broadcasted_iota(jnp.int32, shape, 1)
    dist = (qpos - kpos).astype(F32)
    mask = (qpos >= kpos) & (kpos < N_META)
    for hd in range(N_HEADS):
        cs = slice(hd * HEAD_DIM, (hd + 1) * HEAD_DIM)
        qs = _split_halves(q_ref[:, cs], 1)
        kb = k_ref[:, cs]
        outs = []
        for mi in range(2):
            s = jnp.where(mask, _nt_dot(qs[mi], kb) - slope_ref[hd] * dist, NEG_INF)
            p = jnp.exp(s - jnp.max(s, axis=-1, keepdims=True))
            p = p / jnp.sum(p, axis=-1, keepdims=True)
            outs.append(p)
        a = (outs[0] - lam * outs[1]).astype(BF16)
        o = jnp.dot(a, v_ref[:, cs], preferred_element_type=F32)
        o_ref[:, cs] = (_rms(o, gain_ref[...]) * (1.0 - lam_init)).astype(BF16)


def _attn_meta(q_s, kb_s, vb_s, slopes, lams, gain, *, lam_init):
    m, c = kb_s.shape
    vec = lambda n: pl.BlockSpec((1, n), lambda i: (0, 0))
    rows = lambda n: pl.BlockSpec((n, c), lambda i: (0, 0))
    return pl.pallas_call(
        functools.partial(_attn_meta_body, lam_init=lam_init),
        grid=(1,),
        in_specs=[
            pl.BlockSpec(memory_space=pltpu.SMEM),
            vec(QK_HALF), vec(QK_HALF), vec(QK_HALF), vec(QK_HALF), vec(HEAD_DIM),
            rows(N_META), rows(m), rows(m),
        ],
        out_specs=rows(N_META),
        out_shape=jax.ShapeDtypeStruct((N_META, c), BF16),
        compiler_params=_params("arbitrary"),
        name="attn_meta",
    )(slopes, *lams, gain, q_s, kb_s, vb_s)


def _attn_sample_body(pt_ref, lq1_ref, lk1_ref, lq2_ref, lk2_ref, gain_ref,
                      q_ref, kn_ref, vn_ref, *rest, lam_init, t_dec, past_len):
    n_pg = PAGES_PER_STEP
    k_refs, v_refs = rest[:n_pg], rest[n_pg:2 * n_pg]
    o_ref, qm_ref, m_ref, l_ref, acc_ref, bias_ref, p_ref, al_ref = rest[2 * n_pg:]
    pg = pl.program_id(1)
    n_steps = pl.num_programs(1) - 1
    page = k_refs[0].shape[0]
    grp = 2 * t_dec
    rows = grp * N_HEADS
    cols = page * N_HEADS
    row1 = lax.broadcasted_iota(jnp.int32, (rows, 1), 0)
    r_q = row1 % t_dec
    slope = jnp.exp2(-((row1 // grp) + 1).astype(F32))

    def update(kb, vb, bias, shift):
        s = _nt_dot(qm_ref[...], kb) + bias
        m_prev = m_ref[...]
        m_new = jnp.maximum(m_prev, jnp.max(s, axis=-1, keepdims=True) + shift)
        alpha = jnp.exp(m_prev - m_new)
        p = jnp.exp(s - (m_new - shift))
        l_ref[...] = alpha * l_ref[...] + jnp.sum(p, axis=-1, keepdims=True)
        acc_ref[...] = alpha * acc_ref[...] + jnp.dot(p.astype(BF16), vb, preferred_element_type=F32)
        m_ref[...] = m_new

    @pl.when(pg == 0)
    def _():
        q = q_ref[0]
        r2 = lax.broadcasted_iota(jnp.int32, q.shape, 0)
        c2 = lax.broadcasted_iota(jnp.int32, q.shape, 1)
        qm_ref[...] = jnp.where((c2 // QK_HALF) == (r2 % grp) // t_dec, q, jnp.zeros_like(q))
        m_ref[...] = jnp.full(m_ref.shape, NEG_INF, F32)
        l_ref[...] = jnp.zeros(l_ref.shape, F32)
        acc_ref[...] = jnp.zeros(acc_ref.shape, F32)
        p_ref[...] = jnp.zeros(p_ref.shape, BF16)
        al_ref[...] = jnp.ones(al_ref.shape, F32)
        rw = lax.broadcasted_iota(jnp.int32, (rows, cols), 0)
        cl = lax.broadcasted_iota(jnp.int32, (rows, cols), 1)
        bias_ref[...] = jnp.where((cl % N_HEADS) == (rw // grp),
                                  slope * (cl // N_HEADS).astype(F32), NEG_INF)
        rw = lax.broadcasted_iota(jnp.int32, (rows, t_dec * N_HEADS), 0)
        cl = lax.broadcasted_iota(jnp.int32, (rows, t_dec * N_HEADS), 1)
        r_k = cl // N_HEADS
        ok = ((cl % N_HEADS) == (rw // grp)) & (r_k <= rw % t_dec)
        bias_new = jnp.where(ok, -slope * (rw % t_dec - r_k).astype(F32), NEG_INF)
        update(kn_ref[0], vn_ref[0], bias_new, 0.0)

    n_grp = n_pg // SAMPLE_GROUP

    def values():
        acc = acc_ref[...]
        for g in range(n_grp):
            acc = al_ref[g] * acc
            for gi in range(g * SAMPLE_GROUP, (g + 1) * SAMPLE_GROUP):
                vb = v_refs[gi][...].reshape(cols, HEAD_DIM).astype(BF16)
                acc = acc + jnp.dot(p_ref[:, gi * cols:(gi + 1) * cols], vb, preferred_element_type=F32)
        acc_ref[...] = acc

    @pl.when(pg < n_steps)
    def _():
        values()
        qm = qm_ref[...]
        m_run, l_run = m_ref[...], l_ref[...]
        for g in range(n_grp):
            s_pg, top = [], None
            for gi in range(g * SAMPLE_GROUP, (g + 1) * SAMPLE_GROUP):
                start = (pg * n_pg + gi) * page
                shift = slope * (start - past_len - r_q).astype(F32)
                kb = k_refs[gi][...].reshape(cols, HEAD_DIM).astype(BF16)
                s = _nt_dot(qm, kb) + bias_ref[...]
                s_pg.append((gi, s, shift))
                top_g = jnp.max(s, axis=-1, keepdims=True) + shift
                top = top_g if top is None else jnp.maximum(top, top_g)
            m_new = jnp.maximum(m_run, top)
            alpha = jnp.exp(m_run - m_new)
            l_run = alpha * l_run
            for gi, s, shift in s_pg:
                p = jnp.exp(s + (shift - m_new))
                l_run = l_run + jnp.sum(p, axis=-1, keepdims=True)
                p_ref[:, gi * cols:(gi + 1) * cols] = p.astype(BF16)
            al_ref[g] = alpha
            m_run = m_new
        m_ref[...] = m_run
        l_ref[...] = l_run

    @pl.when(pg == n_steps)
    def _():
        values()
        lam = _lambda(lq1_ref, lk1_ref, lq2_ref, lk2_ref, lam_init)
        for hd in range(N_HEADS):
            rs = slice(hd * grp, (hd + 1) * grp)
            n = acc_ref[rs, :] / l_ref[rs, :]
            o = n - lam * pltpu.roll(n, t_dec, 0)
            o_ref[0, rs, :] = _rms(o, gain_ref[...]) * (1.0 - lam_init)


def _attn_sample(page_table, q_rows, k_new, v_new, cache_k, cache_v, lams, gain, *, lam_init, t_dec):
    n_seq, n_pages = page_table.shape
    _, _, page, n_heads, hd = cache_k.shape
    n_pg = PAGES_PER_STEP
    assert n_pages % n_pg == 0 and n_heads == N_HEADS and hd == HEAD_DIM and 2 * t_dec == 8
    rows = 2 * t_dec * N_HEADS
    vec = lambda n: pl.BlockSpec((1, n), lambda s, p, pt: (0, 0))

    n_steps = n_pages // n_pg

    def page_spec(gi, lag):
        def index(s, p, pt):
            grp = jnp.maximum(p - 1, 0) if lag else jnp.minimum(p, n_steps - 1)
            return (0, pt[s, grp * n_pg + gi], 0, 0, 0)
        return pl.BlockSpec((None, None, page, n_heads, hd), index)

    grid_spec = pltpu.PrefetchScalarGridSpec(
        num_scalar_prefetch=1,
        grid=(n_seq, n_steps + 1),
        in_specs=[
            vec(QK_HALF), vec(QK_HALF), vec(QK_HALF), vec(QK_HALF), vec(HEAD_DIM),
            pl.BlockSpec((1, rows, hd), lambda s, p, pt: (s, 0, 0)),
            pl.BlockSpec((1, t_dec * n_heads, hd), lambda s, p, pt: (s, 0, 0)),
            pl.BlockSpec((1, t_dec * n_heads, hd), lambda s, p, pt: (s, 0, 0)),
        ] + [page_spec(gi, False) for gi in range(n_pg)] + [page_spec(gi, True) for gi in range(n_pg)],
        out_specs=pl.BlockSpec((1, rows, hd), lambda s, p, pt: (s, 0, 0)),
        scratch_shapes=[
            pltpu.VMEM((rows, hd), BF16),
            pltpu.VMEM((rows, 1), F32),
            pltpu.VMEM((rows, 1), F32),
            pltpu.VMEM((rows, hd), F32),
            pltpu.VMEM((rows, page * n_heads), F32),
            pltpu.VMEM((rows, n_pg * page * n_heads), BF16),
            pltpu.VMEM((n_pg // SAMPLE_GROUP, rows, 1), F32),
        ],
    )
    return pl.pallas_call(
        functools.partial(_attn_sample_body, lam_init=lam_init, t_dec=t_dec, past_len=n_pages * page),
        grid_spec=grid_spec,
        out_shape=jax.ShapeDtypeStruct((n_seq, rows, hd), F32),
        compiler_params=_params("parallel", "arbitrary"),
        name="attn_sample",
    )(page_table, *lams, gain, q_rows, k_new, v_new,
      *([cache_k] * n_pg), *([cache_v] * n_pg))


def _outproj_body(x_ref, a_ref, b_ref, w_ref, o_ref):
    half = a_ref.shape[1]
    o_ref[...] = (x_ref[...]
                  + jnp.dot(a_ref[...], w_ref[0:half, :], preferred_element_type=F32)
                  + jnp.dot(b_ref[...], w_ref[half:, :], preferred_element_type=F32))


def _outproj(x, pool_out, attn_out, w_out, *, tm):
    m, d = x.shape
    c = pool_out.shape[1]
    return pl.pallas_call(
        _outproj_body,
        grid=(m // tm,),
        in_specs=[
            pl.BlockSpec((tm, d), lambda i: (i, 0)),
            pl.BlockSpec((tm, c), lambda i: (i, 0)),
            pl.BlockSpec((tm, c), lambda i: (i, 0)),
            pl.BlockSpec(w_out.shape, lambda i: (0, 0)),
        ],
        out_specs=pl.BlockSpec((tm, d), lambda i: (i, 0)),
        out_shape=jax.ShapeDtypeStruct((m, d), F32),
        compiler_params=_params("parallel"),
        name="outproj",
    )(x, pool_out, attn_out, w_out)


def _outproj_pool_body(x_ref, p_ref, halo_ref, pm_ref, wp_ref, sc_ref, b_ref, w_ref, o_ref,
                       buf_ref, pool_ref, *, per_seq):
    i = pl.program_id(0)
    rows = p_ref.shape[0]
    first = (i % per_seq) == 0

    @pl.when(first)
    def _():
        buf_ref[0:HALO, :] = pm_ref[...]

    @pl.when(jnp.logical_not(first))
    def _():
        buf_ref[0:HALO, :] = halo_ref[...]

    buf_ref[HALO:HALO + rows, :] = p_ref[...]

    def write(cs, val):
        pool_ref[:, cs] = val.astype(BF16)

    _pool_groups(buf_ref, rows, wp_ref, sc_ref, lambda w: float(w), write)
    half = pool_ref.shape[1]
    o_ref[...] = (x_ref[...]
                  + jnp.dot(pool_ref[...], w_ref[0:half, :], preferred_element_type=F32)
                  + jnp.dot(b_ref[...], w_ref[half:, :], preferred_element_type=F32))


def _outproj_pool(x, p, p_meta, w_pool, pool_scale, attn_out, w_out, *, tm, seq):
    m, d = x.shape
    c = p.shape[1]
    hb = tm // HALO
    const2 = lambda i: (0, 0)
    return pl.pallas_call(
        functools.partial(_outproj_pool_body, per_seq=seq // tm),
        grid=(m // tm,),
        in_specs=[
            pl.BlockSpec((tm, d), lambda i: (i, 0)),
            pl.BlockSpec((tm, c), lambda i: (i, 0)),
            pl.BlockSpec((HALO, c), lambda i: (jnp.maximum(i * hb - 1, 0), 0)),
            pl.BlockSpec((HALO, c), const2),
            pl.BlockSpec(w_pool.shape, lambda i: (0, 0, 0)),
            pl.BlockSpec((1, c), const2),
            pl.BlockSpec((tm, c), lambda i: (i, 0)),
            pl.BlockSpec(w_out.shape, const2),
        ],
        out_specs=pl.BlockSpec((tm, d), lambda i: (i, 0)),
        out_shape=jax.ShapeDtypeStruct((m, d), F32),
        scratch_shapes=[pltpu.VMEM((HALO + tm, c), F32), pltpu.VMEM((tm, c), BF16)],
        compiler_params=_params("arbitrary"),
        name="outproj_pool",
    )(x, p, p, p_meta, w_pool, pool_scale, attn_out, w_out)


def kernel(x_prompt, x_sample, cache_k, cache_v, state_pool, page_table, meta_tokens,
           norm_ffn1, w_gate1, w_up1, w_down1, norm_mix, w_in, w_pool, pool_scale,
           lambda_q1, lambda_k1, lambda_q2, lambda_k2, subln_gain, w_out,
           norm_ffn2, w_gate2, w_up2, w_down2, norm_final):
    depth = w_in.shape[0]
    assert depth == 1
    bsz, seq, d = x_prompt.shape
    n_seq, t_dec, _ = x_sample.shape
    c = N_HEADS * HEAD_DIM
    n_small = N_META + n_seq * t_dec
    assert n_small <= SMALL_ROWS and 2 * t_dec == 8
    lam_init = 0.8 - 0.6 * math.exp(-0.3 * 0)

    row = lambda a: a.reshape(1, -1)
    w_in_b, w_out_b, w_pool_b = w_in[0].astype(BF16), w_out[0].astype(BF16), w_pool[0].astype(BF16)
    g1, gm, g2, gf = row(norm_ffn1[0]), row(norm_mix[0]), row(norm_ffn2[0]), row(norm_final)
    pscale, gain = row(pool_scale[0]), row(subln_gain[0])
    lams = (row(lambda_q1[0]), row(lambda_k1[0]), row(lambda_q2[0]), row(lambda_k2[0]))
    slopes = jnp.exp2(-jnp.arange(1, N_HEADS + 1, dtype=F32))

    x_big = x_prompt.reshape(bsz * seq, d)
    x_small = jnp.concatenate([meta_tokens, x_sample.reshape(n_seq * t_dec, d),
                               jnp.zeros((SMALL_ROWS - n_small, d), F32)], axis=0)

    xb1, xs1 = _ffn(x_big, x_small, g1, w_gate1[0], w_up1[0], w_down1[0], gf, final_norm=False)

    ps, ks, vs, kbs, qs, vbs = _proj(xs1, gm, w_in_b, tm=SMALL_ROWS, transposed=False)
    state16 = jnp.pad(state_pool[0], ((0, 0), (HALO - POOL_BUF, 0), (0, 0)))
    pool_s = _pool_small(ps, state16, w_pool_b, pscale, n_seq=n_seq, t_dec=t_dec).astype(BF16)
    o_meta = _attn_meta(qs, kbs, vbs, slopes, lams, gain, lam_init=lam_init)
    q_dec = qs[N_META:n_small].reshape(n_seq, t_dec, N_HEADS, HEAD_DIM).transpose(0, 2, 1, 3)
    q_rows = jnp.broadcast_to(q_dec[:, :, None], (n_seq, N_HEADS, 2, t_dec, HEAD_DIM))
    q_rows = q_rows.reshape(n_seq, 2 * t_dec * N_HEADS, HEAD_DIM)
    k_new = kbs[N_META:n_small].reshape(n_seq, t_dec * N_HEADS, HEAD_DIM)
    v_new = vbs[N_META:n_small].reshape(n_seq, t_dec * N_HEADS, HEAD_DIM)
    o_dec = _attn_sample(page_table, q_rows, k_new, v_new, cache_k, cache_v, lams, gain,
                         lam_init=lam_init, t_dec=t_dec)
    o_dec = o_dec.reshape(n_seq, N_HEADS, 2, t_dec, HEAD_DIM)[:, :, 0].transpose(0, 2, 1, 3)
    o_small = jnp.concatenate([o_meta, o_dec.reshape(n_seq * t_dec, c).astype(BF16),
                               jnp.zeros((SMALL_ROWS - n_small, c), BF16)], axis=0)
    xs2 = _outproj(xs1, pool_s, o_small, w_out_b, tm=SMALL_ROWS)

    pb, kb, vb, kbb, q_t, v_t = _proj(xb1, gm, w_in_b, tm=512, transposed=True, seq=seq)
    vm_t =vbs[:N_META].reshape(N_META, N_HEADS, HEAD_DIM).transpose(1, 2, 0)
    ones_rows = jnp.zeros((N_HEADS, V_ROWS - HEAD_DIM, N_META), BF16).at[:, 0].set(1.0)
    vm_t = jnp.concatenate([vm_t, ones_rows], axis=1)
    o_big = _attn_big(q_t, kbb, v_t, kbs, vm_t, slopes, lams, gain.reshape(HEAD_DIM, 1),
                      lam_init=lam_init, bsz=bsz)
    xb2 = _outproj_pool(xb1, pb, ps, w_pool_b, pscale, o_big, w_out_b, tm=512, seq=seq)
    yb, ys = _ffn(xb2, xs2, g2, w_gate2[0], w_up2[0], w_down2[0], gf, final_norm=True)

    k_rows, v_rows = _fill_meta(kb, vb, ks, vs, bsz=bsz)
    prompt_rows = lambda a: a.reshape(1, bsz, N_META + seq, N_HEADS, HEAD_DIM)
    dec = lambda a: a[N_META:n_small].reshape(1, n_seq, t_dec, N_HEADS, HEAD_DIM)
    y_prompt = yb.reshape(bsz, seq, d)
    y_sample = ys[N_META:n_small].reshape(n_seq, t_dec, d)
    pool_prompt = pb.reshape(bsz, seq, c)[:, seq - POOL_BUF:][None]
    pool_sample = jnp.concatenate([state_pool[0][:, t_dec:], ps[N_META:n_small].reshape(n_seq, t_dec, c)],
                                  axis=1)[None]
    return (y_prompt, y_sample, prompt_rows(k_rows), prompt_rows(v_rows), pool_prompt,
            dec(ks), dec(vs), pool_sample)
```

```python
import functools
import math

import jax
import jax.numpy as jnp
from jax import lax
from jax.experimental import pallas as pl
from jax.experimental.pallas import tpu as pltpu

F32 = jnp.float32
BF16 = jnp.bfloat16

N_META = 16
POOL_WINDOWS = (2, 4, 8, 16)
POOL_BUF = max(POOL_WINDOWS) - 1
HALO = 16
N_HEADS = 8
HEAD_DIM = 128
QK_HALF = HEAD_DIM // 2
EPS = 1e-6
NEG_INF = -1e30
SMALL_ROWS = 128
ATTN_BLOCK = 256
V_ROWS = HEAD_DIM + 16
Q_GROUP = 4
KEY_SUB = 128
FFN_ROWS = 1024
FFN_CHUNK = 128
FFN_COLS = 256
SAMPLE_GROUP = 2
PAGES_PER_STEP = 16

VMEM_LIMIT = 60 * 1024 * 1024


def _params(*sem, vmem_mib=None):
    limit = VMEM_LIMIT if vmem_mib is None else vmem_mib * 1024 * 1024
    return pltpu.CompilerParams(dimension_semantics=sem or None, vmem_limit_bytes=limit)


def _rms(x, g):
    return x * lax.rsqrt(jnp.mean(x * x, axis=-1, keepdims=True) + EPS) * g


def _ffn_body(xb_ref, xs_ref, g_ref, wg_ref, wu_ref, wd_ref, gf_ref, ob_ref, os_ref,
              hb_ref, hs_ref, wgb_ref, wub_ref, wdb_ref, *, final_norm):
    i, s = pl.program_id(0), pl.program_id(1)
    n_blk = pl.num_programs(1) - 1

    def round_weights():
        slot = s % 2
        wgb_ref[slot] = wg_ref[...].astype(BF16)
        wub_ref[slot] = wu_ref[...].astype(BF16)
        wdb_ref[slot] = wd_ref[...].astype(BF16)

    def row_chunks(ref):
        n = ref.shape[0]
        step = min(n, FFN_CHUNK)
        return [slice(r, r + step) for r in range(0, n, step)]

    def start(x_ref, o_ref, h_ref):
        for rs in row_chunks(x_ref):
            h_ref[rs, :] = _rms(x_ref[rs, :], g_ref[...]).astype(BF16)
            o_ref[rs, :] = jnp.zeros((rs.stop - rs.start, o_ref.shape[1]), F32)

    def accumulate(o_ref, h_ref):
        slot = (s + 1) % 2
        h = h_ref[...]
        g = jnp.dot(h, wgb_ref[slot], preferred_element_type=F32)
        u = jnp.dot(h, wub_ref[slot], preferred_element_type=F32)
        a = (g * jax.nn.sigmoid(g) * u).astype(BF16)
        o_ref[...] += jnp.dot(a, wdb_ref[slot], preferred_element_type=F32)

    def finish(x_ref, o_ref):
        for rs in row_chunks(x_ref):
            y = x_ref[rs, :] + 0.5 * o_ref[rs, :]
            if final_norm:
                y = _rms(y, gf_ref[...])
            o_ref[rs, :] = y

    last_tile = i == pl.num_programs(0) - 1

    @pl.when(s == 0)
    def _():
        round_weights()
        start(xb_ref, ob_ref, hb_ref)

        @pl.when(last_tile)
        def _():
            start(xs_ref, os_ref, hs_ref)

    @pl.when(s > 0)
    def _():
        accumulate(ob_ref, hb_ref)
        round_weights()

        @pl.when(last_tile)
        def _():
            accumulate(os_ref, hs_ref)

    @pl.when(s == n_blk)
    def _():
        finish(xb_ref, ob_ref)

        @pl.when(last_tile)
        def _():
            finish(xs_ref, os_ref)


def _ffn(x_big, x_small, g, wg, wu, wd, gf, *, final_norm):
    m, d = x_big.shape
    ms = x_small.shape[0]
    f = wg.shape[1]
    tm, tf = FFN_ROWS, FFN_COLS
    assert m % tm == 0 and f % tf == 0
    n_blk = f // tf
    const = lambda i, s: (0, 0)
    blk = lambda s: jnp.minimum(s, n_blk - 1)
    return pl.pallas_call(
        functools.partial(_ffn_body, final_norm=final_norm),
        grid=(m // tm, n_blk + 1),
        in_specs=[
            pl.BlockSpec((tm, d), lambda i, s: (i, 0), pipeline_mode=pl.Buffered(1)),
            pl.BlockSpec((ms, d), const),
            pl.BlockSpec((1, d), const),
            pl.BlockSpec((d, tf), lambda i, s: (0, blk(s))),
            pl.BlockSpec((d, tf), lambda i, s: (0, blk(s))),
            pl.BlockSpec((tf, d), lambda i, s: (blk(s), 0)),
            pl.BlockSpec((1, d), const),
        ],
        out_specs=[pl.BlockSpec((tm, d), lambda i, s: (i, 0)), pl.BlockSpec((ms, d), const)],
        out_shape=[jax.ShapeDtypeStruct((m, d), F32), jax.ShapeDtypeStruct((ms, d), F32)],
        scratch_shapes=[
            pltpu.VMEM((tm, d), BF16), pltpu.VMEM((ms, d), BF16),
            pltpu.VMEM((2, d, tf), BF16), pltpu.VMEM((2, d, tf), BF16), pltpu.VMEM((2, tf, d), BF16),
        ],
        compiler_params=_params("arbitrary", "arbitrary"),
        name="ffn",
    )(x_big, x_small, g, wg, wu, wd, gf)


def _proj_body(x_ref, g_ref, w_ref, p_ref, k_ref, v_ref, kb_ref, a_ref, b_ref, *, transposed):
    h = _rms(x_ref[...], g_ref[...]).astype(BF16)
    w = p_ref.shape[1]

    def block(j):
        return jnp.dot(h, w_ref[:, j * w:(j + 1) * w], preferred_element_type=F32)

    def put_t(dst, val, extra_ones):
        tc = dst.shape[3]
        for hd in range(N_HEADS):
            for ci in range(dst.shape[1]):
                blk = val[ci * tc:(ci + 1) * tc, hd * HEAD_DIM:(hd + 1) * HEAD_DIM]
                dst[hd, ci, 0:HEAD_DIM, :] = blk.T.astype(BF16)
                if extra_ones:
                    rid = lax.broadcasted_iota(jnp.int32, (dst.shape[2] - HEAD_DIM, tc), 0)
                    dst[hd, ci, HEAD_DIM:, :] = jnp.where(rid == 0, 1.0, 0.0).astype(BF16)

    p_ref[...] = block(0)
    q = block(1) * (QK_HALF ** -0.5)
    if transposed:
        put_t(a_ref, q, False)
    else:
        a_ref[...] = q.astype(BF16)
    k = block(2)
    k_ref[...] = k
    kb_ref[...] = k.astype(BF16)
    v = block(3)
    v_ref[...] = v
    if transposed:
        put_t(b_ref, v, True)
    else:
        b_ref[...] = v.astype(BF16)


def _proj(x, g, w_in, *, tm, transposed, seq=None):
    m, d = x.shape
    w = w_in.shape[1] // 4
    row = lambda i: (i, 0)
    rm_spec = pl.BlockSpec((tm, w), row)
    rm_shape = jax.ShapeDtypeStruct((m, w), BF16)
    f32_shape = jax.ShapeDtypeStruct((m, w), F32)
    kv_spec, kv_shape = rm_spec, f32_shape
    if transposed:
        tc = ATTN_BLOCK
        t_spec = lambda n: pl.BlockSpec((N_HEADS, tm // tc, n, tc), lambda i: (0, i, 0, 0))
        t_shape = lambda n: jax.ShapeDtypeStruct((N_HEADS, m // tc, n, tc), BF16)
        extra_specs = [rm_spec, t_spec(HEAD_DIM), t_spec(V_ROWS)]
        extra_shapes = [rm_shape, t_shape(HEAD_DIM), t_shape(V_ROWS)]
        per_seq = seq // tm
        kv_spec = pl.BlockSpec(
            (pl.Element(tm), pl.Element(w)),
            lambda i: (pl.multiple_of(
                (i // per_seq) * (N_META + seq) + N_META + (i % per_seq) * tm, N_META), 0))
        kv_shape = jax.ShapeDtypeStruct((m // seq * (N_META + seq), w), F32)
    else:
        extra_specs = [rm_spec] * 3
        extra_shapes = [rm_shape] * 3
    return pl.pallas_call(
        functools.partial(_proj_body, transposed=transposed),
        grid=(m // tm,),
        in_specs=[
            pl.BlockSpec((tm, d), row),
            pl.BlockSpec((1, d), lambda i: (0, 0)),
            pl.BlockSpec(w_in.shape, lambda i: (0, 0), pipeline_mode=pl.Buffered(1)),
        ],
        out_specs=[rm_spec, kv_spec, kv_spec] + extra_specs,
        out_shape=[f32_shape, kv_shape, kv_shape] + extra_shapes,
        compiler_params=_params("parallel", vmem_mib=None if transposed else 28),
        name="proj",
    )(x, g, w_in)


def _fill_meta_body(km_ref, vm_ref, k_in_ref, v_in_ref, k_ref, v_ref):
    del k_in_ref, v_in_ref
    k_ref[...] = km_ref[...]
    v_ref[...] = vm_ref[...]


def _fill_meta(k_rows, v_rows, k_meta, v_meta, *, bsz):
    m, w = k_rows.shape
    per_seq = m // bsz
    assert per_seq % N_META == 0
    meta_spec = pl.BlockSpec((N_META, w), lambda b: (0, 0))
    out_spec = pl.BlockSpec((N_META, w), lambda b: (b * (per_seq // N_META), 0))
    shape = jax.ShapeDtypeStruct((m, w), k_rows.dtype)
    return pl.pallas_call(
        _fill_meta_body,
        grid=(bsz,),
        in_specs=[meta_spec, meta_spec, pl.BlockSpec(memory_space=pl.ANY), pl.BlockSpec(memory_space=pl.ANY)],
        out_specs=[out_spec, out_spec],
        out_shape=[shape, shape],
        input_output_aliases={2: 0, 3: 1},
        compiler_params=_params("arbitrary", vmem_mib=8),
        name="fill_meta",
    )(k_meta, v_meta, k_rows, v_rows)


def _pool_groups(buf_ref, rows, wp_ref, sc_ref, count_fn, write):
    gd = wp_ref.shape[1]
    for gi, w in enumerate(POOL_WINDOWS):
        cs = slice(gi * gd, (gi + 1) * gd)
        cur = buf_ref[HALO:HALO + rows, cs]
        acc = cur
        for dlt in range(1, w):
            acc = acc + buf_ref[HALO - dlt:HALO - dlt + rows, cs]
        feat = acc / count_fn(w) - cur
        out = jnp.dot(feat.astype(BF16), wp_ref[gi], preferred_element_type=F32) * sc_ref[:, cs]
        write(cs, out)


def _pool_small_body(p_ref, st_ref, wp_ref, sc_ref, o_ref, buf_ref, *, n_seq, t_dec):
    buf_ref[0:HALO, :] = jnp.zeros((HALO, buf_ref.shape[1]), F32)
    buf_ref[HALO:2 * HALO, :] = p_ref[0:N_META, :]
    pos1 = lax.broadcasted_iota(jnp.int32, (N_META, 1), 0).astype(F32) + 1.0

    def write_meta(cs, val):
        o_ref[0:N_META, cs] = val

    _pool_groups(buf_ref, N_META, wp_ref, sc_ref, lambda w: jnp.minimum(pos1, float(w)), write_meta)

    o_ref[N_META + n_seq * t_dec:, :] = jnp.zeros(
        (o_ref.shape[0] - N_META - n_seq * t_dec, o_ref.shape[1]), F32)
    for s in range(n_seq):
        r0 = N_META + s * t_dec
        buf_ref[0:HALO, :] = st_ref[s]
        buf_ref[HALO:HALO + 8, :] = p_ref[r0:r0 + 8, :]

        def write_s(cs, val, r0=r0):
            o_ref[r0:r0 + t_dec, cs] = val[0:t_dec]

        _pool_groups(buf_ref, 8, wp_ref, sc_ref, lambda w: float(w), write_s)


def _pool_small(p_small, state16, w_pool, pool_scale, *, n_seq, t_dec):
    m, c = p_small.shape
    return pl.pallas_call(
        functools.partial(_pool_small_body, n_seq=n_seq, t_dec=t_dec),
        out_shape=jax.ShapeDtypeStruct((m, c), F32),
        scratch_shapes=[pltpu.VMEM((2 * HALO, c), F32)],
        compiler_params=_params(vmem_mib=8),
        name="pool_small",
    )(p_small, state16, w_pool, pool_scale)


def _lambda(lq1_ref, lk1_ref, lq2_ref, lk2_ref, lam_init):
    a = jnp.sum(lq1_ref[...] * lk1_ref[...], axis=-1, keepdims=True)
    b = jnp.sum(lq2_ref[...] * lk2_ref[...], axis=-1, keepdims=True)
    return jnp.exp(a) - jnp.exp(b) + lam_init


def _split_halves(q, axis):
    idx = lax.broadcasted_iota(jnp.int32, q.shape, axis) % HEAD_DIM
    zero = jnp.zeros_like(q)
    return jnp.where(idx < QK_HALF, q, zero), jnp.where(idx >= QK_HALF, q, zero)


def _nt_dot(a, b):
    return lax.dot_general(a, b, (((1,), (1,)), ((), ())), preferred_element_type=F32)


def _attn_big_body(lq1_ref, lk1_ref, lq2_ref, lk2_ref, gain_ref, pos_ref, posm_ref,
                   qt_ref, k_ref, vt_ref, km_ref, vmt_ref, o_ref,
                   m_ref, acc_ref, p_ref, al_ref, *, lam_init):
    blk = ATTN_BLOCK
    n_blk = qt_ref.shape[1]
    lam = _lambda(lq1_ref, lk1_ref, lq2_ref, lk2_ref, lam_init)
    diag = (lax.broadcasted_iota(jnp.int32, (blk, blk), 0)
            <= lax.broadcasted_iota(jnp.int32, (blk, blk), 1))
    nq = Q_GROUP

    def q_group(g, carry):
        q_ext = []
        for qh in range(nq):
            q = qt_ref[0, nq * g + qh].astype(F32)
            rid = lax.broadcasted_iota(jnp.int32, q.shape, 0)
            ones = jnp.where(rid < 2, 1.0, 0.0).astype(BF16)
            q_ext.append(tuple(
                jnp.concatenate([jnp.where(keep, q, 0.0).astype(BF16), ones], axis=0)
                for keep in (rid < QK_HALF, rid >= QK_HALF)))
        m_ref[...] = jnp.full(m_ref.shape, NEG_INF, F32)
        acc_ref[...] = jnp.zeros(acc_ref.shape, F32)
        p_ref[1] = jnp.zeros(p_ref.shape[1:], BF16)
        al_ref[1] = jnp.ones(al_ref.shape[1:], F32)

        def scores(slot, kb, kpos, targets):
            n_keys = kb.shape[0]
            sub = min(KEY_SUB, n_keys)
            for ki in range(n_keys // sub):
                rs = slice(ki * sub, (ki + 1) * sub)
                k_ext = jnp.concatenate([kb[rs], kpos[rs]], axis=1)
                for qh, mask in targets:
                    for mi in range(2):
                        st = 2 * qh + mi
                        s = jnp.dot(k_ext, q_ext[qh][mi], preferred_element_type=F32)
                        if mask is not None:
                            s = jnp.where(mask[rs], s, NEG_INF)
                        m_prev = m_ref[st]
                        m_new = jnp.maximum(m_prev, jnp.max(s, axis=0, keepdims=True))
                        al_ref[slot, ki, st] = jnp.exp(m_prev - m_new)
                        p_ref[slot, st, rs, :] = jnp.exp(s - m_new).astype(BF16)
                        m_ref[st] = m_new

        def values(slot, vt, first_q):
            n_keys = vt.shape[1]
            sub = min(KEY_SUB, n_keys)
            for ki in range(n_keys // sub):
                rs = slice(ki * sub, (ki + 1) * sub)
                for st in range(2 * first_q, 2 * nq):
                    acc_ref[st] = al_ref[slot, ki, st] * acc_ref[st] + jnp.dot(
                        vt[:, rs], p_ref[slot, st, rs, :], preferred_element_type=F32)

        def keys(kj):
            k0 = pl.multiple_of(kj * blk, blk)
            return k_ref[pl.ds(k0, blk), :], pos_ref[pl.ds(k0, blk), :]

        everyone = [(qh, None) for qh in range(nq)]
        first_key = nq * g

        def block_pair(t, c2):
            values(1, vt_ref[0, jnp.maximum(2 * t - 1, 0)], 0)
            scores(0, *keys(2 * t), everyone)
            values(0, vt_ref[0, 2 * t], 0)
            scores(1, *keys(2 * t + 1), everyone)
            return c2

        lax.fori_loop(0, first_key // 2, block_pair, 0)
        values(1, vt_ref[0, jnp.maximum(first_key - 1, 0)], 0)
        for j in range(nq):
            scores(j % 2, *keys(first_key + j), [(j, diag)] + everyone[j + 1:])
            values(j % 2, vt_ref[0, first_key + j], j)
        scores(nq % 2, km_ref[...], posm_ref[...], everyone)
        values(nq % 2, vmt_ref[0], 0)

        for qh in range(nq):
            a1, a2 = acc_ref[2 * qh], acc_ref[2 * qh + 1]
            o = (a1[0:HEAD_DIM] / a1[HEAD_DIM:HEAD_DIM + 1]
                 - lam * (a2[0:HEAD_DIM] / a2[HEAD_DIM:HEAD_DIM + 1]))
            o = o * lax.rsqrt(jnp.mean(o * o, axis=0, keepdims=True) + EPS) * gain_ref[...]
            r0 = pl.multiple_of((first_key + qh) * blk, blk)
            o_ref[pl.ds(r0, blk), :] = (o * (1.0 - lam_init)).T.astype(BF16)
        return carry

    assert n_blk % nq == 0 and nq % 2 == 0
    lax.fori_loop(0, n_blk // nq, q_group, 0)


def _position_blocks(slopes, first, n):
    shape = (slopes.shape[0], n, HEAD_DIM)
    pos = lax.broadcasted_iota(jnp.int32, shape, 1) + first
    col = lax.broadcasted_iota(jnp.int32, shape, 2)
    slope = slopes[:, None, None]
    coarse = slope * ((pos // QK_HALF) * QK_HALF).astype(F32)
    fine = slope * (pos % QK_HALF).astype(F32)
    return jnp.where(col == 0, coarse, jnp.where(col == 1, fine, 0.0)).astype(BF16)


def _attn_big(q_t, kb, v_t, k_meta, vm_t, slopes, lams, gain_col, *, lam_init, bsz):
    n_heads, n_chunks, hd, blk = q_t.shape
    m, c = kb.shape
    seq = m // bsz
    n_blk = seq // blk
    vec = lambda n: pl.BlockSpec((1, n), lambda bi, h: (0, 0))
    pos = _position_blocks(slopes, 0, seq)
    pos_meta = _position_blocks(slopes, -N_META, N_META)
    return pl.pallas_call(
        functools.partial(_attn_big_body, lam_init=lam_init),
        grid=(bsz, n_heads),
        in_specs=[
            vec(QK_HALF), vec(QK_HALF), vec(QK_HALF), vec(QK_HALF),
            pl.BlockSpec((hd, 1), lambda bi, h: (0, 0)),
            pl.BlockSpec((None, seq, hd), lambda bi, h: (h, 0, 0)),
            pl.BlockSpec((None, N_META, hd), lambda bi, h: (h, 0, 0)),
            pl.BlockSpec((1, n_blk, hd, blk), lambda bi, h: (h, bi, 0, 0)),
            pl.BlockSpec((seq, hd), lambda bi, h: (bi, h)),
            pl.BlockSpec((1, n_blk, V_ROWS, blk), lambda bi, h: (h, bi, 0, 0)),
            pl.BlockSpec((N_META, hd), lambda bi, h: (0, h)),
            pl.BlockSpec((1, V_ROWS, N_META), lambda bi, h: (h, 0, 0)),
        ],
        out_specs=pl.BlockSpec((seq, hd), lambda bi, h: (bi, h)),
        out_shape=jax.ShapeDtypeStruct((m, c), BF16),
        scratch_shapes=[
            pltpu.VMEM((2 * Q_GROUP, 1, blk), F32),
            pltpu.VMEM((2 * Q_GROUP, V_ROWS, blk), F32),
            pltpu.VMEM((2, 2 * Q_GROUP, blk, blk), BF16),
            pltpu.VMEM((2, blk // KEY_SUB, 2 * Q_GROUP, 1, blk), F32),
        ],
        compiler_params=_params("parallel", "arbitrary", vmem_mib=24),
        name="attn_big",
    )(*lams, gain_col, pos, pos_meta, q_t, kb, v_t, k_meta, vm_t)


def _attn_meta_body(slope_ref, lq1_ref, lk1_ref, lq2_ref, lk2_ref, gain_ref,
                    q_ref, k_ref, v_ref, o_ref, *, lam_init):
    lam = _lambda(lq1_ref, lk1_ref, lq2_ref, lk2_ref, lam_init)
    shape = (N_META, k_ref.shape[0])
    qpos = lax.broadcasted_iota(jnp.int32, shape, 0)
    kpos = lax.broadcasted_iota(jnp.int32, shape, 1)
    dist = (qpos - kpos).astype(F32)
    mask = (qpos >= kpos) & (kpos < N_META)
    for hd in range(N_HEADS):
        cs = slice(hd * HEAD_DIM, (hd + 1) * HEAD_DIM)
        qs = _split_halves(q_ref[:, cs], 1)
        kb = k_ref[:, cs]
        outs = []
        for mi in range(2):
            s = jnp.where(mask, _nt_dot(qs[mi], kb) - slope_ref[hd] * dist, NEG_INF)
            p = jnp.exp(s - jnp.max(s, axis=-1, keepdims=True))
            p = p / jnp.sum(p, axis=-1, keepdims=True)
            outs.append(p)
        a = (outs[0] - lam * outs[1]).astype(BF16)
        o = jnp.dot(a, v_ref[:, cs], preferred_element_type=F32)
        o_ref[:, cs] = (_rms(o, gain_ref[...]) * (1.0 - lam_init)).astype(BF16)


def _attn_meta(q_s, kb_s, vb_s, slopes, lams, gain, *, lam_init):
    m, c = kb_s.shape
    vec = lambda n: pl.BlockSpec((1, n), lambda i: (0, 0))
    rows = lambda n: pl.BlockSpec((n, c), lambda i: (0, 0))
    return pl.pallas_call(
        functools.partial(_attn_meta_body, lam_init=lam_init),
        grid=(1,),
        in_specs=[
            pl.BlockSpec(memory_space=pltpu.SMEM),
            vec(QK_HALF), vec(QK_HALF), vec(QK_HALF), vec(QK_HALF), vec(HEAD_DIM),
            rows(N_META), rows(m), rows(m),
        ],
        out_specs=rows(N_META),
        out_shape=jax.ShapeDtypeStruct((N_META, c), BF16),
        compiler_params=_params("arbitrary", vmem_mib=8),
        name="attn_meta",
    )(slopes, *lams, gain, q_s, kb_s, vb_s)


def _attn_sample_body(pt_ref, lq1_ref, lk1_ref, lq2_ref, lk2_ref, gain_ref,
                      q_ref, kn_ref, vn_ref, *rest, lam_init, t_dec, past_len):
    n_pg = PAGES_PER_STEP
    k_refs, v_refs = rest[:n_pg], rest[n_pg:2 * n_pg]
    o_ref, qm_ref, m_ref, l_ref, acc_ref, bias_ref, p_ref, al_ref = rest[2 * n_pg:]
    pg = pl.program_id(1)
    n_steps = pl.num_programs(1) - 1
    page = k_refs[0].shape[0]
    grp = 2 * t_dec
    rows = grp * N_HEADS
    cols = page * N_HEADS
    row1 = lax.broadcasted_iota(jnp.int32, (rows, 1), 0)
    r_q = row1 % t_dec
    slope = jnp.exp2(-((row1 // grp) + 1).astype(F32))

    def update(kb, vb, bias, shift):
        s = _nt_dot(qm_ref[...], kb) + bias
        m_prev = m_ref[...]
        m_new = jnp.maximum(m_prev, jnp.max(s, axis=-1, keepdims=True) + shift)
        alpha = jnp.exp(m_prev - m_new)
        p = jnp.exp(s - (m_new - shift))
        l_ref[...] = alpha * l_ref[...] + jnp.sum(p, axis=-1, keepdims=True)
        acc_ref[...] = alpha * acc_ref[...] + jnp.dot(p.astype(BF16), vb, preferred_element_type=F32)
        m_ref[...] = m_new

    @pl.when(pg == 0)
    def _():
        q = q_ref[0]
        r2 = lax.broadcasted_iota(jnp.int32, q.shape, 0)
        c2 = lax.broadcasted_iota(jnp.int32, q.shape, 1)
        qm_ref[...] = jnp.where((c2 // QK_HALF) == (r2 % grp) // t_dec, q, jnp.zeros_like(q))
        m_ref[...] = jnp.full(m_ref.shape, NEG_INF, F32)
        l_ref[...] = jnp.zeros(l_ref.shape, F32)
        acc_ref[...] = jnp.zeros(acc_ref.shape, F32)
        p_ref[...] = jnp.zeros(p_ref.shape, BF16)
        al_ref[...] = jnp.ones(al_ref.shape, F32)
        rw = lax.broadcasted_iota(jnp.int32, (rows, cols), 0)
        cl = lax.broadcasted_iota(jnp.int32, (rows, cols), 1)
        bias_ref[...] = jnp.where((cl % N_HEADS) == (rw // grp),
                                  slope * (cl // N_HEADS).astype(F32), NEG_INF)
        rw = lax.broadcasted_iota(jnp.int32, (rows, t_dec * N_HEADS), 0)
        cl = lax.broadcasted_iota(jnp.int32, (rows, t_dec * N_HEADS), 1)
        r_k = cl // N_HEADS
        ok = ((cl % N_HEADS) == (rw // grp)) & (r_k <= rw % t_dec)
        bias_new = jnp.where(ok, -slope * (rw % t_dec - r_k).astype(F32), NEG_INF)
        update(kn_ref[0], vn_ref[0], bias_new, 0.0)

    n_grp = n_pg // SAMPLE_GROUP

    def values():
        acc = acc_ref[...]
        for g in range(n_grp):
            acc = al_ref[g] * acc
            for gi in range(g * SAMPLE_GROUP, (g + 1) * SAMPLE_GROUP):
                vb = v_refs[gi][...].reshape(cols, HEAD_DIM).astype(BF16)
                acc = acc + jnp.dot(p_ref[:, gi * cols:(gi + 1) * cols], vb, preferred_element_type=F32)
        acc_ref[...] = acc

    @pl.when(pg < n_steps)
    def _():
        values()
        qm = qm_ref[...]
        m_run, l_run = m_ref[...], l_ref[...]
        for g in range(n_grp):
            s_pg, top = [], None
            for gi in range(g * SAMPLE_GROUP, (g + 1) * SAMPLE_GROUP):
                start = (pg * n_pg + gi) * page
                shift = slope * (start - past_len - r_q).astype(F32)
                kb = k_refs[gi][...].reshape(cols, HEAD_DIM).astype(BF16)
                s = _nt_dot(qm, kb) + bias_ref[...]
                s_pg.append((gi, s, shift))
                top_g = jnp.max(s, axis=-1, keepdims=True) + shift
                top = top_g if top is None else jnp.maximum(top, top_g)
            m_new = jnp.maximum(m_run, top)
            alpha = jnp.exp(m_run - m_new)
            l_run = alpha * l_run
            for gi, s, shift in s_pg:
                p = jnp.exp(s + (shift - m_new))
                l_run = l_run + jnp.sum(p, axis=-1, keepdims=True)
                p_ref[:, gi * cols:(gi + 1) * cols] = p.astype(BF16)
            al_ref[g] = alpha
            m_run = m_new
        m_ref[...] = m_run
        l_ref[...] = l_run

    @pl.when(pg == n_steps)
    def _():
        values()
        lam = _lambda(lq1_ref, lk1_ref, lq2_ref, lk2_ref, lam_init)
        for hd in range(N_HEADS):
            rs = slice(hd * grp, (hd + 1) * grp)
            n = acc_ref[rs, :] / l_ref[rs, :]
            o = n - lam * pltpu.roll(n, t_dec, 0)
            o_ref[0, rs, :] = _rms(o, gain_ref[...]) * (1.0 - lam_init)


def _attn_sample(page_table, q_rows, k_new, v_new, cache_k, cache_v, lams, gain, *, lam_init, t_dec):
    n_seq, n_pages = page_table.shape
    _, _, page, n_heads, hd = cache_k.shape
    n_pg = PAGES_PER_STEP
    assert n_pages % n_pg == 0 and n_heads == N_HEADS and hd == HEAD_DIM and 2 * t_dec == 8
    rows = 2 * t_dec * N_HEADS
    vec = lambda n: pl.BlockSpec((1, n), lambda s, p, pt: (0, 0))

    n_steps = n_pages // n_pg

    def page_spec(gi, lag):
        def index(s, p, pt):
            grp = jnp.maximum(p - 1, 0) if lag else jnp.minimum(p, n_steps - 1)
            return (0, pt[s, grp * n_pg + gi], 0, 0, 0)
        return pl.BlockSpec((None, None, page, n_heads, hd), index)

    grid_spec = pltpu.PrefetchScalarGridSpec(
        num_scalar_prefetch=1,
        grid=(n_seq, n_steps + 1),
        in_specs=[
            vec(QK_HALF), vec(QK_HALF), vec(QK_HALF), vec(QK_HALF), vec(HEAD_DIM),
            pl.BlockSpec((1, rows, hd), lambda s, p, pt: (s, 0, 0)),
            pl.BlockSpec((1, t_dec * n_heads, hd), lambda s, p, pt: (s, 0, 0)),
            pl.BlockSpec((1, t_dec * n_heads, hd), lambda s, p, pt: (s, 0, 0)),
        ] + [page_spec(gi, False) for gi in range(n_pg)] + [page_spec(gi, True) for gi in range(n_pg)],
        out_specs=pl.BlockSpec((1, rows, hd), lambda s, p, pt: (s, 0, 0)),
        scratch_shapes=[
            pltpu.VMEM((rows, hd), BF16),
            pltpu.VMEM((rows, 1), F32),
            pltpu.VMEM((rows, 1), F32),
            pltpu.VMEM((rows, hd), F32),
            pltpu.VMEM((rows, page * n_heads), F32),
            pltpu.VMEM((rows, n_pg * page * n_heads), BF16),
            pltpu.VMEM((n_pg // SAMPLE_GROUP, rows, 1), F32),
        ],
    )
    return pl.pallas_call(
        functools.partial(_attn_sample_body, lam_init=lam_init, t_dec=t_dec, past_len=n_pages * page),
        grid_spec=grid_spec,
        out_shape=jax.ShapeDtypeStruct((n_seq, rows, hd), F32),
        compiler_params=_params("parallel", "arbitrary", vmem_mib=48),
        name="attn_sample",
    )(page_table, *lams, gain, q_rows, k_new, v_new,
      *([cache_k] * n_pg), *([cache_v] * n_pg))


def _outproj_body(x_ref, a_ref, b_ref, w_ref, o_ref):
    half = a_ref.shape[1]
    o_ref[...] = (x_ref[...]
                  + jnp.dot(a_ref[...], w_ref[0:half, :], preferred_element_type=F32)
                  + jnp.dot(b_ref[...], w_ref[half:, :], preferred_element_type=F32))


def _outproj(x, pool_out, attn_out, w_out, *, tm):
    m, d = x.shape
    c = pool_out.shape[1]
    return pl.pallas_call(
        _outproj_body,
        grid=(m // tm,),
        in_specs=[
            pl.BlockSpec((tm, d), lambda i: (i, 0)),
            pl.BlockSpec((tm, c), lambda i: (i, 0)),
            pl.BlockSpec((tm, c), lambda i: (i, 0)),
            pl.BlockSpec(w_out.shape, lambda i: (0, 0)),
        ],
        out_specs=pl.BlockSpec((tm, d), lambda i: (i, 0)),
        out_shape=jax.ShapeDtypeStruct((m, d), F32),
        compiler_params=_params("parallel", vmem_mib=32),
        name="outproj",
    )(x, pool_out, attn_out, w_out)


def _outproj_pool_body(x_ref, p_ref, halo_ref, pm_ref, wp_ref, sc_ref, b_ref, w_ref, o_ref,
                       buf_ref, pool_ref, *, per_seq):
    i = pl.program_id(0)
    rows = p_ref.shape[0]
    first = (i % per_seq) == 0

    @pl.when(first)
    def _():
        buf_ref[0:HALO, :] = pm_ref[...]

    @pl.when(jnp.logical_not(first))
    def _():
        buf_ref[0:HALO, :] = halo_ref[...]

    buf_ref[HALO:HALO + rows, :] = p_ref[...]

    def write(cs, val):
        pool_ref[:, cs] = val.astype(BF16)

    _pool_groups(buf_ref, rows, wp_ref, sc_ref, lambda w: float(w), write)
    half = pool_ref.shape[1]
    o_ref[...] = (x_ref[...]
                  + jnp.dot(pool_ref[...], w_ref[0:half, :], preferred_element_type=F32)
                  + jnp.dot(b_ref[...], w_ref[half:, :], preferred_element_type=F32))


def _outproj_pool(x, p, p_meta, w_pool, pool_scale, attn_out, w_out, *, tm, seq):
    m, d = x.shape
    c = p.shape[1]
    hb = tm // HALO
    const2 = lambda i: (0, 0)
    return pl.pallas_call(
        functools.partial(_outproj_pool_body, per_seq=seq // tm),
        grid=(m // tm,),
        in_specs=[
            pl.BlockSpec((tm, d), lambda i: (i, 0)),
            pl.BlockSpec((tm, c), lambda i: (i, 0)),
            pl.BlockSpec((HALO, c), lambda i: (jnp.maximum(i * hb - 1, 0), 0)),
            pl.BlockSpec((HALO, c), const2),
            pl.BlockSpec(w_pool.shape, lambda i: (0, 0, 0)),
            pl.BlockSpec((1, c), const2),
            pl.BlockSpec((tm, c), lambda i: (i, 0)),
            pl.BlockSpec(w_out.shape, const2),
        ],
        out_specs=pl.BlockSpec((tm, d), lambda i: (i, 0)),
        out_shape=jax.ShapeDtypeStruct((m, d), F32),
        scratch_shapes=[pltpu.VMEM((HALO + tm, c), F32), pltpu.VMEM((tm, c), BF16)],
        compiler_params=_params("arbitrary", vmem_mib=52),
        name="outproj_pool",
    )(x, p, p, p_meta, w_pool, pool_scale, attn_out, w_out)


def kernel(x_prompt, x_sample, cache_k, cache_v, state_pool, page_table, meta_tokens,
           norm_ffn1, w_gate1, w_up1, w_down1, norm_mix, w_in, w_pool, pool_scale,
           lambda_q1, lambda_k1, lambda_q2, lambda_k2, subln_gain, w_out,
           norm_ffn2, w_gate2, w_up2, w_down2, norm_final):
    depth = w_in.shape[0]
    assert depth == 1
    bsz, seq, d = x_prompt.shape
    n_seq, t_dec, _ = x_sample.shape
    c = N_HEADS * HEAD_DIM
    n_small = N_META + n_seq * t_dec
    assert n_small <= SMALL_ROWS and 2 * t_dec == 8
    lam_init = 0.8 - 0.6 * math.exp(-0.3 * 0)

    row = lambda a: a.reshape(1, -1)
    w_in_b, w_out_b, w_pool_b = w_in[0].astype(BF16), w_out[0].astype(BF16), w_pool[0].astype(BF16)
    g1, gm, g2, gf = row(norm_ffn1[0]), row(norm_mix[0]), row(norm_ffn2[0]), row(norm_final)
    pscale, gain = row(pool_scale[0]), row(subln_gain[0])
    lams = (row(lambda_q1[0]), row(lambda_k1[0]), row(lambda_q2[0]), row(lambda_k2[0]))
    slopes = jnp.exp2(-jnp.arange(1, N_HEADS + 1, dtype=F32))

    x_big = x_prompt.reshape(bsz * seq, d)
    x_small = jnp.concatenate([meta_tokens, x_sample.reshape(n_seq * t_dec, d),
                               jnp.zeros((SMALL_ROWS - n_small, d), F32)], axis=0)

    xb1, xs1 = _ffn(x_big, x_small, g1, w_gate1[0], w_up1[0], w_down1[0], gf, final_norm=False)

    ps, ks, vs, kbs, qs, vbs = _proj(xs1, gm, w_in_b, tm=SMALL_ROWS, transposed=False)
    state16 = jnp.pad(state_pool[0], ((0, 0), (HALO - POOL_BUF, 0), (0, 0)))
    pool_s = _pool_small(ps, state16, w_pool_b, pscale, n_seq=n_seq, t_dec=t_dec).astype(BF16)
    o_meta = _attn_meta(qs, kbs, vbs, slopes, lams, gain, lam_init=lam_init)
    q_dec = qs[N_META:n_small].reshape(n_seq, t_dec, N_HEADS, HEAD_DIM).transpose(0, 2, 1, 3)
    q_rows = jnp.broadcast_to(q_dec[:, :, None], (n_seq, N_HEADS, 2, t_dec, HEAD_DIM))
    q_rows = q_rows.reshape(n_seq, 2 * t_dec * N_HEADS, HEAD_DIM)
    k_new = kbs[N_META:n_small].reshape(n_seq, t_dec * N_HEADS, HEAD_DIM)
    v_new = vbs[N_META:n_small].reshape(n_seq, t_dec * N_HEADS, HEAD_DIM)
    o_dec = _attn_sample(page_table, q_rows, k_new, v_new, cache_k, cache_v, lams, gain,
                         lam_init=lam_init, t_dec=t_dec)
    o_dec = o_dec.reshape(n_seq, N_HEADS, 2, t_dec, HEAD_DIM)[:, :, 0].transpose(0, 2, 1, 3)
    o_small = jnp.concatenate([o_meta, o_dec.reshape(n_seq * t_dec, c).astype(BF16),
                               jnp.zeros((SMALL_ROWS - n_small, c), BF16)], axis=0)
    xs2 = _outproj(xs1, pool_s, o_small, w_out_b, tm=SMALL_ROWS)

    pb, kb, vb, kbb, q_t, v_t = _proj(xb1, gm, w_in_b, tm=512, transposed=True, seq=seq)
    vm_t =vbs[:N_META].reshape(N_META, N_HEADS, HEAD_DIM).transpose(1, 2, 0)
    ones_rows = jnp.zeros((N_HEADS, V_ROWS - HEAD_DIM, N_META), BF16).at[:, 0].set(1.0)
    vm_t = jnp.concatenate([vm_t, ones_rows], axis=1)
    o_big = _attn_big(q_t, kbb, v_t, kbs, vm_t, slopes, lams, gain.reshape(HEAD_DIM, 1),
                      lam_init=lam_init, bsz=bsz)
    xb2 = _outproj_pool(xb1, pb, ps, w_pool_b, pscale, o_big, w_out_b, tm=512, seq=seq)
    yb, ys = _ffn(xb2, xs2, g2, w_gate2[0], w_up2[0], w_down2[0], gf, final_norm=True)

    k_rows, v_rows = _fill_meta(kb, vb, ks, vs, bsz=bsz)
    prompt_rows = lambda a: a.reshape(1, bsz, N_META + seq, N_HEADS, HEAD_DIM)
    dec = lambda a: a[N_META:n_small].reshape(1, n_seq, t_dec, N_HEADS, HEAD_DIM)
    y_prompt = yb.reshape(bsz, seq, d)
    y_sample = ys[N_META:n_small].reshape(n_seq, t_dec, d)
    pool_prompt = pb.reshape(bsz, seq, c)[:, seq - POOL_BUF:][None]
    pool_sample = jnp.concatenate([state_pool[0][:, t_dec:], ps[N_META:n_small].reshape(n_seq, t_dec, c)],
                                  axis=1)[None]
    return (y_prompt, y_sample, prompt_rows(k_rows), prompt_rows(v_rows), pool_prompt,
            dec(ks), dec(vs), pool_sample)
```

```python
import functools
import math

import jax
import jax.numpy as jnp
from jax import lax
from jax.experimental import pallas as pl
from jax.experimental.pallas import tpu as pltpu

F32 = jnp.float32
BF16 = jnp.bfloat16

N_META = 16
POOL_WINDOWS = (2, 4, 8, 16)
POOL_BUF = max(POOL_WINDOWS) - 1
HALO = 16
N_HEADS = 8
HEAD_DIM = 128
QK_HALF = HEAD_DIM // 2
EPS = 1e-6
NEG_INF = -1e30
LOG2_E = math.log2(math.e)
SMALL_ROWS = 128
ATTN_BLOCK = 256
V_ROWS = HEAD_DIM + 16
Q_GROUP = 4
KEY_SUB = 128
FFN_ROWS = 1024
FFN_CHUNK = 128
FFN_COLS = 256
SAMPLE_GROUP = 2
PAGES_PER_STEP = 16

VMEM_LIMIT = 60 * 1024 * 1024


def _params(*sem):
    return pltpu.CompilerParams(dimension_semantics=sem, vmem_limit_bytes=VMEM_LIMIT)


def _rms(x, g):
    return x * lax.rsqrt(jnp.mean(x * x, axis=-1, keepdims=True) + EPS) * g


def _ffn_body(xb_ref, xs_ref, g_ref, wg_ref, wu_ref, wd_ref, gf_ref, ob_ref, os_ref,
              hb_ref, hs_ref, wgb_ref, wub_ref, wdb_ref, *, final_norm):
    i, s = pl.program_id(0), pl.program_id(1)
    n_blk = pl.num_programs(1) - 1

    def round_weights():
        slot = s % 2
        wgb_ref[slot] = wg_ref[...].astype(BF16)
        wub_ref[slot] = wu_ref[...].astype(BF16)
        wdb_ref[slot] = wd_ref[...].astype(BF16)

    def row_chunks(ref):
        n = ref.shape[0]
        step = min(n, FFN_CHUNK)
        return [slice(r, r + step) for r in range(0, n, step)]

    def start(x_ref, o_ref, h_ref):
        for rs in row_chunks(x_ref):
            h_ref[rs, :] = _rms(x_ref[rs, :], g_ref[...]).astype(BF16)
            o_ref[rs, :] = jnp.zeros((rs.stop - rs.start, o_ref.shape[1]), F32)

    def accumulate(o_ref, h_ref):
        slot = (s + 1) % 2
        h = h_ref[...]
        g = jnp.dot(h, wgb_ref[slot], preferred_element_type=F32)
        u = jnp.dot(h, wub_ref[slot], preferred_element_type=F32)
        a = (g * jax.nn.sigmoid(g) * u).astype(BF16)
        o_ref[...] += jnp.dot(a, wdb_ref[slot], preferred_element_type=F32)

    def finish(x_ref, o_ref):
        for rs in row_chunks(x_ref):
            y = x_ref[rs, :] + 0.5 * o_ref[rs, :]
            if final_norm:
                y = _rms(y, gf_ref[...])
            o_ref[rs, :] = y

    last_tile = i == pl.num_programs(0) - 1

    @pl.when(s == 0)
    def _():
        round_weights()
        start(xb_ref, ob_ref, hb_ref)

        @pl.when(last_tile)
        def _():
            start(xs_ref, os_ref, hs_ref)

    @pl.when(s > 0)
    def _():
        accumulate(ob_ref, hb_ref)
        round_weights()

        @pl.when(last_tile)
        def _():
            accumulate(os_ref, hs_ref)

    @pl.when(s == n_blk)
    def _():
        finish(xb_ref, ob_ref)

        @pl.when(last_tile)
        def _():
            finish(xs_ref, os_ref)


def _ffn(x_big, x_small, g, wg, wu, wd, gf, *, final_norm):
    m, d = x_big.shape
    ms = x_small.shape[0]
    f = wg.shape[1]
    tm, tf = FFN_ROWS, FFN_COLS
    assert m % tm == 0 and f % tf == 0
    n_blk = f // tf
    const = lambda i, s: (0, 0)
    blk = lambda s: jnp.minimum(s, n_blk - 1)
    return pl.pallas_call(
        functools.partial(_ffn_body, final_norm=final_norm),
        grid=(m // tm, n_blk + 1),
        in_specs=[
            pl.BlockSpec((tm, d), lambda i, s: (i, 0), pipeline_mode=pl.Buffered(1)),
            pl.BlockSpec((ms, d), const),
            pl.BlockSpec((1, d), const),
            pl.BlockSpec((d, tf), lambda i, s: (0, blk(s))),
            pl.BlockSpec((d, tf), lambda i, s: (0, blk(s))),
            pl.BlockSpec((tf, d), lambda i, s: (blk(s), 0)),
            pl.BlockSpec((1, d), const),
        ],
        out_specs=[pl.BlockSpec((tm, d), lambda i, s: (i, 0)), pl.BlockSpec((ms, d), const)],
        out_shape=[jax.ShapeDtypeStruct((m, d), F32), jax.ShapeDtypeStruct((ms, d), F32)],
        scratch_shapes=[
            pltpu.VMEM((tm, d), BF16), pltpu.VMEM((ms, d), BF16),
            pltpu.VMEM((2, d, tf), BF16), pltpu.VMEM((2, d, tf), BF16), pltpu.VMEM((2, tf, d), BF16),
        ],
        compiler_params=_params("arbitrary", "arbitrary"),
        name="ffn",
    )(x_big, x_small, g, wg, wu, wd, gf)


def _proj_body(x_ref, g_ref, w_ref, p_ref, k_ref, v_ref, kb_ref, a_ref, b_ref, *, transposed):
    h = _rms(x_ref[...], g_ref[...]).astype(BF16)
    w = p_ref.shape[1]

    def block(j):
        return jnp.dot(h, w_ref[:, j * w:(j + 1) * w], preferred_element_type=F32)

    def put_t(dst, val, extra_ones):
        tc = dst.shape[3]
        for hd in range(N_HEADS):
            for ci in range(dst.shape[1]):
                blk = val[ci * tc:(ci + 1) * tc, hd * HEAD_DIM:(hd + 1) * HEAD_DIM]
                dst[hd, ci, 0:HEAD_DIM, :] = blk.T.astype(BF16)
                if extra_ones:
                    rid = lax.broadcasted_iota(jnp.int32, (dst.shape[2] - HEAD_DIM, tc), 0)
                    dst[hd, ci, HEAD_DIM:, :] = jnp.where(rid == 0, 1.0, 0.0).astype(BF16)

    p_ref[...] = block(0)
    q = block(1) * (QK_HALF ** -0.5)
    if transposed:
        put_t(a_ref, q * LOG2_E, False)
    else:
        a_ref[...] = q.astype(BF16)
    k = block(2)
    k_ref[...] = k
    kb_ref[...] = k.astype(BF16)
    v = block(3)
    v_ref[...] = v
    if transposed:
        put_t(b_ref, v, True)
    else:
        b_ref[...] = v.astype(BF16)


def _proj(x, g, w_in, *, tm, transposed, seq=None):
    m, d = x.shape
    w = w_in.shape[1] // 4
    row = lambda i: (i, 0)
    rm_spec = pl.BlockSpec((tm, w), row)
    rm_shape = jax.ShapeDtypeStruct((m, w), BF16)
    f32_shape = jax.ShapeDtypeStruct((m, w), F32)
    kv_spec, kv_shape = rm_spec, f32_shape
    if transposed:
        tc = ATTN_BLOCK
        t_spec = lambda n: pl.BlockSpec((N_HEADS, tm // tc, n, tc), lambda i: (0, i, 0, 0))
        t_shape = lambda n: jax.ShapeDtypeStruct((N_HEADS, m // tc, n, tc), BF16)
        extra_specs = [rm_spec, t_spec(HEAD_DIM), t_spec(V_ROWS)]
        extra_shapes = [rm_shape, t_shape(HEAD_DIM), t_shape(V_ROWS)]
        per_seq = seq // tm
        kv_spec = pl.BlockSpec(
            (pl.Element(tm), pl.Element(w)),
            lambda i: (pl.multiple_of(
                (i // per_seq) * (N_META + seq) + N_META + (i % per_seq) * tm, N_META), 0))
        kv_shape = jax.ShapeDtypeStruct((m // seq * (N_META + seq), w), F32)
    else:
        extra_specs = [rm_spec] * 3
        extra_shapes = [rm_shape] * 3
    return pl.pallas_call(
        functools.partial(_proj_body, transposed=transposed),
        grid=(m // tm,),
        in_specs=[
            pl.BlockSpec((tm, d), row),
            pl.BlockSpec((1, d), lambda i: (0, 0)),
            pl.BlockSpec(w_in.shape, lambda i: (0, 0), pipeline_mode=pl.Buffered(1)),
        ],
        out_specs=[rm_spec, kv_spec, kv_spec] + extra_specs,
        out_shape=[f32_shape, kv_shape, kv_shape] + extra_shapes,
        compiler_params=_params("parallel"),
        name="proj",
    )(x, g, w_in)


def _fill_meta_body(km_ref, vm_ref, k_in_ref, v_in_ref, k_ref, v_ref):
    del k_in_ref, v_in_ref
    k_ref[...] = km_ref[...]
    v_ref[...] = vm_ref[...]


def _fill_meta(k_rows, v_rows, k_meta, v_meta, *, bsz):
    m, w = k_rows.shape
    per_seq = m // bsz
    assert per_seq % N_META == 0
    meta_spec = pl.BlockSpec((N_META, w), lambda b: (0, 0))
    out_spec = pl.BlockSpec((N_META, w), lambda b: (b * (per_seq // N_META), 0))
    shape = jax.ShapeDtypeStruct((m, w), k_rows.dtype)
    return pl.pallas_call(
        _fill_meta_body,
        grid=(bsz,),
        in_specs=[meta_spec, meta_spec, pl.BlockSpec(memory_space=pl.ANY), pl.BlockSpec(memory_space=pl.ANY)],
        out_specs=[out_spec, out_spec],
        out_shape=[shape, shape],
        input_output_aliases={2: 0, 3: 1},
        compiler_params=_params("arbitrary"),
        name="fill_meta",
    )(k_meta, v_meta, k_rows, v_rows)


def _pool_groups(buf_ref, rows, wp_ref, sc_ref, count_fn, write):
    gd = wp_ref.shape[1]
    for gi, w in enumerate(POOL_WINDOWS):
        cs = slice(gi * gd, (gi + 1) * gd)
        cur = buf_ref[HALO:HALO + rows, cs]
        acc = cur
        for dlt in range(1, w):
            acc = acc + buf_ref[HALO - dlt:HALO - dlt + rows, cs]
        feat = acc / count_fn(w) - cur
        out = jnp.dot(feat.astype(BF16), wp_ref[gi], preferred_element_type=F32) * sc_ref[:, cs]
        write(cs, out)


def _pool_small_body(p_ref, st_ref, wp_ref, sc_ref, o_ref, buf_ref, *, n_seq, t_dec):
    buf_ref[0:HALO, :] = jnp.zeros((HALO, buf_ref.shape[1]), F32)
    buf_ref[HALO:2 * HALO, :] = p_ref[0:N_META, :]
    pos1 = lax.broadcasted_iota(jnp.int32, (N_META, 1), 0).astype(F32) + 1.0

    def write_meta(cs, val):
        o_ref[0:N_META, cs] = val

    _pool_groups(buf_ref, N_META, wp_ref, sc_ref, lambda w: jnp.minimum(pos1, float(w)), write_meta)

    o_ref[N_META + n_seq * t_dec:, :] = jnp.zeros(
        (o_ref.shape[0] - N_META - n_seq * t_dec, o_ref.shape[1]), F32)
    for s in range(n_seq):
        r0 = N_META + s * t_dec
        buf_ref[0:HALO, :] = st_ref[s]
        buf_ref[HALO:HALO + 8, :] = p_ref[r0:r0 + 8, :]

        def write_s(cs, val, r0=r0):
            o_ref[r0:r0 + t_dec, cs] = val[0:t_dec]

        _pool_groups(buf_ref, 8, wp_ref, sc_ref, lambda w: float(w), write_s)


def _pool_small(p_small, state16, w_pool, pool_scale, *, n_seq, t_dec):
    m, c = p_small.shape
    return pl.pallas_call(
        functools.partial(_pool_small_body, n_seq=n_seq, t_dec=t_dec),
        out_shape=jax.ShapeDtypeStruct((m, c), F32),
        scratch_shapes=[pltpu.VMEM((2 * HALO, c), F32)],
        compiler_params=pltpu.CompilerParams(vmem_limit_bytes=VMEM_LIMIT),
        name="pool_small",
    )(p_small, state16, w_pool, pool_scale)


def _lambda(lq1_ref, lk1_ref, lq2_ref, lk2_ref, lam_init):
    a = jnp.sum(lq1_ref[...] * lk1_ref[...], axis=-1, keepdims=True)
    b = jnp.sum(lq2_ref[...] * lk2_ref[...], axis=-1, keepdims=True)
    return jnp.exp(a) - jnp.exp(b) + lam_init


def _split_halves(q, axis):
    idx = lax.broadcasted_iota(jnp.int32, q.shape, axis) % HEAD_DIM
    zero = jnp.zeros_like(q)
    return jnp.where(idx < QK_HALF, q, zero), jnp.where(idx >= QK_HALF, q, zero)


def _nt_dot(a, b):
    return lax.dot_general(a, b, (((1,), (1,)), ((), ())), preferred_element_type=F32)


def _attn_big_body(lq1_ref, lk1_ref, lq2_ref, lk2_ref, gain_ref, pos_ref, posm_ref,
                   qt_ref, k_ref, vt_ref, km_ref, vmt_ref, o_ref,
                   m_ref, acc_ref, p_ref, al_ref, *, lam_init):
    blk = ATTN_BLOCK
    n_blk = qt_ref.shape[1]
    lam = _lambda(lq1_ref, lk1_ref, lq2_ref, lk2_ref, lam_init)
    diag = (lax.broadcasted_iota(jnp.int32, (blk, blk), 0)
            <= lax.broadcasted_iota(jnp.int32, (blk, blk), 1))
    nq = Q_GROUP

    def q_group(g, carry):
        q_ext = []
        for qh in range(nq):
            q = qt_ref[0, nq * g + qh].astype(F32)
            rid = lax.broadcasted_iota(jnp.int32, q.shape, 0)
            ones = jnp.where(rid < 3, 1.0, 0.0).astype(BF16)
            q_ext.append(tuple(
                jnp.concatenate([jnp.where(keep, q, 0.0).astype(BF16), ones], axis=0)
                for keep in (rid < QK_HALF, rid >= QK_HALF)))
        m_ref[...] = jnp.full(m_ref.shape, NEG_INF, F32)
        acc_ref[...] = jnp.zeros(acc_ref.shape, F32)
        p_ref[1] = jnp.zeros(p_ref.shape[1:], BF16)
        al_ref[1] = jnp.ones(al_ref.shape[1:], F32)

        def scores(slot, kb, kpos, targets):
            n_keys = kb.shape[0]
            sub = min(KEY_SUB, n_keys)
            for ki in range(n_keys // sub):
                rs = slice(ki * sub, (ki + 1) * sub)
                k_ext = jnp.concatenate([kb[rs], kpos[rs]], axis=1)
                for qh, mask in targets:
                    for mi in range(2):
                        st = 2 * qh + mi
                        s = jnp.dot(k_ext, q_ext[qh][mi], preferred_element_type=F32)
                        if mask is not None:
                            s = jnp.where(mask[rs], s, NEG_INF)
                        m_prev = m_ref[st]
                        m_new = jnp.maximum(m_prev, jnp.max(s, axis=0, keepdims=True))
                        al_ref[slot, ki, st] = jnp.exp2(m_prev - m_new)
                        p_ref[slot, st, rs, :] = jnp.exp2(s - m_new).astype(BF16)
                        m_ref[st] = m_new

        def values(slot, vt, first_q):
            n_keys = vt.shape[1]
            sub = min(KEY_SUB, n_keys)
            for ki in range(n_keys // sub):
                rs = slice(ki * sub, (ki + 1) * sub)
                for st in range(2 * first_q, 2 * nq):
                    acc_ref[st] = al_ref[slot, ki, st] * acc_ref[st] + jnp.dot(
                        vt[:, rs], p_ref[slot, st, rs, :], preferred_element_type=F32)

        def keys(kj):
            k0 = pl.multiple_of(kj * blk, blk)
            return k_ref[pl.ds(k0, blk), :], pos_ref[pl.ds(k0, blk), :]

        everyone = [(qh, None) for qh in range(nq)]
        first_key = nq * g

        def block_pair(t, c2):
            values(1, vt_ref[0, jnp.maximum(2 * t - 1, 0)], 0)
            scores(0, *keys(2 * t), everyone)
            values(0, vt_ref[0, 2 * t], 0)
            scores(1, *keys(2 * t + 1), everyone)
            return c2

        lax.fori_loop(0, first_key // 2, block_pair, 0)
        values(1, vt_ref[0, jnp.maximum(first_key - 1, 0)], 0)
        for j in range(nq):
            scores(j % 2, *keys(first_key + j), [(j, diag)] + everyone[j + 1:])
            values(j % 2, vt_ref[0, first_key + j], j)
        scores(nq % 2, km_ref[...], posm_ref[...], everyone)
        values(nq % 2, vmt_ref[0], 0)

        for qh in range(nq):
            a1, a2 = acc_ref[2 * qh], acc_ref[2 * qh + 1]
            o = (a1[0:HEAD_DIM] / a1[HEAD_DIM:HEAD_DIM + 1]
                 - lam * (a2[0:HEAD_DIM] / a2[HEAD_DIM:HEAD_DIM + 1]))
            o = o * lax.rsqrt(jnp.mean(o * o, axis=0, keepdims=True) + EPS) * gain_ref[...]
            r0 = pl.multiple_of((first_key + qh) * blk, blk)
            o_ref[pl.ds(r0, blk), :] = (o * (1.0 - lam_init)).T.astype(BF16)
        return carry

    assert n_blk % nq == 0 and nq % 2 == 0
    lax.fori_loop(0, n_blk // nq, q_group, 0)


def _position_blocks(slopes, first, n):
    shape = (slopes.shape[0], n, HEAD_DIM)
    pos = lax.broadcasted_iota(jnp.int32, shape, 1) + first
    col = lax.broadcasted_iota(jnp.int32, shape, 2)
    bias = LOG2_E * slopes[:, None, None] * pos.astype(F32)
    round_bf16 = lambda a: lax.reduce_precision(a, exponent_bits=8, mantissa_bits=7)
    t0 = round_bf16(bias)
    t1 = round_bf16(bias - t0)
    t2 = bias - t0 - t1
    return jnp.where(col == 0, t0, jnp.where(col == 1, t1, jnp.where(col == 2, t2, 0.0))).astype(BF16)


def _attn_big(q_t, kb, v_t, k_meta, vm_t, slopes, lams, gain_col, *, lam_init, bsz):
    n_heads, n_chunks, hd, blk = q_t.shape
    m, c = kb.shape
    seq = m // bsz
    n_blk = seq // blk
    vec = lambda n: pl.BlockSpec((1, n), lambda bi, h: (0, 0))
    pos = _position_blocks(slopes, 0, seq)
    pos_meta = _position_blocks(slopes, -N_META, N_META)
    return pl.pallas_call(
        functools.partial(_attn_big_body, lam_init=lam_init),
        grid=(bsz, n_heads),
        in_specs=[
            vec(QK_HALF), vec(QK_HALF), vec(QK_HALF), vec(QK_HALF),
            pl.BlockSpec((hd, 1), lambda bi, h: (0, 0)),
            pl.BlockSpec((None, seq, hd), lambda bi, h: (h, 0, 0)),
            pl.BlockSpec((None, N_META, hd), lambda bi, h: (h, 0, 0)),
            pl.BlockSpec((1, n_blk, hd, blk), lambda bi, h: (h, bi, 0, 0)),
            pl.BlockSpec((seq, hd), lambda bi, h: (bi, h)),
            pl.BlockSpec((1, n_blk, V_ROWS, blk), lambda bi, h: (h, bi, 0, 0)),
            pl.BlockSpec((N_META, hd), lambda bi, h: (0, h)),
            pl.BlockSpec((1, V_ROWS, N_META), lambda bi, h: (h, 0, 0)),
        ],
        out_specs=pl.BlockSpec((seq, hd), lambda bi, h: (bi, h)),
        out_shape=jax.ShapeDtypeStruct((m, c), BF16),
        scratch_shapes=[
            pltpu.VMEM((2 * Q_GROUP, 1, blk), F32),
            pltpu.VMEM((2 * Q_GROUP, V_ROWS, blk), F32),
            pltpu.VMEM((2, 2 * Q_GROUP, blk, blk), BF16),
            pltpu.VMEM((2, blk // KEY_SUB, 2 * Q_GROUP, 1, blk), F32),
        ],
        compiler_params=_params("parallel", "arbitrary"),
        name="attn_big",
    )(*lams, gain_col, pos, pos_meta, q_t, kb, v_t, k_meta, vm_t)


def _attn_meta_body(slope_ref, lq1_ref, lk1_ref, lq2_ref, lk2_ref, gain_ref,
                    q_ref, k_ref, v_ref, o_ref, *, lam_init):
    lam = _lambda(lq1_ref, lk1_ref, lq2_ref, lk2_ref, lam_init)
    shape = (N_META, k_ref.shape[0])
    qpos = lax.broadcasted_iota(jnp.int32, shape, 0)
    kpos = lax.broadcasted_iota(jnp.int32, shape, 1)
    dist = (qpos - kpos).astype(F32)
    mask = (qpos >= kpos) & (kpos < N_META)
    for hd in range(N_HEADS):
        cs = slice(hd * HEAD_DIM, (hd + 1) * HEAD_DIM)
        qs = _split_halves(q_ref[:, cs], 1)
        kb = k_ref[:, cs]
        outs = []
        for mi in range(2):
            s = jnp.where(mask, _nt_dot(qs[mi], kb) - slope_ref[hd] * dist, NEG_INF)
            p = jnp.exp(s - jnp.max(s, axis=-1, keepdims=True))
            p = p / jnp.sum(p, axis=-1, keepdims=True)
            outs.append(p)
        a = (outs[0] - lam * outs[1]).astype(BF16)
        o = jnp.dot(a, v_ref[:, cs], preferred_element_type=F32)
        o_ref[:, cs] = (_rms(o, gain_ref[...]) * (1.0 - lam_init)).astype(BF16)


def _attn_meta(q_s, kb_s, vb_s, slopes, lams, gain, *, lam_init):
    m, c = kb_s.shape
    vec = lambda n: pl.BlockSpec((1, n), lambda i: (0, 0))
    rows = lambda n: pl.BlockSpec((n, c), lambda i: (0, 0))
    return pl.pallas_call(
        functools.partial(_attn_meta_body, lam_init=lam_init),
        grid=(1,),
        in_specs=[
            pl.BlockSpec(memory_space=pltpu.SMEM),
            vec(QK_HALF), vec(QK_HALF), vec(QK_HALF), vec(QK_HALF), vec(HEAD_DIM),
            rows(N_META), rows(m), rows(m),
        ],
        out_specs=rows(N_META),
        out_shape=jax.ShapeDtypeStruct((N_META, c), BF16),
        compiler_params=_params("arbitrary"),
        name="attn_meta",
    )(slopes, *lams, gain, q_s, kb_s, vb_s)


def _attn_sample_body(pt_ref, lq1_ref, lk1_ref, lq2_ref, lk2_ref, gain_ref,
                      q_ref, kn_ref, vn_ref, *rest, lam_init, t_dec, past_len):
    n_pg = PAGES_PER_STEP
    k_refs, v_refs = rest[:n_pg], rest[n_pg:2 * n_pg]
    o_ref, qm_ref, m_ref, l_ref, acc_ref, bias_ref, p_ref, al_ref = rest[2 * n_pg:]
    pg = pl.program_id(1)
    n_steps = pl.num_programs(1) - 1
    page = k_refs[0].shape[0]
    grp = 2 * t_dec
    rows = grp * N_HEADS
    cols = page * N_HEADS
    row1 = lax.broadcasted_iota(jnp.int32, (rows, 1), 0)
    r_q = row1 % t_dec
    slope = jnp.exp2(-((row1 // grp) + 1).astype(F32))

    def update(kb, vb, bias, shift):
        s = _nt_dot(qm_ref[...], kb) + bias
        m_prev = m_ref[...]
        m_new = jnp.maximum(m_prev, jnp.max(s, axis=-1, keepdims=True) + shift)
        alpha = jnp.exp(m_prev - m_new)
        p = jnp.exp(s - (m_new - shift))
        l_ref[...] = alpha * l_ref[...] + jnp.sum(p, axis=-1, keepdims=True)
        acc_ref[...] = alpha * acc_ref[...] + jnp.dot(p.astype(BF16), vb, preferred_element_type=F32)
        m_ref[...] = m_new

    @pl.when(pg == 0)
    def _():
        q = q_ref[0]
        r2 = lax.broadcasted_iota(jnp.int32, q.shape, 0)
        c2 = lax.broadcasted_iota(jnp.int32, q.shape, 1)
        qm_ref[...] = jnp.where((c2 // QK_HALF) == (r2 % grp) // t_dec, q, jnp.zeros_like(q))
        m_ref[...] = jnp.full(m_ref.shape, NEG_INF, F32)
        l_ref[...] = jnp.zeros(l_ref.shape, F32)
        acc_ref[...] = jnp.zeros(acc_ref.shape, F32)
        p_ref[...] = jnp.zeros(p_ref.shape, BF16)
        al_ref[...] = jnp.ones(al_ref.shape, F32)
        rw = lax.broadcasted_iota(jnp.int32, (rows, cols), 0)
        cl = lax.broadcasted_iota(jnp.int32, (rows, cols), 1)
        bias_ref[...] = jnp.where((cl % N_HEADS) == (rw // grp),
                                  slope * (cl // N_HEADS).astype(F32), NEG_INF)
        rw = lax.broadcasted_iota(jnp.int32, (rows, t_dec * N_HEADS), 0)
        cl = lax.broadcasted_iota(jnp.int32, (rows, t_dec * N_HEADS), 1)
        r_k = cl // N_HEADS
        ok = ((cl % N_HEADS) == (rw // grp)) & (r_k <= rw % t_dec)
        bias_new = jnp.where(ok, -slope * (rw % t_dec - r_k).astype(F32), NEG_INF)
        update(kn_ref[0], vn_ref[0], bias_new, 0.0)

    n_grp = n_pg // SAMPLE_GROUP

    def values():
        acc = acc_ref[...]
        for g in range(n_grp):
            acc = al_ref[g] * acc
            for gi in range(g * SAMPLE_GROUP, (g + 1) * SAMPLE_GROUP):
                vb = v_refs[gi][...].reshape(cols, HEAD_DIM).astype(BF16)
                acc = acc + jnp.dot(p_ref[:, gi * cols:(gi + 1) * cols], vb, preferred_element_type=F32)
        acc_ref[...] = acc

    @pl.when(pg < n_steps)
    def _():
        values()
        qm = qm_ref[...]
        m_run, l_run = m_ref[...], l_ref[...]
        for g in range(n_grp):
            s_pg, top = [], None
            for gi in range(g * SAMPLE_GROUP, (g + 1) * SAMPLE_GROUP):
                start = (pg * n_pg + gi) * page
                shift = slope * (start - past_len - r_q).astype(F32)
                kb = k_refs[gi][...].reshape(cols, HEAD_DIM).astype(BF16)
                s = _nt_dot(qm, kb) + bias_ref[...]
                s_pg.append((gi, s, shift))
                top_g = jnp.max(s, axis=-1, keepdims=True) + shift
                top = top_g if top is None else jnp.maximum(top, top_g)
            m_new = jnp.maximum(m_run, top)
            alpha = jnp.exp(m_run - m_new)
            l_run = alpha * l_run
            for gi, s, shift in s_pg:
                p = jnp.exp(s + (shift - m_new))
                l_run = l_run + jnp.sum(p, axis=-1, keepdims=True)
                p_ref[:, gi * cols:(gi + 1) * cols] = p.astype(BF16)
            al_ref[g] = alpha
            m_run = m_new
        m_ref[...] = m_run
        l_ref[...] = l_run

    @pl.when(pg == n_steps)
    def _():
        values()
        lam = _lambda(lq1_ref, lk1_ref, lq2_ref, lk2_ref, lam_init)
        for hd in range(N_HEADS):
            rs = slice(hd * grp, (hd + 1) * grp)
            n = acc_ref[rs, :] / l_ref[rs, :]
            o = n - lam * pltpu.roll(n, t_dec, 0)
            o_ref[0, rs, :] = _rms(o, gain_ref[...]) * (1.0 - lam_init)


def _attn_sample(page_table, q_rows, k_new, v_new, cache_k, cache_v, lams, gain, *, lam_init, t_dec):
    n_seq, n_pages = page_table.shape
    _, _, page, n_heads, hd = cache_k.shape
    n_pg = PAGES_PER_STEP
    assert n_pages % n_pg == 0 and n_heads == N_HEADS and hd == HEAD_DIM and 2 * t_dec == 8
    rows = 2 * t_dec * N_HEADS
    vec = lambda n: pl.BlockSpec((1, n), lambda s, p, pt: (0, 0))

    n_steps = n_pages // n_pg

    def page_spec(gi, lag):
        def index(s, p, pt):
            grp = jnp.maximum(p - 1, 0) if lag else jnp.minimum(p, n_steps - 1)
            return (0, pt[s, grp * n_pg + gi], 0, 0, 0)
        return pl.BlockSpec((None, None, page, n_heads, hd), index)

    grid_spec = pltpu.PrefetchScalarGridSpec(
        num_scalar_prefetch=1,
        grid=(n_seq, n_steps + 1),
        in_specs=[
            vec(QK_HALF), vec(QK_HALF), vec(QK_HALF), vec(QK_HALF), vec(HEAD_DIM),
            pl.BlockSpec((1, rows, hd), lambda s, p, pt: (s, 0, 0)),
            pl.BlockSpec((1, t_dec * n_heads, hd), lambda s, p, pt: (s, 0, 0)),
            pl.BlockSpec((1, t_dec * n_heads, hd), lambda s, p, pt: (s, 0, 0)),
        ] + [page_spec(gi, False) for gi in range(n_pg)] + [page_spec(gi, True) for gi in range(n_pg)],
        out_specs=pl.BlockSpec((1, rows, hd), lambda s, p, pt: (s, 0, 0)),
        scratch_shapes=[
            pltpu.VMEM((rows, hd), BF16),
            pltpu.VMEM((rows, 1), F32),
            pltpu.VMEM((rows, 1), F32),
            pltpu.VMEM((rows, hd), F32),
            pltpu.VMEM((rows, page * n_heads), F32),
            pltpu.VMEM((rows, n_pg * page * n_heads), BF16),
            pltpu.VMEM((n_pg // SAMPLE_GROUP, rows, 1), F32),
        ],
    )
    return pl.pallas_call(
        functools.partial(_attn_sample_body, lam_init=lam_init, t_dec=t_dec, past_len=n_pages * page),
        grid_spec=grid_spec,
        out_shape=jax.ShapeDtypeStruct((n_seq, rows, hd), F32),
        compiler_params=_params("parallel", "arbitrary"),
        name="attn_sample",
    )(page_table, *lams, gain, q_rows, k_new, v_new,
      *([cache_k] * n_pg), *([cache_v] * n_pg))


def _outproj_body(x_ref, a_ref, b_ref, w_ref, o_ref):
    half = a_ref.shape[1]
    o_ref[...] = (x_ref[...]
                  + jnp.dot(a_ref[...], w_ref[0:half, :], preferred_element_type=F32)
                  + jnp.dot(b_ref[...], w_ref[half:, :], preferred_element_type=F32))


def _outproj(x, pool_out, attn_out, w_out, *, tm):
    m, d = x.shape
    c = pool_out.shape[1]
    return pl.pallas_call(
        _outproj_body,
        grid=(m // tm,),
        in_specs=[
            pl.BlockSpec((tm, d), lambda i: (i, 0)),
            pl.BlockSpec((tm, c), lambda i: (i, 0)),
            pl.BlockSpec((tm, c), lambda i: (i, 0)),
            pl.BlockSpec(w_out.shape, lambda i: (0, 0)),
        ],
        out_specs=pl.BlockSpec((tm, d), lambda i: (i, 0)),
        out_shape=jax.ShapeDtypeStruct((m, d), F32),
        compiler_params=_params("parallel"),
        name="outproj",
    )(x, pool_out, attn_out, w_out)


def _outproj_pool_body(x_ref, p_ref, halo_ref, pm_ref, wp_ref, sc_ref, b_ref, w_ref, o_ref,
                       buf_ref, pool_ref, *, per_seq):
    i = pl.program_id(0)
    rows = p_ref.shape[0]
    first = (i % per_seq) == 0

    @pl.when(first)
    def _():
        buf_ref[0:HALO, :] = pm_ref[...]

    @pl.when(jnp.logical_not(first))
    def _():
        buf_ref[0:HALO, :] = halo_ref[...]

    buf_ref[HALO:HALO + rows, :] = p_ref[...]

    def write(cs, val):
        pool_ref[:, cs] = val.astype(BF16)

    _pool_groups(buf_ref, rows, wp_ref, sc_ref, lambda w: float(w), write)
    half = pool_ref.shape[1]
    o_ref[...] = (x_ref[...]
                  + jnp.dot(pool_ref[...], w_ref[0:half, :], preferred_element_type=F32)
                  + jnp.dot(b_ref[...], w_ref[half:, :], preferred_element_type=F32))


def _outproj_pool(x, p, p_meta, w_pool, pool_scale, attn_out, w_out, *, tm, seq):
    m, d = x.shape
    c = p.shape[1]
    hb = tm // HALO
    const2 = lambda i: (0, 0)
    return pl.pallas_call(
        functools.partial(_outproj_pool_body, per_seq=seq // tm),
        grid=(m // tm,),
        in_specs=[
            pl.BlockSpec((tm, d), lambda i: (i, 0)),
            pl.BlockSpec((tm, c), lambda i: (i, 0)),
            pl.BlockSpec((HALO, c), lambda i: (jnp.maximum(i * hb - 1, 0), 0)),
            pl.BlockSpec((HALO, c), const2),
            pl.BlockSpec(w_pool.shape, lambda i: (0, 0, 0)),
            pl.BlockSpec((1, c), const2),
            pl.BlockSpec((tm, c), lambda i: (i, 0)),
            pl.BlockSpec(w_out.shape, const2),
        ],
        out_specs=pl.BlockSpec((tm, d), lambda i: (i, 0)),
        out_shape=jax.ShapeDtypeStruct((m, d), F32),
        scratch_shapes=[pltpu.VMEM((HALO + tm, c), F32), pltpu.VMEM((tm, c), BF16)],
        compiler_params=_params("arbitrary"),
        name="outproj_pool",
    )(x, p, p, p_meta, w_pool, pool_scale, attn_out, w_out)


def kernel(x_prompt, x_sample, cache_k, cache_v, state_pool, page_table, meta_tokens,
           norm_ffn1, w_gate1, w_up1, w_down1, norm_mix, w_in, w_pool, pool_scale,
           lambda_q1, lambda_k1, lambda_q2, lambda_k2, subln_gain, w_out,
           norm_ffn2, w_gate2, w_up2, w_down2, norm_final):
    depth = w_in.shape[0]
    assert depth == 1
    bsz, seq, d = x_prompt.shape
    n_seq, t_dec, _ = x_sample.shape
    c = N_HEADS * HEAD_DIM
    n_small = N_META + n_seq * t_dec
    assert n_small <= SMALL_ROWS and 2 * t_dec == 8
    lam_init = 0.8 - 0.6 * math.exp(-0.3 * 0)

    row = lambda a: a.reshape(1, -1)
    w_in_b, w_out_b, w_pool_b = w_in[0].astype(BF16), w_out[0].astype(BF16), w_pool[0].astype(BF16)
    g1, gm, g2, gf = row(norm_ffn1[0]), row(norm_mix[0]), row(norm_ffn2[0]), row(norm_final)
    pscale, gain = row(pool_scale[0]), row(subln_gain[0])
    lams = (row(lambda_q1[0]), row(lambda_k1[0]), row(lambda_q2[0]), row(lambda_k2[0]))
    slopes = jnp.exp2(-jnp.arange(1, N_HEADS + 1, dtype=F32))

    x_big = x_prompt.reshape(bsz * seq, d)
    x_small = jnp.concatenate([meta_tokens, x_sample.reshape(n_seq * t_dec, d),
                               jnp.zeros((SMALL_ROWS - n_small, d), F32)], axis=0)

    xb1, xs1 = _ffn(x_big, x_small, g1, w_gate1[0], w_up1[0], w_down1[0], gf, final_norm=False)

    ps, ks, vs, kbs, qs, vbs = _proj(xs1, gm, w_in_b, tm=SMALL_ROWS, transposed=False)
    state16 = jnp.pad(state_pool[0], ((0, 0), (HALO - POOL_BUF, 0), (0, 0)))
    pool_s = _pool_small(ps, state16, w_pool_b, pscale, n_seq=n_seq, t_dec=t_dec).astype(BF16)
    o_meta = _attn_meta(qs, kbs, vbs, slopes, lams, gain, lam_init=lam_init)
    q_dec = qs[N_META:n_small].reshape(n_seq, t_dec, N_HEADS, HEAD_DIM).transpose(0, 2, 1, 3)
    q_rows = jnp.broadcast_to(q_dec[:, :, None], (n_seq, N_HEADS, 2, t_dec, HEAD_DIM))
    q_rows = q_rows.reshape(n_seq, 2 * t_dec * N_HEADS, HEAD_DIM)
    k_new = kbs[N_META:n_small].reshape(n_seq, t_dec * N_HEADS, HEAD_DIM)
    v_new = vbs[N_META:n_small].reshape(n_seq, t_dec * N_HEADS, HEAD_DIM)
    o_dec = _attn_sample(page_table, q_rows, k_new, v_new, cache_k, cache_v, lams, gain,
                         lam_init=lam_init, t_dec=t_dec)
    o_dec = o_dec.reshape(n_seq, N_HEADS, 2, t_dec, HEAD_DIM)[:, :, 0].transpose(0, 2, 1, 3)
    o_small = jnp.concatenate([o_meta, o_dec.reshape(n_seq * t_dec, c).astype(BF16),
                               jnp.zeros((SMALL_ROWS - n_small, c), BF16)], axis=0)
    xs2 = _outproj(xs1, pool_s, o_small, w_out_b, tm=SMALL_ROWS)

    pb, kb, vb, kbb, q_t, v_t = _proj(xb1, gm, w_in_b, tm=512, transposed=True, seq=seq)
    vm_t =vbs[:N_META].reshape(N_META, N_HEADS, HEAD_DIM).transpose(1, 2, 0)
    ones_rows = jnp.zeros((N_HEADS, V_ROWS - HEAD_DIM, N_META), BF16).at[:, 0].set(1.0)
    vm_t = jnp.concatenate([vm_t, ones_rows], axis=1)
    o_big = _attn_big(q_t, kbb, v_t, kbs, vm_t, slopes, lams, gain.reshape(HEAD_DIM, 1),
                      lam_init=lam_init, bsz=bsz)
    xb2 = _outproj_pool(xb1, pb, ps, w_pool_b, pscale, o_big, w_out_b, tm=512, seq=seq)
    yb, ys = _ffn(xb2, xs2, g2, w_gate2[0], w_up2[0], w_down2[0], gf, final_norm=True)

    k_rows, v_rows = _fill_meta(kb, vb, ks, vs, bsz=bsz)
    prompt_rows = lambda a: a.reshape(1, bsz, N_META + seq, N_HEADS, HEAD_DIM)
    dec = lambda a: a[N_META:n_small].reshape(1, n_seq, t_dec, N_HEADS, HEAD_DIM)
    y_prompt = yb.reshape(bsz, seq, d)
    y_sample = ys[N_META:n_small].reshape(n_seq, t_dec, d)
    pool_prompt = pb.reshape(bsz, seq, c)[:, seq - POOL_BUF:][None]
    pool_sample = jnp.concatenate([state_pool[0][:, t_dec:], ps[N_META:n_small].reshape(n_seq, t_dec, c)],
                                  axis=1)[None]
    return (y_prompt, y_sample, prompt_rows(k_rows), prompt_rows(v_rows), pool_prompt,
            dec(ks), dec(vs), pool_sample)
```
